```python
import jax
import jax.numpy as jnp
from jax import lax
import numpy as np

D_MODEL = 1024
BATCH = 4
SEQ = 8192
DEPTH = 1

GM_GROUPS = 4
GM_DIM = 128
GM_WIDTH = GM_GROUPS * GM_DIM
GM_CHUNK = 128
GDN_HEADS = 4
GDN_DK = 128
GDN_DV = 128
GDN_QK_WIDTH = GDN_HEADS * GDN_DK
GDN_V_WIDTH = GDN_HEADS * GDN_DV
GDN_CONV = 4
GDN_CHUNK = 64
D_MIX = GM_WIDTH + GDN_V_WIDTH
IN_SPLITS = (GM_WIDTH, 2 * GM_WIDTH, 2 * GM_WIDTH + GDN_QK_WIDTH, 2 * GM_WIDTH + 2 * GDN_QK_WIDTH, 2 * GM_WIDTH + 2 * GDN_QK_WIDTH + GDN_V_WIDTH, 2 * GM_WIDTH + 2 * GDN_QK_WIDTH + 2 * GDN_V_WIDTH, 2 * GM_WIDTH + 2 * GDN_QK_WIDTH + 2 * GDN_V_WIDTH + GDN_HEADS)
IN_COLS = IN_SPLITS[-1] + GDN_HEADS
N_EXPERTS = 32
TOP_K = 4
D_FF = D_MODEL
SWIGLU_LIMIT = 7.0
SWIGLU_ALPHA = 1.702
MOE_BLOCK = 256
N_MOD = 6
EPS = 1e-6

kernel_name = 'hybrid_gmlp_gdn_moe_adaln'


def _rmsnorm(x, w):
    xf = x.astype(jnp.float32)
    y = xf * lax.rsqrt(jnp.mean(xf * xf, axis=-1, keepdims=True) + EPS)
    return y.astype(x.dtype) * w


def _l2norm(x):
    return x * lax.rsqrt(jnp.sum(x * x, axis=-1, keepdims=True) + EPS)


def _modulate(x, w, shift, scale):
    return _rmsnorm(x, w) * (1 + scale[:, None, :]) + shift[:, None, :]


def _chunked_gmlp(u, v, v_norm_w, w_spatial, b_spatial):
    bsz, seq = u.shape[:2]
    n_chunks = seq // GM_CHUNK
    u = jax.nn.gelu(u, approximate=False)
    v = _rmsnorm(jax.nn.gelu(v, approximate=False), v_norm_w)
    causal = jnp.tril(jnp.ones((GM_CHUNK, GM_CHUNK), dtype=bool))
    ws = jnp.where(causal[None], w_spatial, jnp.zeros_like(w_spatial))
    vc = v.reshape(bsz, n_chunks, GM_CHUNK, GM_GROUPS, GM_DIM)
    z = jnp.einsum('gts,bnsgd->bntgd', ws, vc) + b_spatial.T[None, None, :, :, None]
    return (u * z.reshape(bsz, seq, GM_GROUPS, GM_DIM)).reshape(bsz, seq, GM_WIDTH)


def _chunk_gated_delta(q, k, v, g, beta):
    bsz, seq, nh, dk = q.shape
    dv = v.shape[-1]
    n = seq // GDN_CHUNK

    def to_chunks(t):
        return t.reshape(bsz, n, GDN_CHUNK, nh, -1).transpose(0, 3, 1, 2, 4)

    q, k, v = to_chunks(q), to_chunks(k), to_chunks(v)
    gc = jnp.cumsum(g.reshape(bsz, n, GDN_CHUNK, nh).transpose(0, 3, 1, 2), axis=-1)
    beta = beta.reshape(bsz, n, GDN_CHUNK, nh).transpose(0, 3, 1, 2)[..., None]
    tri = jnp.tril(jnp.ones((GDN_CHUNK, GDN_CHUNK), dtype=bool))
    strict = jnp.tril(jnp.ones((GDN_CHUNK, GDN_CHUNK), dtype=bool), k=-1)
    decay = jnp.exp(jnp.where(tri, gc[..., :, None] - gc[..., None, :], -jnp.inf))
    kb = k * beta
    a_mat = jnp.where(strict, jnp.einsum('bhnid,bhnjd->bhnij', kb, k) * decay, 0.0)
    eye = jnp.eye(GDN_CHUNK, dtype=q.dtype)
    rhs = jnp.concatenate([v * beta, kb * jnp.exp(gc)[..., None]], axis=-1)
    sol = lax.linalg.triangular_solve(a_mat + eye, rhs, left_side=True, lower=True, unit_diagonal=True)
    u_c, w_c = sol[..., :dv], sol[..., dv:]
    qk = jnp.where(tri, jnp.einsum('bhnid,bhnjd->bhnij', q, k) * decay, 0.0)

    def step(state, xs):
        q_i, k_i, u_i, w_i, g_i, qk_i = xs
        v_new = u_i - jnp.einsum('bhcd,bhde->bhce', w_i, state)
        o_i = jnp.einsum('bhcd,bhde->bhce', q_i * jnp.exp(g_i)[..., None], state) + jnp.einsum('bhij,bhje->bhie', qk_i, v_new)
        g_last = g_i[..., -1]
        k_dec = k_i * jnp.exp(g_last[..., None] - g_i)[..., None]
        state = state * jnp.exp(g_last)[..., None, None] + jnp.einsum('bhcd,bhce->bhde', k_dec, v_new)
        return state, o_i

    def front(t):
        return jnp.moveaxis(t, 2, 0)

    state0 = jnp.zeros((bsz, nh, dk, dv), q.dtype)
    _, o = lax.scan(step, state0, (front(q), front(k), front(u_c), front(w_c), front(gc), front(qk)))
    return o.transpose(1, 0, 3, 2, 4).reshape(bsz, seq, nh, dv)


def _gated_deltanet(q, k, v, a, b, z, conv_w, a_log, dt_bias, o_norm_w):
    dtype = v.dtype
    bsz, seq, _ = q.shape
    qkv = jnp.concatenate([q, k, v], axis=-1)
    n_ch = qkv.shape[-1]
    qkv = lax.conv_general_dilated(qkv, conv_w[:, None, :], window_strides=(1,), padding=[(GDN_CONV - 1, 0)], dimension_numbers=('NWC', 'WIO', 'NWC'), feature_group_count=n_ch)
    qkv = jax.nn.silu(qkv).astype(jnp.float32)
    q, k, v = jnp.split(qkv, [GDN_QK_WIDTH, 2 * GDN_QK_WIDTH], axis=-1)
    q = _l2norm(q.reshape(bsz, seq, GDN_HEADS, GDN_DK)) * (GDN_DK ** -0.5)
    k = _l2norm(k.reshape(bsz, seq, GDN_HEADS, GDN_DK))
    v = v.reshape(bsz, seq, GDN_HEADS, GDN_DV)
    beta = jax.nn.sigmoid(b.astype(jnp.float32))
    g = -jnp.exp(a_log.astype(jnp.float32)) * jax.nn.softplus(a.astype(jnp.float32) + dt_bias.astype(jnp.float32))
    o = _chunk_gated_delta(q, k, v, g, beta).astype(dtype)
    o = _rmsnorm(o, o_norm_w) * jax.nn.silu(z.reshape(bsz, seq, GDN_HEADS, GDN_DV))
    return o.reshape(bsz, seq, GDN_V_WIDTH)


def _moe(h, w_router, b_router, w_gu, b_gu, w_down, b_down):
    bsz, seq, d = h.shape
    n_tok = bsz * seq
    hf = h.reshape(n_tok, d)
    logits = (hf @ w_router + b_router).astype(jnp.float32)
    top_val, top_idx = lax.top_k(logits, TOP_K)
    weights = jax.nn.softmax(top_val, axis=-1).astype(h.dtype)
    n_assign = n_tok * TOP_K
    e_flat = top_idx.reshape(-1)
    tok_flat = jnp.repeat(jnp.arange(n_tok, dtype=jnp.int32), TOP_K)
    w_flat = weights.reshape(-1)
    order = jnp.argsort(e_flat)
    e_sorted = e_flat[order]
    counts = jnp.bincount(e_flat, length=N_EXPERTS)
    padded = (counts + MOE_BLOCK - 1) // MOE_BLOCK * MOE_BLOCK
    start = jnp.cumsum(counts) - counts
    pad_end = jnp.cumsum(padded)
    pad_start = pad_end - padded
    dest = pad_start[e_sorted] + jnp.arange(n_assign, dtype=jnp.int32) - start[e_sorted]
    n_slots = n_assign + N_EXPERTS * MOE_BLOCK
    n_blocks = n_slots // MOE_BLOCK
    slot_tok = jnp.full((n_slots,), n_tok, jnp.int32).at[dest].set(tok_flat[order])
    slot_w = jnp.zeros((n_slots,), h.dtype).at[dest].set(w_flat[order])
    block_e = jnp.minimum(jnp.searchsorted(pad_end, jnp.arange(n_blocks, dtype=jnp.int32) * MOE_BLOCK, side='right'), N_EXPERTS - 1)
    h_pad = jnp.concatenate([hf, jnp.zeros((1, d), hf.dtype)], axis=0)

    def expert_block(args):
        tok, wt, e = args
        xb = h_pad[tok]
        gu = xb @ w_gu[e] + b_gu[e]
        gate = jnp.minimum(gu[:, :D_FF], SWIGLU_LIMIT)
        up = jnp.clip(gu[:, D_FF:], -SWIGLU_LIMIT, SWIGLU_LIMIT)
        act = gate * jax.nn.sigmoid(SWIGLU_ALPHA * gate) * (up + 1)
        return (act @ w_down[e] + b_down[e]) * wt[:, None]

    y = lax.map(expert_block, (slot_tok.reshape(n_blocks, MOE_BLOCK), slot_w.reshape(n_blocks, MOE_BLOCK), block_e))
    out = jax.ops.segment_sum(y.reshape(n_slots, d), slot_tok, num_segments=n_tok + 1)[:n_tok]
    return out.reshape(bsz, seq, d)


def setup_inputs(seed: int = 0) -> dict:
    key = jax.random.key(seed)
    ks = jax.random.split(key, 24)
    L, D = DEPTH, D_MODEL

    def nrm(k, shape, s):
        return s * jax.random.normal(k, shape, jnp.float32)

    return {
        'x': nrm(ks[0], (BATCH, SEQ, D), 1.0),
        'c': nrm(ks[1], (BATCH, D), 1.0),
        'w_ada': nrm(ks[2], (L, D, N_MOD * D), 0.5 * D ** -0.5),
        'b_ada': nrm(ks[3], (L, N_MOD * D), 0.02),
        'norm1_w': 1.0 + nrm(ks[4], (L, D), 0.05),
        'w_in': nrm(ks[5], (L, D, IN_COLS), D ** -0.5),
        'gm_vnorm_w': 1.0 + nrm(ks[6], (L, GM_GROUPS, GM_DIM), 0.05),
        'gm_w_spatial': nrm(ks[7], (L, GM_GROUPS, GM_CHUNK, GM_CHUNK), GM_CHUNK ** -0.5),
        'gm_b_spatial': 1.0 + nrm(ks[8], (L, GM_GROUPS, GM_CHUNK), 0.1),
        'gdn_conv_w': nrm(ks[9], (L, GDN_CONV, 2 * GDN_QK_WIDTH + GDN_V_WIDTH), GDN_CONV ** -0.5),
        'gdn_a_log': jnp.log(jax.random.uniform(ks[10], (L, GDN_HEADS), jnp.float32, 1.0, 16.0)),
        'gdn_dt_bias': jnp.log(jnp.expm1(jax.random.uniform(ks[11], (L, GDN_HEADS), jnp.float32, 1e-3, 1e-1))),
        'gdn_onorm_w': 1.0 + nrm(ks[12], (L, GDN_DV), 0.05),
        'w_out': nrm(ks[13], (L, D_MIX, D), D_MIX ** -0.5),
        'norm2_w': 1.0 + nrm(ks[14], (L, D), 0.05),
        'w_router': nrm(ks[15], (L, D, N_EXPERTS), D ** -0.5),
        'b_router': nrm(ks[16], (L, N_EXPERTS), 0.01),
        'w_gu': nrm(ks[17], (L, N_EXPERTS, D, 2 * D_FF), D ** -0.5),
        'b_gu': nrm(ks[18], (L, N_EXPERTS, 2 * D_FF), 0.02),
        'w_down': nrm(ks[19], (L, N_EXPERTS, D_FF, D), D_FF ** -0.5),
        'b_down': nrm(ks[20], (L, N_EXPERTS, D), 0.02),
        'norm_f_w': 1.0 + nrm(ks[21], (D,), 0.05),
    }


def reference(x, c, w_ada, b_ada, norm1_w, w_in, gm_vnorm_w, gm_w_spatial, gm_b_spatial, gdn_conv_w, gdn_a_log, gdn_dt_bias, gdn_onorm_w, w_out, norm2_w, w_router, b_router, w_gu, b_gu, w_down, b_down, norm_f_w):
    bsz, seq, _ = x.shape
    c_act = jax.nn.silu(c)
    for l in range(DEPTH):
        mod = c_act @ w_ada[l] + b_ada[l]
        sh1, sc1, g1, sh2, sc2, g2 = jnp.split(mod, N_MOD, axis=-1)
        h = _modulate(x, norm1_w[l], sh1, sc1)
        proj = h @ w_in[l]
        gm_u, gm_v, q, k, v, z, a, b = jnp.split(proj, IN_SPLITS, axis=-1)
        y_a = _chunked_gmlp(gm_u.reshape(bsz, seq, GM_GROUPS, GM_DIM), gm_v.reshape(bsz, seq, GM_GROUPS, GM_DIM), gm_vnorm_w[l], gm_w_spatial[l], gm_b_spatial[l])
        y_b = _gated_deltanet(q, k, v, a, b, z, gdn_conv_w[l], gdn_a_log[l], gdn_dt_bias[l], gdn_onorm_w[l])
        mix = jnp.concatenate([y_a, y_b], axis=-1) @ w_out[l]
        x = x + g1[:, None, :] * mix
        h = _modulate(x, norm2_w[l], sh2, sc2)
        x = x + g2[:, None, :] * _moe(h, w_router[l], b_router[l], w_gu[l], b_gu[l], w_down[l], b_down[l])
    return _rmsnorm(x, norm_f_w)
```

```python
import functools

import jax
import jax.numpy as jnp
from jax import lax
from jax.experimental import pallas as pl
from jax.experimental.pallas import tpu as pltpu

F32 = jnp.float32
BF16 = jnp.bfloat16
HIGHEST = lax.Precision.HIGHEST

D_MODEL = 1024
GM_GROUPS = 4
GM_DIM = 128
GM_WIDTH = GM_GROUPS * GM_DIM
GM_CHUNK = 128
GDN_HEADS = 4
GDN_DK = 128
GDN_DV = 128
GDN_WIDTH = GDN_HEADS * GDN_DK
GDN_CONV = 4
GDN_CHUNK = 64
N_EXPERTS = 32
TOP_K = 4
D_FF = D_MODEL
SWIGLU_LIMIT = 7.0
SWIGLU_ALPHA = 1.702
N_MOD = 6
EPS = 1e-6

LANES = 128
SUBLANES = 8
PROJ_MAIN = 2 * GM_WIDTH + 4 * GDN_WIDTH

TM_INPROJ = 512
TM_PREP = 256
TM_SCAN = 256
TM_ROUTER = 256
MOE_BLOCK = 256
TM_COMBINE = 128
FF_TILE = 512
VMEM_LIMIT = 48 * 1024 * 1024


def _dot(a, b):
    return jnp.dot(a, b, preferred_element_type=F32)


def _dot_nt(a, b):
    return lax.dot_general(a, b, (((1,), (1,)), ((), ())), preferred_element_type=F32)


def _dot_tn(a, b):
    return lax.dot_general(a, b, (((0,), (0,)), ((), ())), preferred_element_type=F32)


def _rms(x, w):
    return x * lax.rsqrt(jnp.mean(x * x, axis=-1, keepdims=True) + EPS) * w


def _gelu(x):
    return 0.5 * x * (1.0 + lax.erf(x * (2.0 ** -0.5)))


def _silu(x):
    return x * jax.nn.sigmoid(x)


def _adaln_kernel(c_ref, w_ref, b_ref, o_ref):
    c = c_ref[...]
    o_ref[...] = jnp.dot(_silu(c), w_ref[...], precision=HIGHEST,
                         preferred_element_type=F32) + b_ref[...]


def _adaln(c_pad, w_ada, b_ada):
    rows = c_pad.shape[0]
    n_out = w_ada.shape[1]
    return pl.pallas_call(
        _adaln_kernel,
        grid=(n_out // D_MODEL,),
        in_specs=[
            pl.BlockSpec((rows, D_MODEL), lambda j: (0, 0)),
            pl.BlockSpec((D_MODEL, D_MODEL), lambda j: (0, j)),
            pl.BlockSpec((1, D_MODEL), lambda j: (0, j)),
        ],
        out_specs=pl.BlockSpec((rows, D_MODEL), lambda j: (0, j)),
        out_shape=jax.ShapeDtypeStruct((rows, n_out), F32),
        name="adaln",
    )(c_pad, w_ada, b_ada)


def _inproj_kernel(x_ref, mod_ref, nw_ref, w_ref, wab_ref, proj_ref, ab_ref):
    h = _rms(x_ref[...], nw_ref[...]) * (1.0 + mod_ref[0, 1:2, :]) + mod_ref[0, 0:1, :]
    hb = h.astype(BF16)
    for j in range(PROJ_MAIN // 512):
        cols = slice(j * 512, (j + 1) * 512)
        proj_ref[:, cols] = _dot(hb, w_ref[:, cols]).astype(BF16)
    ab_ref[...] = _dot(hb, wab_ref[...])


def _inproj(x2d, mod3, norm_w, w_main, w_ab, seq):
    n = x2d.shape[0]
    tiles_per_batch = seq // TM_INPROJ
    return pl.pallas_call(
        _inproj_kernel,
        grid=(n // TM_INPROJ,),
        in_specs=[
            pl.BlockSpec((TM_INPROJ, D_MODEL), lambda i: (i, 0)),
            pl.BlockSpec((1, N_MOD, D_MODEL), lambda i: (i // tiles_per_batch, 0, 0)),
            pl.BlockSpec((1, D_MODEL), lambda i: (0, 0)),
            pl.BlockSpec((D_MODEL, PROJ_MAIN), lambda i: (0, 0)),
            pl.BlockSpec((D_MODEL, LANES), lambda i: (0, 0)),
        ],
        out_specs=[
            pl.BlockSpec((TM_INPROJ, PROJ_MAIN), lambda i: (i, 0)),
            pl.BlockSpec((TM_INPROJ, LANES), lambda i: (i, 0)),
        ],
        out_shape=[
            jax.ShapeDtypeStruct((n, PROJ_MAIN), BF16),
            jax.ShapeDtypeStruct((n, LANES), F32),
        ],
        compiler_params=pltpu.CompilerParams(
            dimension_semantics=("parallel",), vmem_limit_bytes=VMEM_LIMIT),
        name="in_proj",
    )(x2d, mod3, norm_w, w_main, w_ab)


def _unit_lower_inverse(a):
    c = a.shape[0]
    row = lax.broadcasted_iota(jnp.int32, (c, c), 0)
    col = lax.broadcasted_iota(jnp.int32, (c, c), 1)
    p = jnp.where(row == col, 1.0, 0.0).astype(F32) - a
    ab = a.astype(BF16)
    q = _dot(ab, ab)
    power = 2
    while 2 * power < c:
        qb = q.astype(BF16)
        p = p + _dot(p.astype(BF16), qb)
        q = _dot(qb, qb)
        power *= 2
    return p + _dot(p.astype(BF16), q.astype(BF16))


def _mix_prep_kernel(proj_ref, ab_ref, vnw_ref, wsp_ref, bsp_ref, cw_ref, alog_ref, dtb_ref,
                     ya_ref, qt_ref, kt_ref, w_ref, u_ref, qk_ref, dec_ref, ext_ref,
                     *, tiles_per_batch):
    tm = TM_PREP
    i = pl.program_id(0)

    row = lax.broadcasted_iota(jnp.int32, (GM_CHUNK, GM_CHUNK), 0)
    col = lax.broadcasted_iota(jnp.int32, (GM_CHUNK, GM_CHUNK), 1)
    causal = row >= col
    for g in range(GM_GROUPS):
        ws = jnp.where(causal, wsp_ref[g], 0.0).astype(BF16)
        bcol = bsp_ref[:, g:g + 1]
        cols_u = slice(g * GM_DIM, (g + 1) * GM_DIM)
        cols_v = slice(GM_WIDTH + g * GM_DIM, GM_WIDTH + (g + 1) * GM_DIM)
        for c in range(tm // GM_CHUNK):
            rows = slice(c * GM_CHUNK, (c + 1) * GM_CHUNK)
            u = _gelu(proj_ref[rows, cols_u].astype(F32))
            v = _rms(_gelu(proj_ref[rows, cols_v].astype(F32)), vnw_ref[g:g + 1, :])
            z = _dot(ws, v.astype(BF16)) + bcol
            ya_ref[rows, cols_u] = (u * z).astype(BF16)

    @pl.when(i % tiles_per_batch == 0)
    def _():
        ext_ref[0:SUBLANES, :] = jnp.zeros((SUBLANES, 3 * GDN_WIDTH), F32)

    qkv_cols = slice(2 * GM_WIDTH, 2 * GM_WIDTH + 3 * GDN_WIDTH)
    ext_ref[SUBLANES:SUBLANES + tm, :] = proj_ref[:, qkv_cols].astype(F32)
    conv = jnp.zeros((tm, 3 * GDN_WIDTH), F32)
    for j in range(GDN_CONV):
        start = SUBLANES - (GDN_CONV - 1) + j
        conv = conv + cw_ref[j:j + 1, :] * ext_ref[start:start + tm, :]
    ext_ref[0:SUBLANES, :] = ext_ref[tm:tm + SUBLANES, :]
    act = _silu(conv)

    ab = ab_ref[...]
    g_all = -jnp.exp(alog_ref[...]) * (
        jnp.maximum(ab + dtb_ref[...], 0.0) + jnp.log1p(jnp.exp(-jnp.abs(ab + dtb_ref[...]))))
    beta_all = jax.nn.sigmoid(ab)
    trow = lax.broadcasted_iota(jnp.int32, (tm, tm), 0)
    tcol = lax.broadcasted_iota(jnp.int32, (tm, tm), 1)
    blk_lower = jnp.where((trow >= tcol) & (trow // GDN_CHUNK == tcol // GDN_CHUNK), 1.0, 0.0)
    gc_all = jnp.dot(blk_lower.astype(F32), g_all, precision=HIGHEST,
                     preferred_element_type=F32)
    gc_all_t = gc_all.T

    crow = lax.broadcasted_iota(jnp.int32, (GDN_CHUNK, GDN_CHUNK), 0)
    ccol = lax.broadcasted_iota(jnp.int32, (GDN_CHUNK, GDN_CHUNK), 1)
    tri = crow >= ccol
    strict = crow > ccol

    for h in range(GDN_HEADS):
        hq = slice(h * GDN_DK, (h + 1) * GDN_DK)
        hk = slice(GDN_WIDTH + h * GDN_DK, GDN_WIDTH + (h + 1) * GDN_DK)
        hv = slice(2 * GDN_WIDTH + h * GDN_DV, 2 * GDN_WIDTH + (h + 1) * GDN_DV)
        q_h = act[:, hq]
        k_h = act[:, hk]
        q_h = q_h * lax.rsqrt(jnp.sum(q_h * q_h, axis=-1, keepdims=True) + EPS) * (GDN_DK ** -0.5)
        k_h = k_h * lax.rsqrt(jnp.sum(k_h * k_h, axis=-1, keepdims=True) + EPS)
        v_h = act[:, hv]
        beta_h = beta_all[:, GDN_HEADS + h:GDN_HEADS + h + 1]
        gc_h = gc_all[:, h:h + 1]
        for c in range(tm // GDN_CHUNK):
            rows = slice(c * GDN_CHUNK, (c + 1) * GDN_CHUNK)
            q = q_h[rows]
            k = k_h[rows]
            v = v_h[rows]
            beta = beta_h[rows]
            gcc = gc_h[rows]
            gcr = gc_all_t[h:h + 1, c * GDN_CHUNK:(c + 1) * GDN_CHUNK]
            decay = jnp.where(tri, jnp.exp(jnp.where(tri, gcc - gcr, 0.0)), 0.0)
            kb = k * beta
            kbf = k.astype(BF16)
            a_mat = jnp.where(strict, _dot_nt(kb.astype(BF16), kbf) * decay, 0.0)
            t_inv = _unit_lower_inverse(a_mat)
            egc = jnp.exp(gcc)
            rhs = jnp.concatenate([v * beta, kb * egc], axis=1).astype(BF16)
            sol = _dot(t_inv.astype(BF16), rhs)
            qk = jnp.where(tri, _dot_nt(q.astype(BF16), kbf) * decay, 0.0)
            g_last = gc_h[(c + 1) * GDN_CHUNK - 1:(c + 1) * GDN_CHUNK]
            out_cols = slice(h * GDN_DK, (h + 1) * GDN_DK)
            u_ref[rows, out_cols] = sol[:, :GDN_DV]
            w_ref[rows, out_cols] = sol[:, GDN_DV:].astype(BF16)
            qt_ref[rows, out_cols] = (q * egc).astype(BF16)
            kt_ref[rows, out_cols] = (k * jnp.exp(g_last - gcc)).astype(BF16)
            qk_ref[rows, h * GDN_CHUNK:(h + 1) * GDN_CHUNK] = qk.astype(BF16)
            dec_ref[c, h:h + 1, :] = jnp.broadcast_to(jnp.exp(g_last), (1, LANES))


def _mix_prep(proj, ab, vnorm_w, w_spatial, b_spatial_t, conv_w, alog_pad, dtb_pad, seq):
    n = proj.shape[0]
    tm = TM_PREP
    tiles_per_batch = seq // tm
    const2 = lambda i: (0, 0)
    return pl.pallas_call(
        functools.partial(_mix_prep_kernel, tiles_per_batch=tiles_per_batch),
        grid=(n // tm,),
        in_specs=[
            pl.BlockSpec((tm, PROJ_MAIN), lambda i: (i, 0)),
            pl.BlockSpec((tm, LANES), lambda i: (i, 0)),
            pl.BlockSpec((GM_GROUPS, GM_DIM), const2),
            pl.BlockSpec((GM_GROUPS, GM_CHUNK, GM_CHUNK), lambda i: (0, 0, 0)),
            pl.BlockSpec((GM_CHUNK, GM_GROUPS), const2),
            pl.BlockSpec((GDN_CONV, 3 * GDN_WIDTH), const2),
            pl.BlockSpec((1, LANES), const2),
            pl.BlockSpec((1, LANES), const2),
        ],
        out_specs=[
            pl.BlockSpec((tm, GM_WIDTH), lambda i: (i, 0)),
            pl.BlockSpec((tm, GDN_WIDTH), lambda i: (i, 0)),
            pl.BlockSpec((tm, GDN_WIDTH), lambda i: (i, 0)),
            pl.BlockSpec((tm, GDN_WIDTH), lambda i: (i, 0)),
            pl.BlockSpec((tm, GDN_WIDTH), lambda i: (i, 0)),
            pl.BlockSpec((tm, GDN_HEADS * GDN_CHUNK), lambda i: (i, 0)),
            pl.BlockSpec((tm // GDN_CHUNK, GDN_HEADS, LANES), lambda i: (i, 0, 0)),
        ],
        out_shape=[
            jax.ShapeDtypeStruct((n, GM_WIDTH), BF16),
            jax.ShapeDtypeStruct((n, GDN_WIDTH), BF16),
            jax.ShapeDtypeStruct((n, GDN_WIDTH), BF16),
            jax.ShapeDtypeStruct((n, GDN_WIDTH), BF16),
            jax.ShapeDtypeStruct((n, GDN_WIDTH), F32),
            jax.ShapeDtypeStruct((n, GDN_HEADS * GDN_CHUNK), BF16),
            jax.ShapeDtypeStruct((n // GDN_CHUNK, GDN_HEADS, LANES), F32),
        ],
        scratch_shapes=[pltpu.VMEM((tm + 2 * SUBLANES, 3 * GDN_WIDTH), F32)],
        compiler_params=pltpu.CompilerParams(
            dimension_semantics=("arbitrary",), vmem_limit_bytes=VMEM_LIMIT),
        name="mix_prep",
    )(proj, ab, vnorm_w, w_spatial, b_spatial_t, conv_w, alog_pad, dtb_pad)


def _gdn_scan_kernel(qt_ref, kt_ref, w_ref, u_ref, qk_ref, dec_ref, z_ref, onw_ref,
                     yb_ref, s_ref):
    @pl.when(pl.program_id(1) == 0)
    def _():
        s_ref[...] = jnp.zeros(s_ref.shape, F32)

    for c in range(TM_SCAN // GDN_CHUNK):
        rows = slice(c * GDN_CHUNK, (c + 1) * GDN_CHUNK)
        for h in range(GDN_HEADS):
            cols = slice(h * GDN_DK, (h + 1) * GDN_DK)
            state = s_ref[h]
            state_b = state.astype(BF16)
            v_new = u_ref[rows, cols] - _dot(w_ref[rows, cols], state_b)
            v_new_b = v_new.astype(BF16)
            o = _dot(qt_ref[rows, cols], state_b) + _dot(
                qk_ref[rows, h * GDN_CHUNK:(h + 1) * GDN_CHUNK], v_new_b)
            s_ref[h] = state * dec_ref[c, h:h + 1, :] + _dot_tn(kt_ref[rows, cols], v_new_b)
            zz = z_ref[rows, cols].astype(F32)
            yb_ref[rows, cols] = (_rms(o, onw_ref[...]) * _silu(zz)).astype(BF16)


def _gdn_scan(qt, kt, w, u, qk, dec, proj, onorm_w, bsz, seq):
    n = qt.shape[0]
    tm = TM_SCAN
    tpb = seq // tm
    tok = lambda b, j: (b * tpb + j, 0)
    z_block = (2 * GM_WIDTH + 3 * GDN_WIDTH) // GDN_WIDTH
    return pl.pallas_call(
        _gdn_scan_kernel,
        grid=(bsz, tpb),
        in_specs=[
            pl.BlockSpec((tm, GDN_WIDTH), tok),
            pl.BlockSpec((tm, GDN_WIDTH), tok),
            pl.BlockSpec((tm, GDN_WIDTH), tok),
            pl.BlockSpec((tm, GDN_WIDTH), tok),
            pl.BlockSpec((tm, GDN_HEADS * GDN_CHUNK), tok),
            pl.BlockSpec((tm // GDN_CHUNK, GDN_HEADS, LANES), lambda b, j: (b * tpb + j, 0, 0)),
            pl.BlockSpec((tm, GDN_WIDTH), lambda b, j: (b * tpb + j, z_block)),
            pl.BlockSpec((1, GDN_DV), lambda b, j: (0, 0)),
        ],
        out_specs=pl.BlockSpec((tm, GDN_WIDTH), tok),
        out_shape=jax.ShapeDtypeStruct((n, GDN_WIDTH), BF16),
        scratch_shapes=[pltpu.VMEM((GDN_HEADS, GDN_DK, GDN_DV), F32)],
        compiler_params=pltpu.CompilerParams(dimension_semantics=("arbitrary", "arbitrary")),
        name="gdn_scan",
    )(qt, kt, w, u, qk, dec, proj, onorm_w)


def _out_router_kernel(ya_ref, yb_ref, x_ref, mod_ref, wo_ref, n2w_ref, wr_ref, br_ref,
                       x1_ref, h2_ref, ridx_ref, rw_ref, cnt_ref, carry_ref):
    tm = TM_ROUTER

    @pl.when(pl.program_id(0) == 0)
    def _():
        carry_ref[...] = jnp.zeros(carry_ref.shape, F32)

    mix = _dot(ya_ref[...], wo_ref[0:GM_WIDTH, :]) + _dot(yb_ref[...], wo_ref[GM_WIDTH:, :])
    x1 = x_ref[...] + mod_ref[0, 2:3, :] * mix
    x1_ref[...] = x1
    h2 = _rms(x1, n2w_ref[...]) * (1.0 + mod_ref[0, 4:5, :]) + mod_ref[0, 3:4, :]
    h2_ref[...] = h2

    lane = lax.broadcasted_iota(jnp.int32, (tm, LANES), 1)
    logits = jnp.dot(h2, wr_ref[...], precision=HIGHEST, preferred_element_type=F32) + br_ref[...]
    work = jnp.where(lane < N_EXPERTS, logits, -jnp.inf)
    sel_e, sel_v = [], []
    for _ in range(TOP_K):
        m = jnp.max(work, axis=-1, keepdims=True)
        e = jnp.min(jnp.where(work == m, lane, LANES), axis=-1, keepdims=True)
        sel_e.append(e)
        sel_v.append(m)
        work = jnp.where(lane == e, -jnp.inf, work)
    ex = [jnp.exp(v - sel_v[0]) for v in sel_v]
    den = ex[0] + ex[1] + ex[2] + ex[3]

    onehot = jnp.zeros((tm, LANES), F32)
    for e in sel_e:
        onehot = onehot + jnp.where(lane == e, 1.0, 0.0)
    trow = lax.broadcasted_iota(jnp.int32, (tm, tm), 0)
    tcol = lax.broadcasted_iota(jnp.int32, (tm, tm), 1)
    strict = jnp.where(trow > tcol, 1.0, 0.0).astype(BF16)
    before = _dot(strict, onehot.astype(BF16)) + carry_ref[0:1, :]
    carry = carry_ref[0:1, :] + jnp.sum(onehot, axis=0, keepdims=True)
    carry_ref[...] = jnp.broadcast_to(carry, carry_ref.shape)
    cnt_ref[...] = jnp.broadcast_to(carry, cnt_ref.shape)

    ridx = jnp.zeros((tm, LANES), jnp.int32)
    rw = jnp.zeros((tm, LANES), F32)
    for k in range(TOP_K):
        rank = jnp.sum(jnp.where(lane == sel_e[k], before, 0.0), axis=-1, keepdims=True)
        ridx = jnp.where(lane == k, sel_e[k], ridx)
        ridx = jnp.where(lane == TOP_K + k, rank.astype(jnp.int32), ridx)
        rw = jnp.where(lane == k, ex[k] / den, rw)
    ridx_ref[...] = ridx
    rw_ref[...] = rw


def _out_router(ya, yb, x2d, mod3, w_out, norm2_w, wr_pad, br_pad, seq):
    n = x2d.shape[0]
    tm = TM_ROUTER
    tpb = seq // tm
    tok = lambda i: (i, 0)
    const2 = lambda i: (0, 0)
    return pl.pallas_call(
        _out_router_kernel,
        grid=(n // tm,),
        in_specs=[
            pl.BlockSpec((tm, GM_WIDTH), tok),
            pl.BlockSpec((tm, GDN_WIDTH), tok),
            pl.BlockSpec((tm, D_MODEL), tok),
            pl.BlockSpec((1, N_MOD, D_MODEL), lambda i: (i // tpb, 0, 0)),
            pl.BlockSpec((GM_WIDTH + GDN_WIDTH, D_MODEL), const2),
            pl.BlockSpec((1, D_MODEL), const2),
            pl.BlockSpec((D_MODEL, LANES), const2),
            pl.BlockSpec((1, LANES), const2),
        ],
        out_specs=[
            pl.BlockSpec((tm, D_MODEL), tok),
            pl.BlockSpec((tm, D_MODEL), tok),
            pl.BlockSpec((tm, LANES), tok),
            pl.BlockSpec((tm, LANES), tok),
            pl.BlockSpec((SUBLANES, LANES), const2),
        ],
        out_shape=[
            jax.ShapeDtypeStruct((n, D_MODEL), F32),
            jax.ShapeDtypeStruct((n, D_MODEL), F32),
            jax.ShapeDtypeStruct((n, LANES), jnp.int32),
            jax.ShapeDtypeStruct((n, LANES), F32),
            jax.ShapeDtypeStruct((SUBLANES, LANES), F32),
        ],
        scratch_shapes=[pltpu.VMEM((SUBLANES, LANES), F32)],
        compiler_params=pltpu.CompilerParams(
            dimension_semantics=("arbitrary",), vmem_limit_bytes=VMEM_LIMIT),
        name="out_router",
    )(ya, yb, x2d, mod3, w_out, norm2_w, wr_pad, br_pad)


def _row_copy(src_hbm, src_row, dst_row_ref, sem):
    return pltpu.make_async_copy(src_hbm.at[pl.ds(src_row, 1)], dst_row_ref, sem)


def _fetch_indices(idx_hbm, start, idx_smem, sem):
    cp = pltpu.make_async_copy(idx_hbm.at[pl.ds(start, idx_smem.shape[0])], idx_smem, sem)
    cp.start()
    cp.wait()


def _moe_kernel(be_ref, nvb_ref, slot_tok_hbm, h2_hbm, wgu_ref, bgu_ref, wd_ref, bd_ref,
                ys_ref, idx_smem, xbuf, row_sem, idx_sem):
    t = MOE_BLOCK
    b = pl.program_id(0)
    nvb = nvb_ref[0]
    slot = b % 2

    def issue(block, s):
        _fetch_indices(slot_tok_hbm, block * t, idx_smem.at[s], idx_sem)

        def body(r, carry):
            _row_copy(h2_hbm, idx_smem[s, r], xbuf.at[s, pl.ds(r, 1)], row_sem.at[s]).start()
            return carry
        lax.fori_loop(0, t, body, 0)

    @pl.when(b == 0)
    def _():
        issue(0, 0)

    @pl.when(b + 1 < nvb)
    def _():
        issue(b + 1, 1 - slot)

    @pl.when(b < nvb)
    def _():
        def wait_body(r, carry):
            _row_copy(h2_hbm, 0, xbuf.at[slot, pl.ds(r, 1)], row_sem.at[slot]).wait()
            return carry
        lax.fori_loop(0, t, wait_body, 0)

        xb = xbuf[slot].astype(BF16)
        acc = jnp.zeros((t, D_MODEL), F32)
        for f in range(D_FF // FF_TILE):
            gcols = slice(f * FF_TILE, (f + 1) * FF_TILE)
            ucols = slice(D_FF + f * FF_TILE, D_FF + (f + 1) * FF_TILE)
            gate = _dot(xb, wgu_ref[0, :, gcols]) + bgu_ref[0, :, gcols]
            up = _dot(xb, wgu_ref[0, :, ucols]) + bgu_ref[0, :, ucols]
            gate = jnp.minimum(gate, SWIGLU_LIMIT)
            up = jnp.clip(up, -SWIGLU_LIMIT, SWIGLU_LIMIT)
            act = gate * jax.nn.sigmoid(SWIGLU_ALPHA * gate) * (up + 1.0)
            acc = acc + _dot(act.astype(BF16), wd_ref[0, gcols, :])
        ys_ref[...] = acc + bd_ref[0]

    @pl.when(b >= nvb)
    def _():
        ys_ref[...] = jnp.zeros(ys_ref.shape, F32)


def _moe_ffn(block_e, nvb, slot_tok, h2, w_gu, b_gu, w_down, b_down, n_blocks):
    t = MOE_BLOCK
    n_slots = n_blocks * t

    def w_map(b, be, nv):
        return (be[b], 0, 0)

    def y_map(b, be, nv):
        return (b, 0)

    grid_spec = pltpu.PrefetchScalarGridSpec(
        num_scalar_prefetch=2,
        grid=(n_blocks,),
        in_specs=[
            pl.BlockSpec(memory_space=pl.ANY),
            pl.BlockSpec(memory_space=pl.ANY),
            pl.BlockSpec((1, D_MODEL, 2 * D_FF), w_map),
            pl.BlockSpec((1, 1, 2 * D_FF), w_map),
            pl.BlockSpec((1, D_FF, D_MODEL), w_map),
            pl.BlockSpec((1, 1, D_MODEL), w_map),
        ],
        out_specs=pl.BlockSpec((t, D_MODEL), y_map),
        scratch_shapes=[
            pltpu.SMEM((2, t), jnp.int32),
            pltpu.VMEM((2, t, D_MODEL), F32),
            pltpu.SemaphoreType.DMA((2,)),
            pltpu.SemaphoreType.DMA(()),
        ],
    )
    return pl.pallas_call(
        _moe_kernel,
        grid_spec=grid_spec,
        out_shape=jax.ShapeDtypeStruct((n_slots, D_MODEL), F32),
        compiler_params=pltpu.CompilerParams(
            dimension_semantics=("arbitrary",), vmem_limit_bytes=VMEM_LIMIT),
        name="moe_ffn",
    )(block_e, nvb, slot_tok, h2, w_gu, b_gu, w_down, b_down)


def _combine_kernel(dest_hbm, ys_hbm, x1_ref, rw_ref, mod_ref, nfw_ref, out_ref,
                    idx_smem, ybuf, row_sem, idx_sem):
    tm = TM_COMBINE
    i = pl.program_id(0)
    n_tiles = pl.num_programs(0)
    slot = i % 2

    def issue(tile, s):
        _fetch_indices(dest_hbm, tile * tm * TOP_K, idx_smem.at[s], idx_sem)

        def body(r, carry):
            for k in range(TOP_K):
                _row_copy(ys_hbm, idx_smem[s, r * TOP_K + k], ybuf.at[s, k, pl.ds(r, 1)],
                          row_sem.at[s]).start()
            return carry
        lax.fori_loop(0, tm, body, 0)

    @pl.when(i == 0)
    def _():
        issue(0, 0)

    @pl.when(i + 1 < n_tiles)
    def _():
        issue(i + 1, 1 - slot)

    def wait_body(r, carry):
        for k in range(TOP_K):
            _row_copy(ys_hbm, 0, ybuf.at[slot, k, pl.ds(r, 1)], row_sem.at[slot]).wait()
        return carry
    lax.fori_loop(0, tm, wait_body, 0)

    rw = rw_ref[...]
    moe = jnp.zeros((tm, D_MODEL), F32)
    for k in range(TOP_K):
        moe = moe + rw[:, k:k + 1] * ybuf[slot, k]
    x2 = x1_ref[...] + mod_ref[0, 5:6, :] * moe
    out_ref[...] = _rms(x2, nfw_ref[...])


def _combine(dest_flat, ys, x1, rw, mod3, norm_f_w, seq):
    n = x1.shape[0]
    tm = TM_COMBINE
    tpb = seq // tm
    tok = lambda i: (i, 0)
    return pl.pallas_call(
        _combine_kernel,
        grid=(n // tm,),
        in_specs=[
            pl.BlockSpec(memory_space=pl.ANY),
            pl.BlockSpec(memory_space=pl.ANY),
            pl.BlockSpec((tm, D_MODEL), tok),
            pl.BlockSpec((tm, LANES), tok),
            pl.BlockSpec((1, N_MOD, D_MODEL), lambda i: (i // tpb, 0, 0)),
            pl.BlockSpec((1, D_MODEL), lambda i: (0, 0)),
        ],
        out_specs=pl.BlockSpec((tm, D_MODEL), tok),
        out_shape=jax.ShapeDtypeStruct((n, D_MODEL), F32),
        scratch_shapes=[
            pltpu.SMEM((2, tm * TOP_K), jnp.int32),
            pltpu.VMEM((2, TOP_K, tm, D_MODEL), F32),
            pltpu.SemaphoreType.DMA((2,)),
            pltpu.SemaphoreType.DMA(()),
        ],
        compiler_params=pltpu.CompilerParams(
            dimension_semantics=("arbitrary",), vmem_limit_bytes=VMEM_LIMIT),
        name="combine",
    )(dest_flat, ys, x1, rw, mod3, norm_f_w)


def _pad_lanes(v, fill=0.0):
    out = jnp.full((1, LANES), fill, F32)
    return out.at[0, :v.shape[0]].set(v.astype(F32))


def kernel(x, c, w_ada, b_ada, norm1_w, w_in, gm_vnorm_w, gm_w_spatial, gm_b_spatial, gdn_conv_w,
           gdn_a_log, gdn_dt_bias, gdn_onorm_w, w_out, norm2_w, w_router, b_router, w_gu, b_gu,
           w_down, b_down, norm_f_w):
    bsz, seq, d = x.shape
    n = bsz * seq
    depth = w_ada.shape[0]
    assert depth == 1, "the closing RMSNorm is fused into the single layer's combine call"
    xcur = x.reshape(n, d)
    c_pad = jnp.zeros((SUBLANES, d), F32).at[:bsz].set(c)

    for l in range(depth):
        mod = _adaln(c_pad, w_ada[l], b_ada[l][None, :])[:bsz]
        mod3 = mod.reshape(bsz, N_MOD, d)

        w_main = w_in[l][:, :PROJ_MAIN].astype(BF16)
        w_ab = jnp.zeros((d, LANES), BF16).at[:, :2 * GDN_HEADS].set(
            w_in[l][:, PROJ_MAIN:].astype(BF16))
        proj, ab = _inproj(xcur, mod3, norm1_w[l][None, :], w_main, w_ab, seq)

        ya, qt, kt, wmat, umat, qk, dec = _mix_prep(
            proj, ab, gm_vnorm_w[l], gm_w_spatial[l], gm_b_spatial[l].T, gdn_conv_w[l],
            _pad_lanes(gdn_a_log[l]), _pad_lanes(gdn_dt_bias[l]), seq)
        yb = _gdn_scan(qt, kt, wmat, umat, qk, dec, proj, gdn_onorm_w[l][None, :], bsz, seq)

        wr_pad = jnp.zeros((d, LANES), F32).at[:, :N_EXPERTS].set(w_router[l])
        x1, h2, ridx, rw, cnt = _out_router(
            ya, yb, xcur, mod3, w_out[l].astype(BF16), norm2_w[l][None, :], wr_pad,
            _pad_lanes(b_router[l]), seq)

        t = MOE_BLOCK
        n_assign = n * TOP_K
        n_blocks = n_assign // t + N_EXPERTS
        n_slots = n_blocks * t
        counts = cnt[0, :N_EXPERTS].astype(jnp.int32)
        padded = (counts + t - 1) // t * t
        pad_end = jnp.cumsum(padded)
        pad_start = pad_end - padded
        dest = pad_start[ridx[:, :TOP_K]] + ridx[:, TOP_K:2 * TOP_K]
        tok_ids = jnp.repeat(jnp.arange(n, dtype=jnp.int32), TOP_K)
        slot_tok = (jnp.arange(n_slots, dtype=jnp.int32) % n).at[dest.reshape(-1)].set(tok_ids)
        nvb = (pad_end[-1] // t).astype(jnp.int32)
        blk = jnp.minimum(jnp.arange(n_blocks, dtype=jnp.int32), nvb - 1) * t
        block_e = jnp.minimum(jnp.searchsorted(pad_end, blk, side="right"),
                              N_EXPERTS - 1).astype(jnp.int32)

        ys = _moe_ffn(block_e, nvb[None], slot_tok, h2, w_gu[l].astype(BF16), b_gu[l][:, None, :],
                      w_down[l].astype(BF16), b_down[l][:, None, :], n_blocks)
        xcur = _combine(dest.reshape(-1).astype(jnp.int32), ys, x1, rw, mod3, norm_f_w[None, :],
                        seq)
    return xcur.reshape(bsz, seq, d)
```

```python
import functools

import jax
import jax.numpy as jnp
from jax import lax
from jax.experimental import pallas as pl
from jax.experimental.pallas import tpu as pltpu

F32 = jnp.float32
BF16 = jnp.bfloat16
HIGHEST = lax.Precision.HIGHEST

D_MODEL = 1024
GM_GROUPS = 4
GM_DIM = 128
GM_WIDTH = GM_GROUPS * GM_DIM
GM_CHUNK = 128
GDN_HEADS = 4
GDN_DK = 128
GDN_DV = 128
GDN_WIDTH = GDN_HEADS * GDN_DK
GDN_CONV = 4
GDN_CHUNK = 64
GDN_PAIR = 2 * GDN_CHUNK
N_EXPERTS = 32
TOP_K = 4
D_FF = D_MODEL
SWIGLU_LIMIT = 7.0
SWIGLU_ALPHA = 1.702
N_MOD = 6
EPS = 1e-6

LANES = 128
SUBLANES = 8
PROJ_MAIN = 2 * GM_WIDTH + 4 * GDN_WIDTH

TM_INPROJ = 512
TM_PREP = 256
TM_SCAN = 256
TM_ROUTER = 256
MOE_BLOCK = 256
TM_COMBINE = 128
FF_TILE = 512
IDX_RING = 4
VMEM_LIMIT = 48 * 1024 * 1024


def _dot(a, b):
    return jnp.dot(a, b, preferred_element_type=F32)


def _dot_nt(a, b):
    return lax.dot_general(a, b, (((1,), (1,)), ((), ())), preferred_element_type=F32)


def _dot_tn(a, b):
    return lax.dot_general(a, b, (((0,), (0,)), ((), ())), preferred_element_type=F32)


def _split3(x):
    hi = x.astype(BF16)
    r1 = x - hi.astype(F32)
    mid = r1.astype(BF16)
    lo = (r1 - mid.astype(F32)).astype(BF16)
    return hi, mid, lo


def _rms(x, w):
    return x * lax.rsqrt(jnp.mean(x * x, axis=-1, keepdims=True) + EPS) * w


def _gelu(x):
    return 0.5 * x * (1.0 + lax.erf(x * (2.0 ** -0.5)))


def _silu(x):
    return x * jax.nn.sigmoid(x)


def _adaln_kernel(c_ref, w_ref, b_ref, o_ref):
    c = c_ref[...]
    o_ref[...] = jnp.dot(_silu(c), w_ref[...], precision=HIGHEST,
                         preferred_element_type=F32) + b_ref[...]


def _adaln(c_pad, w_ada, b_ada):
    rows = c_pad.shape[0]
    n_out = w_ada.shape[1]
    return pl.pallas_call(
        _adaln_kernel,
        grid=(n_out // D_MODEL,),
        in_specs=[
            pl.BlockSpec((rows, D_MODEL), lambda j: (0, 0)),
            pl.BlockSpec((D_MODEL, D_MODEL), lambda j: (0, j)),
            pl.BlockSpec((1, D_MODEL), lambda j: (0, j)),
        ],
        out_specs=pl.BlockSpec((rows, D_MODEL), lambda j: (0, j)),
        out_shape=jax.ShapeDtypeStruct((rows, n_out), F32),
        name="adaln",
    )(c_pad, w_ada, b_ada)


def _inproj_kernel(x_ref, mod_ref, nw_ref, w_ref, wab_ref, proj_ref, ab_ref):
    h = _rms(x_ref[...], nw_ref[...]) * (1.0 + mod_ref[0, 1:2, :]) + mod_ref[0, 0:1, :]
    hb = h.astype(BF16)
    for j in range(PROJ_MAIN // 512):
        cols = slice(j * 512, (j + 1) * 512)
        proj_ref[:, cols] = _dot(hb, w_ref[:, cols]).astype(BF16)
    ab_ref[...] = _dot(hb, wab_ref[...])


def _inproj(x2d, mod3, norm_w, w_main, w_ab, seq):
    n = x2d.shape[0]
    tiles_per_batch = seq // TM_INPROJ
    return pl.pallas_call(
        _inproj_kernel,
        grid=(n // TM_INPROJ,),
        in_specs=[
            pl.BlockSpec((TM_INPROJ, D_MODEL), lambda i: (i, 0)),
            pl.BlockSpec((1, N_MOD, D_MODEL), lambda i: (i // tiles_per_batch, 0, 0)),
            pl.BlockSpec((1, D_MODEL), lambda i: (0, 0)),
            pl.BlockSpec((D_MODEL, PROJ_MAIN), lambda i: (0, 0)),
            pl.BlockSpec((D_MODEL, LANES), lambda i: (0, 0)),
        ],
        out_specs=[
            pl.BlockSpec((TM_INPROJ, PROJ_MAIN), lambda i: (i, 0)),
            pl.BlockSpec((TM_INPROJ, LANES), lambda i: (i, 0)),
        ],
        out_shape=[
            jax.ShapeDtypeStruct((n, PROJ_MAIN), BF16),
            jax.ShapeDtypeStruct((n, LANES), F32),
        ],
        compiler_params=pltpu.CompilerParams(
            dimension_semantics=("parallel",), vmem_limit_bytes=VMEM_LIMIT),
        name="in_proj",
    )(x2d, mod3, norm_w, w_main, w_ab)


def _unit_lower_inverses(a_list):
    c = a_list[0].shape[0]
    row = lax.broadcasted_iota(jnp.int32, (c, c), 0)
    col = lax.broadcasted_iota(jnp.int32, (c, c), 1)
    eye = jnp.where(row == col, 1.0, 0.0).astype(F32)
    ps = [eye - a for a in a_list]
    qs = [a.astype(BF16) for a in a_list]
    qs = [_dot(q, q) for q in qs]
    power = 2
    while 2 * power < GDN_CHUNK:
        qbs = [q.astype(BF16) for q in qs]
        ps = [p + _dot(p.astype(BF16), qb) for p, qb in zip(ps, qbs)]
        qs = [_dot(qb, qb) for qb in qbs]
        power *= 2
    return [p + _dot(p.astype(BF16), q.astype(BF16)) for p, q in zip(ps, qs)]


def _mix_prep_kernel(proj_ref, ab_ref, vnw_ref, wsp_ref, bsp_ref, cw_ref, alog_ref, dtb_ref,
                     ya_ref, qt_ref, kt_ref, w_ref, u_ref, qk_ref, dec_ref, ext_ref,
                     *, tiles_per_batch):
    tm = TM_PREP
    i = pl.program_id(0)

    row = lax.broadcasted_iota(jnp.int32, (GM_CHUNK, GM_CHUNK), 0)
    col = lax.broadcasted_iota(jnp.int32, (GM_CHUNK, GM_CHUNK), 1)
    causal = row >= col
    for g in range(GM_GROUPS):
        ws = jnp.where(causal, wsp_ref[g], 0.0).astype(BF16)
        bcol = bsp_ref[:, g:g + 1]
        cols_u = slice(g * GM_DIM, (g + 1) * GM_DIM)
        cols_v = slice(GM_WIDTH + g * GM_DIM, GM_WIDTH + (g + 1) * GM_DIM)
        for c in range(tm // GM_CHUNK):
            rows = slice(c * GM_CHUNK, (c + 1) * GM_CHUNK)
            u = _gelu(proj_ref[rows, cols_u].astype(F32))
            v = _rms(_gelu(proj_ref[rows, cols_v].astype(F32)), vnw_ref[g:g + 1, :])
            z = _dot(ws, v.astype(BF16)) + bcol
            ya_ref[rows, cols_u] = (u * z).astype(BF16)

    @pl.when(i % tiles_per_batch == 0)
    def _():
        ext_ref[0:SUBLANES, :] = jnp.zeros((SUBLANES, 3 * GDN_WIDTH), F32)

    qkv_cols = slice(2 * GM_WIDTH, 2 * GM_WIDTH + 3 * GDN_WIDTH)
    ext_ref[SUBLANES:SUBLANES + tm, :] = proj_ref[:, qkv_cols].astype(F32)
    conv = jnp.zeros((tm, 3 * GDN_WIDTH), F32)
    for j in range(GDN_CONV):
        start = SUBLANES - (GDN_CONV - 1) + j
        conv = conv + cw_ref[j:j + 1, :] * ext_ref[start:start + tm, :]
    ext_ref[0:SUBLANES, :] = ext_ref[tm:tm + SUBLANES, :]
    act = _silu(conv)

    ab = ab_ref[...]
    sp_in = ab + dtb_ref[...]
    g_all = -jnp.exp(alog_ref[...]) * (
        jnp.maximum(sp_in, 0.0) + jnp.log1p(jnp.exp(-jnp.abs(sp_in))))
    beta_all = jax.nn.sigmoid(ab)
    trow = lax.broadcasted_iota(jnp.int32, (tm, tm), 0)
    tcol = lax.broadcasted_iota(jnp.int32, (tm, tm), 1)
    blk_lower = jnp.where((trow >= tcol) & (trow // GDN_CHUNK == tcol // GDN_CHUNK),
                          1.0, 0.0).astype(BF16)
    g_hi, g_mid, g_lo = _split3(g_all)
    gc_all = _dot(blk_lower, g_hi) + _dot(blk_lower, g_mid) + _dot(blk_lower, g_lo)
    gc_all_t = gc_all.T

    prow = lax.broadcasted_iota(jnp.int32, (GDN_PAIR, GDN_PAIR), 0)
    pcol = lax.broadcasted_iota(jnp.int32, (GDN_PAIR, GDN_PAIR), 1)
    same_chunk = prow // GDN_CHUNK == pcol // GDN_CHUNK
    tri = (prow >= pcol) & same_chunk
    strict = (prow > pcol) & same_chunk
    first_half = lax.broadcasted_iota(jnp.int32, (GDN_PAIR, 1), 0) < GDN_CHUNK

    blocks = [(h, p) for h in range(GDN_HEADS) for p in range(tm // GDN_PAIR)]
    q_l, k_l, kb_l, kbf_l, beta_l, gcc_l, decay_l, v_l = [], [], [], [], [], [], [], []
    for h in range(GDN_HEADS):
        hq = slice(h * GDN_DK, (h + 1) * GDN_DK)
        hk = slice(GDN_WIDTH + h * GDN_DK, GDN_WIDTH + (h + 1) * GDN_DK)
        hv = slice(2 * GDN_WIDTH + h * GDN_DV, 2 * GDN_WIDTH + (h + 1) * GDN_DV)
        q_h = act[:, hq]
        k_h = act[:, hk]
        q_h = q_h * lax.rsqrt(jnp.sum(q_h * q_h, axis=-1, keepdims=True) + EPS) * (GDN_DK ** -0.5)
        k_h = k_h * lax.rsqrt(jnp.sum(k_h * k_h, axis=-1, keepdims=True) + EPS)
        v_h = act[:, hv]
        for p in range(tm // GDN_PAIR):
            rows = slice(p * GDN_PAIR, (p + 1) * GDN_PAIR)
            beta = beta_all[rows, GDN_HEADS + h:GDN_HEADS + h + 1]
            gcc = gc_all[rows, h:h + 1]
            gcr = gc_all_t[h:h + 1, p * GDN_PAIR:(p + 1) * GDN_PAIR]
            k = k_h[rows]
            q_l.append(q_h[rows])
            k_l.append(k)
            kb_l.append(k * beta)
            kbf_l.append(k.astype(BF16))
            beta_l.append(beta)
            gcc_l.append(gcc)
            v_l.append(v_h[rows])
            decay_l.append(jnp.where(tri, jnp.exp(jnp.where(tri, gcc - gcr, 0.0)), 0.0))

    kk_l = [_dot_nt(kb.astype(BF16), kbf) for kb, kbf in zip(kb_l, kbf_l)]
    a_l = [jnp.where(strict, kk * decay, 0.0) for kk, decay in zip(kk_l, decay_l)]
    t_l = _unit_lower_inverses(a_l)
    egc_l = [jnp.exp(gcc) for gcc in gcc_l]
    rhs_l = [jnp.concatenate([v * beta, kb * egc], axis=1).astype(BF16)
             for v, beta, kb, egc in zip(v_l, beta_l, kb_l, egc_l)]
    sol_l = [_dot(t.astype(BF16), rhs) for t, rhs in zip(t_l, rhs_l)]
    qk_l = [jnp.where(tri, _dot_nt(q.astype(BF16), kbf) * decay, 0.0)
            for q, kbf, decay in zip(q_l, kbf_l, decay_l)]

    for idx, (h, p) in enumerate(blocks):
        rows = slice(p * GDN_PAIR, (p + 1) * GDN_PAIR)
        cols = slice(h * GDN_DK, (h + 1) * GDN_DK)
        gcc = gcc_l[idx]
        gl0 = gcc[GDN_CHUNK - 1:GDN_CHUNK]
        gl1 = gcc[GDN_PAIR - 1:GDN_PAIR]
        g_last = jnp.where(first_half, gl0, gl1)
        u_ref[rows, cols] = sol_l[idx][:, :GDN_DV]
        w_ref[rows, cols] = sol_l[idx][:, GDN_DV:].astype(BF16)
        qt_ref[rows, cols] = (q_l[idx] * egc_l[idx]).astype(BF16)
        kt_ref[rows, cols] = (k_l[idx] * jnp.exp(g_last - gcc)).astype(BF16)
        qk_ref[rows, cols] = qk_l[idx].astype(BF16)
        dec_ref[2 * p, h:h + 1, :] = jnp.broadcast_to(jnp.exp(gl0), (1, LANES))
        dec_ref[2 * p + 1, h:h + 1, :] = jnp.broadcast_to(jnp.exp(gl1), (1, LANES))


def _mix_prep(proj, ab, vnorm_w, w_spatial, b_spatial_t, conv_w, alog_pad, dtb_pad, seq):
    n = proj.shape[0]
    tm = TM_PREP
    tiles_per_batch = seq // tm
    const2 = lambda i: (0, 0)
    return pl.pallas_call(
        functools.partial(_mix_prep_kernel, tiles_per_batch=tiles_per_batch),
        grid=(n // tm,),
        in_specs=[
            pl.BlockSpec((tm, PROJ_MAIN), lambda i: (i, 0)),
            pl.BlockSpec((tm, LANES), lambda i: (i, 0)),
            pl.BlockSpec((GM_GROUPS, GM_DIM), const2),
            pl.BlockSpec((GM_GROUPS, GM_CHUNK, GM_CHUNK), lambda i: (0, 0, 0)),
            pl.BlockSpec((GM_CHUNK, GM_GROUPS), const2),
            pl.BlockSpec((GDN_CONV, 3 * GDN_WIDTH), const2),
            pl.BlockSpec((1, LANES), const2),
            pl.BlockSpec((1, LANES), const2),
        ],
        out_specs=[
            pl.BlockSpec((tm, GM_WIDTH), lambda i: (i, 0)),
            pl.BlockSpec((tm, GDN_WIDTH), lambda i: (i, 0)),
            pl.BlockSpec((tm, GDN_WIDTH), lambda i: (i, 0)),
            pl.BlockSpec((tm, GDN_WIDTH), lambda i: (i, 0)),
            pl.BlockSpec((tm, GDN_WIDTH), lambda i: (i, 0)),
            pl.BlockSpec((tm, GDN_WIDTH), lambda i: (i, 0)),
            pl.BlockSpec((tm // GDN_CHUNK, GDN_HEADS, LANES), lambda i: (i, 0, 0)),
        ],
        out_shape=[
            jax.ShapeDtypeStruct((n, GM_WIDTH), BF16),
            jax.ShapeDtypeStruct((n, GDN_WIDTH), BF16),
            jax.ShapeDtypeStruct((n, GDN_WIDTH), BF16),
            jax.ShapeDtypeStruct((n, GDN_WIDTH), BF16),
            jax.ShapeDtypeStruct((n, GDN_WIDTH), F32),
            jax.ShapeDtypeStruct((n, GDN_WIDTH), BF16),
            jax.ShapeDtypeStruct((n // GDN_CHUNK, GDN_HEADS, LANES), F32),
        ],
        scratch_shapes=[pltpu.VMEM((tm + 2 * SUBLANES, 3 * GDN_WIDTH), F32)],
        compiler_params=pltpu.CompilerParams(
            dimension_semantics=("arbitrary",), vmem_limit_bytes=VMEM_LIMIT),
        name="mix_prep",
    )(proj, ab, vnorm_w, w_spatial, b_spatial_t, conv_w, alog_pad, dtb_pad)


def _gdn_scan_kernel(qt_ref, kt_ref, w_ref, u_ref, qk_ref, dec_ref, z_ref, onw_ref,
                     yb_ref, s_ref):
    @pl.when(pl.program_id(1) == 0)
    def _():
        s_ref[...] = jnp.zeros(s_ref.shape, F32)

    heads = range(GDN_HEADS)
    hcols = [slice(h * GDN_DK, (h + 1) * GDN_DK) for h in heads]
    states = [s_ref[h] for h in heads]
    v_prev = [None] * GDN_HEADS
    for c in range(TM_SCAN // GDN_CHUNK):
        rows = slice(c * GDN_CHUNK, (c + 1) * GDN_CHUNK)
        states_b = [s.astype(BF16) for s in states]
        ws = [_dot(w_ref[rows, hcols[h]], states_b[h]) for h in heads]
        qs = [_dot(qt_ref[rows, hcols[h]], states_b[h]) for h in heads]
        v_new = [(u_ref[rows, hcols[h]] - ws[h]).astype(BF16) for h in heads]
        if c % 2 == 0:
            o = [qs[h] + _dot(qk_ref[rows, h * GDN_DK:h * GDN_DK + GDN_CHUNK], v_new[h])
                 for h in heads]
        else:
            o = [qs[h] + _dot(qk_ref[rows, hcols[h]],
                              jnp.concatenate([v_prev[h], v_new[h]], axis=0)) for h in heads]
        states = [states[h] * dec_ref[c, h:h + 1, :] + _dot_tn(kt_ref[rows, hcols[h]], v_new[h])
                  for h in heads]
        v_prev = v_new
        for h in heads:
            zz = z_ref[rows, hcols[h]].astype(F32)
            yb_ref[rows, hcols[h]] = (_rms(o[h], onw_ref[...]) * _silu(zz)).astype(BF16)
    for h in heads:
        s_ref[h] = states[h]


def _gdn_scan(qt, kt, w, u, qk, dec, proj, onorm_w, bsz, seq):
    n = qt.shape[0]
    tm = TM_SCAN
    tpb = seq // tm
    tok = lambda b, j: (b * tpb + j, 0)
    z_block = (2 * GM_WIDTH + 3 * GDN_WIDTH) // GDN_WIDTH
    return pl.pallas_call(
        _gdn_scan_kernel,
        grid=(bsz, tpb),
        in_specs=[
            pl.BlockSpec((tm, GDN_WIDTH), tok),
            pl.BlockSpec((tm, GDN_WIDTH), tok),
            pl.BlockSpec((tm, GDN_WIDTH), tok),
            pl.BlockSpec((tm, GDN_WIDTH), tok),
            pl.BlockSpec((tm, GDN_WIDTH), tok),
            pl.BlockSpec((tm // GDN_CHUNK, GDN_HEADS, LANES), lambda b, j: (b * tpb + j, 0, 0)),
            pl.BlockSpec((tm, GDN_WIDTH), lambda b, j: (b * tpb + j, z_block)),
            pl.BlockSpec((1, GDN_DV), lambda b, j: (0, 0)),
        ],
        out_specs=pl.BlockSpec((tm, GDN_WIDTH), tok),
        out_shape=jax.ShapeDtypeStruct((n, GDN_WIDTH), BF16),
        scratch_shapes=[pltpu.VMEM((GDN_HEADS, GDN_DK, GDN_DV), F32)],
        compiler_params=pltpu.CompilerParams(dimension_semantics=("arbitrary", "arbitrary")),
        name="gdn_scan",
    )(qt, kt, w, u, qk, dec, proj, onorm_w)


def _out_router_kernel(ya_ref, yb_ref, x_ref, mod_ref, wo_ref, n2w_ref, wrh_ref, wrl_ref, br_ref,
                       x1_ref, h2_ref, ridx_ref, rw_ref, cnt_ref, carry_ref):
    tm = TM_ROUTER

    @pl.when(pl.program_id(0) == 0)
    def _():
        carry_ref[...] = jnp.zeros(carry_ref.shape, F32)

    mix = _dot(ya_ref[...], wo_ref[0:GM_WIDTH, :]) + _dot(yb_ref[...], wo_ref[GM_WIDTH:, :])
    x1 = x_ref[...] + mod_ref[0, 2:3, :] * mix
    x1_ref[...] = x1
    h2 = _rms(x1, n2w_ref[...]) * (1.0 + mod_ref[0, 4:5, :]) + mod_ref[0, 3:4, :]
    h2_ref[...] = h2

    h_hi = h2.astype(BF16)
    h_lo = (h2 - h_hi.astype(F32)).astype(BF16)
    logits = (_dot(h_hi, wrh_ref[...]) + _dot(h_hi, wrl_ref[...]) + _dot(h_lo, wrh_ref[...])
              + br_ref[...])
    lane = lax.broadcasted_iota(jnp.int32, (tm, LANES), 1)
    work = jnp.where(lane < N_EXPERTS, logits, -jnp.inf)
    sel_e, sel_v = [], []
    for _ in range(TOP_K):
        m = jnp.max(work, axis=-1, keepdims=True)
        e = jnp.min(jnp.where(work == m, lane, LANES), axis=-1, keepdims=True)
        sel_e.append(e)
        sel_v.append(m)
        work = jnp.where(lane == e, -jnp.inf, work)
    ex = [jnp.exp(v - sel_v[0]) for v in sel_v]
    den = ex[0] + ex[1] + ex[2] + ex[3]

    ridx = jnp.zeros((tm, LANES), jnp.int32)
    rw = jnp.zeros((tm, LANES), F32)
    onehot = jnp.zeros((tm, LANES), F32)
    for k in range(TOP_K):
        onehot = onehot + jnp.where(lane == sel_e[k], 1.0, 0.0)
        ridx = jnp.where(lane == k, sel_e[k], ridx)
        rw = jnp.where(lane == k, ex[k] / den, rw)
    ridx_ref[...] = ridx
    rw_ref[...] = rw
    carry = carry_ref[0:1, :] + jnp.sum(onehot, axis=0, keepdims=True)
    carry_ref[...] = jnp.broadcast_to(carry, carry_ref.shape)
    cnt_ref[...] = jnp.broadcast_to(carry, cnt_ref.shape)


def _out_router(ya, yb, x2d, mod3, w_out, norm2_w, wr_hi, wr_lo, br_pad, seq):
    n = x2d.shape[0]
    tm = TM_ROUTER
    tpb = seq // tm
    tok = lambda i: (i, 0)
    const2 = lambda i: (0, 0)
    return pl.pallas_call(
        _out_router_kernel,
        grid=(n // tm,),
        in_specs=[
            pl.BlockSpec((tm, GM_WIDTH), tok),
            pl.BlockSpec((tm, GDN_WIDTH), tok),
            pl.BlockSpec((tm, D_MODEL), tok),
            pl.BlockSpec((1, N_MOD, D_MODEL), lambda i: (i // tpb, 0, 0)),
            pl.BlockSpec((GM_WIDTH + GDN_WIDTH, D_MODEL), const2),
            pl.BlockSpec((1, D_MODEL), const2),
            pl.BlockSpec((D_MODEL, LANES), const2),
            pl.BlockSpec((D_MODEL, LANES), const2),
            pl.BlockSpec((1, LANES), const2),
        ],
        out_specs=[
            pl.BlockSpec((tm, D_MODEL), tok),
            pl.BlockSpec((tm, D_MODEL), tok),
            pl.BlockSpec((tm, LANES), tok),
            pl.BlockSpec((tm, LANES), tok),
            pl.BlockSpec((SUBLANES, LANES), const2),
        ],
        out_shape=[
            jax.ShapeDtypeStruct((n, D_MODEL), F32),
            jax.ShapeDtypeStruct((n, D_MODEL), F32),
            jax.ShapeDtypeStruct((n, LANES), jnp.int32),
            jax.ShapeDtypeStruct((n, LANES), F32),
            jax.ShapeDtypeStruct((SUBLANES, LANES), F32),
        ],
        scratch_shapes=[pltpu.VMEM((SUBLANES, LANES), F32)],
        compiler_params=pltpu.CompilerParams(
            dimension_semantics=("arbitrary",), vmem_limit_bytes=VMEM_LIMIT),
        name="out_router",
    )(ya, yb, x2d, mod3, w_out, norm2_w, wr_hi, wr_lo, br_pad)


def _moe_kernel(be_ref, rs_ref, off_ref, nv_ref, asg_hbm, h2_hbm, wgu_ref, bgu_ref, wd_ref, bd_ref,
                y4_hbm, idx_smem, x0, x1, y0, y1, gsem, ssem, isem, *, n_tokens):
    t = MOE_BLOCK
    b = pl.program_id(0)
    n_blocks = pl.num_programs(0)
    dump_row = TOP_K * n_tokens

    def idx_copy(block):
        slot = (block + IDX_RING) % IDX_RING
        first = pl.multiple_of(rs_ref[block + 1], LANES)
        return pltpu.make_async_copy(asg_hbm.at[pl.ds(first, t + LANES)],
                                     idx_smem.at[slot], isem.at[slot])

    def gather_row(block, r, dst, sem):
        slot = (block + IDX_RING) % IDX_RING
        asg = idx_smem[slot, off_ref[block + 1] + r]
        tok = asg & (n_tokens - 1)
        return pltpu.make_async_copy(h2_hbm.at[pl.ds(tok, 1)], dst.at[pl.ds(r, 1)], sem)

    def scatter_row(block, r, src, sem):
        slot = (block + IDX_RING) % IDX_RING
        asg = idx_smem[slot, off_ref[block + 1] + r]
        row = jnp.where(r < nv_ref[block + 1], asg, dump_row + r)
        return pltpu.make_async_copy(src.at[pl.ds(r, 1)], y4_hbm.at[pl.ds(row, 1)], sem)

    def wait_gather(dst, sem):
        pltpu.make_async_copy(h2_hbm.at[pl.ds(0, t)], dst, sem).wait()

    def wait_scatter(src, sem):
        pltpu.make_async_copy(src, y4_hbm.at[pl.ds(0, t)], sem).wait()

    @pl.when(b == 0)
    def _():
        idx_copy(-1).start()
        idx_copy(0).start()
        idx_copy(1).start()
        idx_copy(-1).wait()
        idx_copy(0).wait()
        y1[...] = jnp.zeros(y1.shape, F32)

        def body(r, carry):
            gather_row(0, r, x0, gsem.at[0]).start()
            return carry
        lax.fori_loop(0, t, body, 0)

    def step(p):
        xb, xo = (x0, x1) if p == 0 else (x1, x0)
        yb, yo = (y0, y1) if p == 0 else (y1, y0)
        idx_copy(b + 2).start()
        idx_copy(b + 1).wait()
        wait_gather(xb, gsem.at[p])

        @pl.when(b >= 1)
        def _():
            wait_scatter(yb, ssem.at[p])

        n_parts = 2 * (D_FF // FF_TILE)
        per = t // n_parts

        def issue(part):
            for r in range(part * per, (part + 1) * per):
                gather_row(b + 1, r, xo, gsem.at[1 - p]).start()
                scatter_row(b - 1, r, yo, ssem.at[1 - p]).start()

        xbf = xb[...].astype(BF16)
        acc = None
        for f in range(D_FF // FF_TILE):
            gcols = slice(f * FF_TILE, (f + 1) * FF_TILE)
            ucols = slice(D_FF + f * FF_TILE, D_FF + (f + 1) * FF_TILE)
            issue(2 * f)
            gate = _dot(xbf, wgu_ref[0, :, gcols]) + bgu_ref[0, :, gcols]
            up = _dot(xbf, wgu_ref[0, :, ucols]) + bgu_ref[0, :, ucols]
            gate = jnp.minimum(gate, SWIGLU_LIMIT)
            up = jnp.clip(up, -SWIGLU_LIMIT, SWIGLU_LIMIT)
            act = gate * jax.nn.sigmoid(SWIGLU_ALPHA * gate) * (up + 1.0)
            issue(2 * f + 1)
            part = _dot(act.astype(BF16), wd_ref[0, gcols, :])
            acc = part if acc is None else acc + part
        yb[...] = acc + bd_ref[0]

        @pl.when(b == n_blocks - 1)
        def _():
            wait_scatter(yo, ssem.at[1 - p])

            def body(r, carry):
                scatter_row(b, r, yb, ssem.at[p]).start()
                return carry
            lax.fori_loop(0, t, body, 0)
            wait_scatter(yb, ssem.at[p])
            wait_gather(xo, gsem.at[1 - p])
            idx_copy(b + 2).wait()

    @pl.when(b % 2 == 0)
    def _():
        step(0)

    @pl.when(b % 2 == 1)
    def _():
        step(1)


def _moe_ffn(block_e, row_start, n_valid, asg, h2, w_gu, b_gu, w_down, b_down, n_blocks):
    t = MOE_BLOCK
    n_tokens = h2.shape[0]
    assert n_tokens & (n_tokens - 1) == 0, "token index is masked out of the assignment id"

    def w_map(b, be, rs, off, nv):
        return (be[b], 0, 0)

    grid_spec = pltpu.PrefetchScalarGridSpec(
        num_scalar_prefetch=4,
        grid=(n_blocks,),
        in_specs=[
            pl.BlockSpec(memory_space=pl.ANY),
            pl.BlockSpec(memory_space=pl.ANY),
            pl.BlockSpec((1, D_MODEL, 2 * D_FF), w_map),
            pl.BlockSpec((1, 1, 2 * D_FF), w_map),
            pl.BlockSpec((1, D_FF, D_MODEL), w_map),
            pl.BlockSpec((1, 1, D_MODEL), w_map),
        ],
        out_specs=pl.BlockSpec(memory_space=pl.ANY),
        scratch_shapes=[
            pltpu.SMEM((IDX_RING, t + LANES), jnp.int32),
            pltpu.VMEM((t, D_MODEL), F32),
            pltpu.VMEM((t, D_MODEL), F32),
            pltpu.VMEM((t, D_MODEL), F32),
            pltpu.VMEM((t, D_MODEL), F32),
            pltpu.SemaphoreType.DMA((2,)),
            pltpu.SemaphoreType.DMA((2,)),
            pltpu.SemaphoreType.DMA((IDX_RING,)),
        ],
    )
    return pl.pallas_call(
        functools.partial(_moe_kernel, n_tokens=n_tokens),
        grid_spec=grid_spec,
        out_shape=jax.ShapeDtypeStruct((TOP_K * n_tokens + t, D_MODEL), F32),
        compiler_params=pltpu.CompilerParams(
            dimension_semantics=("arbitrary",), vmem_limit_bytes=VMEM_LIMIT),
        name="moe_ffn",
    )(block_e, row_start // LANES * LANES, row_start % LANES, n_valid, asg, h2, w_gu, b_gu, w_down,
      b_down)


def _combine_kernel(ya_ref, yb_ref, yc_ref, yd_ref, x1_ref, rw_ref, mod_ref, nfw_ref, out_ref):
    rw = rw_ref[...]
    moe = rw[:, 0:1] * ya_ref[...]
    for k, y_ref in enumerate((yb_ref, yc_ref, yd_ref), start=1):
        moe = moe + rw[:, k:k + 1] * y_ref[...]
    x2 = x1_ref[...] + mod_ref[0, 5:6, :] * moe
    out_ref[...] = _rms(x2, nfw_ref[...])


def _combine(y4, x1, rw, mod3, norm_f_w, seq):
    n = x1.shape[0]
    tm = TM_COMBINE
    tpb = seq // tm
    tok = lambda i: (i, 0)
    y_specs = [pl.BlockSpec((tm, D_MODEL),
                            functools.partial(lambda i, k: (k * (n // tm) + i, 0), k=k))
               for k in range(TOP_K)]
    return pl.pallas_call(
        _combine_kernel,
        grid=(n // tm,),
        in_specs=y_specs + [
            pl.BlockSpec((tm, D_MODEL), tok),
            pl.BlockSpec((tm, LANES), tok),
            pl.BlockSpec((1, N_MOD, D_MODEL), lambda i: (i // tpb, 0, 0)),
            pl.BlockSpec((1, D_MODEL), lambda i: (0, 0)),
        ],
        out_specs=pl.BlockSpec((tm, D_MODEL), tok),
        out_shape=jax.ShapeDtypeStruct((n, D_MODEL), F32),
        compiler_params=pltpu.CompilerParams(
            dimension_semantics=("parallel",), vmem_limit_bytes=VMEM_LIMIT),
        name="combine",
    )(y4, y4, y4, y4, x1, rw, mod3, norm_f_w)


def _pad_lanes(v, fill=0.0):
    out = jnp.full((1, LANES), fill, F32)
    return out.at[0, :v.shape[0]].set(v.astype(F32))


def kernel(x, c, w_ada, b_ada, norm1_w, w_in, gm_vnorm_w, gm_w_spatial, gm_b_spatial, gdn_conv_w,
           gdn_a_log, gdn_dt_bias, gdn_onorm_w, w_out, norm2_w, w_router, b_router, w_gu, b_gu,
           w_down, b_down, norm_f_w):
    bsz, seq, d = x.shape
    n = bsz * seq
    assert w_ada.shape[0] == 1, "the closing RMSNorm is fused into the single layer's combine call"
    l = 0
    x2d = x.reshape(n, d)
    c_pad = jnp.zeros((SUBLANES, d), F32).at[:bsz].set(c)

    mod = _adaln(c_pad, w_ada[l], b_ada[l][None, :])[:bsz]
    mod3 = mod.reshape(bsz, N_MOD, d)

    w_main = w_in[l][:, :PROJ_MAIN].astype(BF16)
    w_ab = jnp.zeros((d, LANES), BF16).at[:, :2 * GDN_HEADS].set(
        w_in[l][:, PROJ_MAIN:].astype(BF16))
    proj, ab = _inproj(x2d, mod3, norm1_w[l][None, :], w_main, w_ab, seq)

    ya, qt, kt, wmat, umat, qk, dec = _mix_prep(
        proj, ab, gm_vnorm_w[l], gm_w_spatial[l], gm_b_spatial[l].T, gdn_conv_w[l],
        _pad_lanes(gdn_a_log[l]), _pad_lanes(gdn_dt_bias[l]), seq)
    yb = _gdn_scan(qt, kt, wmat, umat, qk, dec, proj, gdn_onorm_w[l][None, :], bsz, seq)

    wr_pad = jnp.zeros((d, LANES), F32).at[:, :N_EXPERTS].set(w_router[l])
    wr_hi = wr_pad.astype(BF16)
    wr_lo = (wr_pad - wr_hi.astype(F32)).astype(BF16)
    x1, h2, ridx, rw, cnt = _out_router(
        ya, yb, x2d, mod3, w_out[l].astype(BF16), norm2_w[l][None, :], wr_hi, wr_lo,
        _pad_lanes(b_router[l]), seq)

    t = MOE_BLOCK
    n_assign = n * TOP_K
    n_blocks = n_assign // t + N_EXPERTS
    e_flat = ridx[:, :TOP_K].T.reshape(n_assign)
    keys = e_flat * n_assign + jnp.arange(n_assign, dtype=jnp.int32)
    asg = jnp.sort(keys) % n_assign
    asg = jnp.concatenate([asg, jnp.arange(t + LANES, dtype=jnp.int32)])
    counts = cnt[0, :N_EXPERTS].astype(jnp.int32)
    start = jnp.cumsum(counts) - counts
    nblk = (counts + t - 1) // t
    blk_end = jnp.cumsum(nblk)
    blk_start = blk_end - nblk
    nvb = blk_end[-1]
    bid = jnp.arange(-1, n_blocks + 2, dtype=jnp.int32)
    live = (bid >= 0) & (bid < nvb)
    e_of = jnp.minimum(jnp.sum(blk_end[None, :] <= jnp.clip(bid, 0, nvb - 1)[:, None], axis=1),
                       N_EXPERTS - 1).astype(jnp.int32)
    j_of = bid - blk_start[e_of]
    row_start = jnp.where(live, start[e_of] + j_of * t, 0).astype(jnp.int32)
    n_valid = jnp.where(live, jnp.clip(counts[e_of] - j_of * t, 0, t), 0).astype(jnp.int32)
    block_e = e_of[1:n_blocks + 1]

    y4 = _moe_ffn(block_e, row_start, n_valid, asg, h2, w_gu[l].astype(BF16), b_gu[l][:, None, :],
                  w_down[l].astype(BF16), b_down[l][:, None, :], n_blocks)
    out = _combine(y4, x1, rw, mod3, norm_f_w[None, :], seq)
    return out.reshape(bsz, seq, d)
```

```python
import functools

import jax
import jax.numpy as jnp
from jax import lax
from jax.experimental import pallas as pl
from jax.experimental.pallas import tpu as pltpu

F32 = jnp.float32
BF16 = jnp.bfloat16
HIGHEST = lax.Precision.HIGHEST

D_MODEL = 1024
GM_GROUPS = 4
GM_DIM = 128
GM_WIDTH = GM_GROUPS * GM_DIM
GM_CHUNK = 128
GDN_HEADS = 4
GDN_DK = 128
GDN_DV = 128
GDN_WIDTH = GDN_HEADS * GDN_DK
GDN_CONV = 4
GDN_CHUNK = 64
GDN_PAIR = 2 * GDN_CHUNK
N_EXPERTS = 32
TOP_K = 4
D_FF = D_MODEL
SWIGLU_LIMIT = 7.0
SWIGLU_ALPHA = 1.702
N_MOD = 6
EPS = 1e-6

LANES = 128
SUBLANES = 8
PROJ_MAIN = 2 * GM_WIDTH + 4 * GDN_WIDTH

TM_INPROJ = 512
TM_PREP = 256
TM_SCAN = 256
TM_ROUTER = 256
MOE_BLOCK = 256
TM_COMBINE = 128
FF_TILE = 512
ROW_TILE = D_MODEL // LANES
IDX_RING = 4
VMEM_LIMIT = 48 * 1024 * 1024


def _dot(a, b):
    return jnp.dot(a, b, preferred_element_type=F32)


def _dot_nt(a, b):
    return lax.dot_general(a, b, (((1,), (1,)), ((), ())), preferred_element_type=F32)


def _dot_tn(a, b):
    return lax.dot_general(a, b, (((0,), (0,)), ((), ())), preferred_element_type=F32)


def _split3(x):
    hi = x.astype(BF16)
    r1 = x - hi.astype(F32)
    mid = r1.astype(BF16)
    lo = (r1 - mid.astype(F32)).astype(BF16)
    return hi, mid, lo


def _rms(x, w):
    return x * lax.rsqrt(jnp.mean(x * x, axis=-1, keepdims=True) + EPS) * w


def _gelu(x):
    return 0.5 * x * (1.0 + lax.erf(x * (2.0 ** -0.5)))


def _silu(x):
    return x * jax.nn.sigmoid(x)


def _adaln_kernel(c_ref, w_ref, b_ref, o_ref):
    c = c_ref[...]
    o_ref[...] = jnp.dot(_silu(c), w_ref[...], precision=HIGHEST,
                         preferred_element_type=F32) + b_ref[...]


def _adaln(c_pad, w_ada, b_ada):
    rows = c_pad.shape[0]
    n_out = w_ada.shape[1]
    return pl.pallas_call(
        _adaln_kernel,
        grid=(n_out // D_MODEL,),
        in_specs=[
            pl.BlockSpec((rows, D_MODEL), lambda j: (0, 0)),
            pl.BlockSpec((D_MODEL, D_MODEL), lambda j: (0, j)),
            pl.BlockSpec((1, D_MODEL), lambda j: (0, j)),
        ],
        out_specs=pl.BlockSpec((rows, D_MODEL), lambda j: (0, j)),
        out_shape=jax.ShapeDtypeStruct((rows, n_out), F32),
        name="adaln",
    )(c_pad, w_ada, b_ada)


def _inproj_kernel(x_ref, mod_ref, nw_ref, w_ref, wab_ref, proj_ref, ab_ref):
    h = _rms(x_ref[...], nw_ref[...]) * (1.0 + mod_ref[0, 1:2, :]) + mod_ref[0, 0:1, :]
    hb = h.astype(BF16)
    for j in range(PROJ_MAIN // 512):
        cols = slice(j * 512, (j + 1) * 512)
        proj_ref[:, cols] = _dot(hb, w_ref[:, cols]).astype(BF16)
    ab_ref[...] = _dot(hb, wab_ref[...])


def _inproj(x2d, mod3, norm_w, w_main, w_ab, seq):
    n = x2d.shape[0]
    tiles_per_batch = seq // TM_INPROJ
    return pl.pallas_call(
        _inproj_kernel,
        grid=(n // TM_INPROJ,),
        in_specs=[
            pl.BlockSpec((TM_INPROJ, D_MODEL), lambda i: (i, 0)),
            pl.BlockSpec((1, N_MOD, D_MODEL), lambda i: (i // tiles_per_batch, 0, 0)),
            pl.BlockSpec((1, D_MODEL), lambda i: (0, 0)),
            pl.BlockSpec((D_MODEL, PROJ_MAIN), lambda i: (0, 0)),
            pl.BlockSpec((D_MODEL, LANES), lambda i: (0, 0)),
        ],
        out_specs=[
            pl.BlockSpec((TM_INPROJ, PROJ_MAIN), lambda i: (i, 0)),
            pl.BlockSpec((TM_INPROJ, LANES), lambda i: (i, 0)),
        ],
        out_shape=[
            jax.ShapeDtypeStruct((n, PROJ_MAIN), BF16),
            jax.ShapeDtypeStruct((n, LANES), F32),
        ],
        compiler_params=pltpu.CompilerParams(
            dimension_semantics=("parallel",), vmem_limit_bytes=VMEM_LIMIT),
        name="in_proj",
    )(x2d, mod3, norm_w, w_main, w_ab)


def _unit_lower_inverses(a_list):
    c = a_list[0].shape[0]
    row = lax.broadcasted_iota(jnp.int32, (c, c), 0)
    col = lax.broadcasted_iota(jnp.int32, (c, c), 1)
    eye = jnp.where(row == col, 1.0, 0.0).astype(F32)
    ps = [eye - a for a in a_list]
    qs = [a.astype(BF16) for a in a_list]
    qs = [_dot(q, q) for q in qs]
    power = 2
    while 2 * power < GDN_CHUNK:
        qbs = [q.astype(BF16) for q in qs]
        ps = [p + _dot(p.astype(BF16), qb) for p, qb in zip(ps, qbs)]
        qs = [_dot(qb, qb) for qb in qbs]
        power *= 2
    return [p + _dot(p.astype(BF16), q.astype(BF16)) for p, q in zip(ps, qs)]


def _mix_prep_kernel(proj_ref, ab_ref, vnw_ref, wsp_ref, bsp_ref, cw_ref, alog_ref, dtb_ref,
                     ya_ref, qt_ref, kt_ref, w_ref, u_ref, qk_ref, dec_ref, ext_ref,
                     *, tiles_per_batch):
    tm = TM_PREP
    i = pl.program_id(0)

    row = lax.broadcasted_iota(jnp.int32, (GM_CHUNK, GM_CHUNK), 0)
    col = lax.broadcasted_iota(jnp.int32, (GM_CHUNK, GM_CHUNK), 1)
    causal = row >= col
    for g in range(GM_GROUPS):
        ws = jnp.where(causal, wsp_ref[g], 0.0).astype(BF16)
        bcol = bsp_ref[:, g:g + 1]
        cols_u = slice(g * GM_DIM, (g + 1) * GM_DIM)
        cols_v = slice(GM_WIDTH + g * GM_DIM, GM_WIDTH + (g + 1) * GM_DIM)
        for c in range(tm // GM_CHUNK):
            rows = slice(c * GM_CHUNK, (c + 1) * GM_CHUNK)
            u = _gelu(proj_ref[rows, cols_u].astype(F32))
            v = _rms(_gelu(proj_ref[rows, cols_v].astype(F32)), vnw_ref[g:g + 1, :])
            z = _dot(ws, v.astype(BF16)) + bcol
            ya_ref[rows, cols_u] = (u * z).astype(BF16)

    @pl.when(i % tiles_per_batch == 0)
    def _():
        ext_ref[0:SUBLANES, :] = jnp.zeros((SUBLANES, 3 * GDN_WIDTH), F32)

    qkv_cols = slice(2 * GM_WIDTH, 2 * GM_WIDTH + 3 * GDN_WIDTH)
    ext_ref[SUBLANES:SUBLANES + tm, :] = proj_ref[:, qkv_cols].astype(F32)
    conv = jnp.zeros((tm, 3 * GDN_WIDTH), F32)
    for j in range(GDN_CONV):
        start = SUBLANES - (GDN_CONV - 1) + j
        conv = conv + cw_ref[j:j + 1, :] * ext_ref[start:start + tm, :]
    ext_ref[0:SUBLANES, :] = ext_ref[tm:tm + SUBLANES, :]
    act = _silu(conv)

    ab = ab_ref[...]
    sp_in = ab + dtb_ref[...]
    g_all = -jnp.exp(alog_ref[...]) * (
        jnp.maximum(sp_in, 0.0) + jnp.log1p(jnp.exp(-jnp.abs(sp_in))))
    beta_all = jax.nn.sigmoid(ab)
    trow = lax.broadcasted_iota(jnp.int32, (tm, tm), 0)
    tcol = lax.broadcasted_iota(jnp.int32, (tm, tm), 1)
    blk_lower = jnp.where((trow >= tcol) & (trow // GDN_CHUNK == tcol // GDN_CHUNK),
                          1.0, 0.0).astype(BF16)
    g_hi, g_mid, g_lo = _split3(g_all)
    gc_all = _dot(blk_lower, g_hi) + _dot(blk_lower, g_mid) + _dot(blk_lower, g_lo)
    gc_all_t = gc_all.T

    prow = lax.broadcasted_iota(jnp.int32, (GDN_PAIR, GDN_PAIR), 0)
    pcol = lax.broadcasted_iota(jnp.int32, (GDN_PAIR, GDN_PAIR), 1)
    same_chunk = prow // GDN_CHUNK == pcol // GDN_CHUNK
    tri = (prow >= pcol) & same_chunk
    strict = (prow > pcol) & same_chunk
    first_half = lax.broadcasted_iota(jnp.int32, (GDN_PAIR, 1), 0) < GDN_CHUNK

    blocks = [(h, p) for h in range(GDN_HEADS) for p in range(tm // GDN_PAIR)]
    q_l, k_l, kb_l, kbf_l, beta_l, gcc_l, decay_l, v_l = [], [], [], [], [], [], [], []
    for h in range(GDN_HEADS):
        hq = slice(h * GDN_DK, (h + 1) * GDN_DK)
        hk = slice(GDN_WIDTH + h * GDN_DK, GDN_WIDTH + (h + 1) * GDN_DK)
        hv = slice(2 * GDN_WIDTH + h * GDN_DV, 2 * GDN_WIDTH + (h + 1) * GDN_DV)
        q_h = act[:, hq]
        k_h = act[:, hk]
        q_h = q_h * lax.rsqrt(jnp.sum(q_h * q_h, axis=-1, keepdims=True) + EPS) * (GDN_DK ** -0.5)
        k_h = k_h * lax.rsqrt(jnp.sum(k_h * k_h, axis=-1, keepdims=True) + EPS)
        v_h = act[:, hv]
        for p in range(tm // GDN_PAIR):
            rows = slice(p * GDN_PAIR, (p + 1) * GDN_PAIR)
            beta = beta_all[rows, GDN_HEADS + h:GDN_HEADS + h + 1]
            gcc = gc_all[rows, h:h + 1]
            gcr = gc_all_t[h:h + 1, p * GDN_PAIR:(p + 1) * GDN_PAIR]
            k = k_h[rows]
            q_l.append(q_h[rows])
            k_l.append(k)
            kb_l.append(k * beta)
            kbf_l.append(k.astype(BF16))
            beta_l.append(beta)
            gcc_l.append(gcc)
            v_l.append(v_h[rows])
            decay_l.append(jnp.where(tri, jnp.exp(jnp.where(tri, gcc - gcr, 0.0)), 0.0))

    kk_l = [_dot_nt(kb.astype(BF16), kbf) for kb, kbf in zip(kb_l, kbf_l)]
    a_l = [jnp.where(strict, kk * decay, 0.0) for kk, decay in zip(kk_l, decay_l)]
    t_l = _unit_lower_inverses(a_l)
    egc_l = [jnp.exp(gcc) for gcc in gcc_l]
    rhs_l = [jnp.concatenate([v * beta, kb * egc], axis=1).astype(BF16)
             for v, beta, kb, egc in zip(v_l, beta_l, kb_l, egc_l)]
    sol_l = [_dot(t.astype(BF16), rhs) for t, rhs in zip(t_l, rhs_l)]
    qk_l = [jnp.where(tri, _dot_nt(q.astype(BF16), kbf) * decay, 0.0)
            for q, kbf, decay in zip(q_l, kbf_l, decay_l)]

    for idx, (h, p) in enumerate(blocks):
        rows = slice(p * GDN_PAIR, (p + 1) * GDN_PAIR)
        cols = slice(h * GDN_DK, (h + 1) * GDN_DK)
        gcc = gcc_l[idx]
        gl0 = gcc[GDN_CHUNK - 1:GDN_CHUNK]
        gl1 = gcc[GDN_PAIR - 1:GDN_PAIR]
        g_last = jnp.where(first_half, gl0, gl1)
        u_ref[rows, cols] = sol_l[idx][:, :GDN_DV]
        w_ref[rows, cols] = sol_l[idx][:, GDN_DV:].astype(BF16)
        qt_ref[rows, cols] = (q_l[idx] * egc_l[idx]).astype(BF16)
        kt_ref[rows, cols] = (k_l[idx] * jnp.exp(g_last - gcc)).astype(BF16)
        qk_ref[rows, cols] = qk_l[idx].astype(BF16)
        dec_ref[2 * p, h:h + 1, :] = jnp.broadcast_to(jnp.exp(gl0), (1, LANES))
        dec_ref[2 * p + 1, h:h + 1, :] = jnp.broadcast_to(jnp.exp(gl1), (1, LANES))


def _mix_prep(proj, ab, vnorm_w, w_spatial, b_spatial_t, conv_w, alog_pad, dtb_pad, seq):
    n = proj.shape[0]
    tm = TM_PREP
    tiles_per_batch = seq // tm
    const2 = lambda i: (0, 0)
    return pl.pallas_call(
        functools.partial(_mix_prep_kernel, tiles_per_batch=tiles_per_batch),
        grid=(n // tm,),
        in_specs=[
            pl.BlockSpec((tm, PROJ_MAIN), lambda i: (i, 0)),
            pl.BlockSpec((tm, LANES), lambda i: (i, 0)),
            pl.BlockSpec((GM_GROUPS, GM_DIM), const2),
            pl.BlockSpec((GM_GROUPS, GM_CHUNK, GM_CHUNK), lambda i: (0, 0, 0)),
            pl.BlockSpec((GM_CHUNK, GM_GROUPS), const2),
            pl.BlockSpec((GDN_CONV, 3 * GDN_WIDTH), const2),
            pl.BlockSpec((1, LANES), const2),
            pl.BlockSpec((1, LANES), const2),
        ],
        out_specs=[
            pl.BlockSpec((tm, GM_WIDTH), lambda i: (i, 0)),
            pl.BlockSpec((tm, GDN_WIDTH), lambda i: (i, 0)),
            pl.BlockSpec((tm, GDN_WIDTH), lambda i: (i, 0)),
            pl.BlockSpec((tm, GDN_WIDTH), lambda i: (i, 0)),
            pl.BlockSpec((tm, GDN_WIDTH), lambda i: (i, 0)),
            pl.BlockSpec((tm, GDN_WIDTH), lambda i: (i, 0)),
            pl.BlockSpec((tm // GDN_CHUNK, GDN_HEADS, LANES), lambda i: (i, 0, 0)),
        ],
        out_shape=[
            jax.ShapeDtypeStruct((n, GM_WIDTH), BF16),
            jax.ShapeDtypeStruct((n, GDN_WIDTH), BF16),
            jax.ShapeDtypeStruct((n, GDN_WIDTH), BF16),
            jax.ShapeDtypeStruct((n, GDN_WIDTH), BF16),
            jax.ShapeDtypeStruct((n, GDN_WIDTH), F32),
            jax.ShapeDtypeStruct((n, GDN_WIDTH), BF16),
            jax.ShapeDtypeStruct((n // GDN_CHUNK, GDN_HEADS, LANES), F32),
        ],
        scratch_shapes=[pltpu.VMEM((tm + 2 * SUBLANES, 3 * GDN_WIDTH), F32)],
        compiler_params=pltpu.CompilerParams(
            dimension_semantics=("arbitrary",), vmem_limit_bytes=VMEM_LIMIT),
        name="mix_prep",
    )(proj, ab, vnorm_w, w_spatial, b_spatial_t, conv_w, alog_pad, dtb_pad)


def _gdn_scan_kernel(qt_ref, kt_ref, w_ref, u_ref, qk_ref, dec_ref, z_ref, onw_ref,
                     yb_ref, s_ref):
    @pl.when(pl.program_id(1) == 0)
    def _():
        s_ref[...] = jnp.zeros(s_ref.shape, F32)

    heads = range(GDN_HEADS)
    hcols = [slice(h * GDN_DK, (h + 1) * GDN_DK) for h in heads]
    states = [s_ref[h] for h in heads]
    v_prev = [None] * GDN_HEADS
    for c in range(TM_SCAN // GDN_CHUNK):
        rows = slice(c * GDN_CHUNK, (c + 1) * GDN_CHUNK)
        states_b = [s.astype(BF16) for s in states]
        ws = [_dot(w_ref[rows, hcols[h]], states_b[h]) for h in heads]
        qs = [_dot(qt_ref[rows, hcols[h]], states_b[h]) for h in heads]
        v_new = [(u_ref[rows, hcols[h]] - ws[h]).astype(BF16) for h in heads]
        if c % 2 == 0:
            o = [qs[h] + _dot(qk_ref[rows, h * GDN_DK:h * GDN_DK + GDN_CHUNK], v_new[h])
                 for h in heads]
        else:
            o = [qs[h] + _dot(qk_ref[rows, hcols[h]],
                              jnp.concatenate([v_prev[h], v_new[h]], axis=0)) for h in heads]
        states = [states[h] * dec_ref[c, h:h + 1, :] + _dot_tn(kt_ref[rows, hcols[h]], v_new[h])
                  for h in heads]
        v_prev = v_new
        for h in heads:
            zz = z_ref[rows, hcols[h]].astype(F32)
            yb_ref[rows, hcols[h]] = (_rms(o[h], onw_ref[...]) * _silu(zz)).astype(BF16)
    for h in heads:
        s_ref[h] = states[h]


def _gdn_scan(qt, kt, w, u, qk, dec, proj, onorm_w, bsz, seq):
    n = qt.shape[0]
    tm = TM_SCAN
    tpb = seq // tm
    tok = lambda b, j: (b * tpb + j, 0)
    z_block = (2 * GM_WIDTH + 3 * GDN_WIDTH) // GDN_WIDTH
    return pl.pallas_call(
        _gdn_scan_kernel,
        grid=(bsz, tpb),
        in_specs=[
            pl.BlockSpec((tm, GDN_WIDTH), tok),
            pl.BlockSpec((tm, GDN_WIDTH), tok),
            pl.BlockSpec((tm, GDN_WIDTH), tok),
            pl.BlockSpec((tm, GDN_WIDTH), tok),
            pl.BlockSpec((tm, GDN_WIDTH), tok),
            pl.BlockSpec((tm // GDN_CHUNK, GDN_HEADS, LANES), lambda b, j: (b * tpb + j, 0, 0)),
            pl.BlockSpec((tm, GDN_WIDTH), lambda b, j: (b * tpb + j, z_block)),
            pl.BlockSpec((1, GDN_DV), lambda b, j: (0, 0)),
        ],
        out_specs=pl.BlockSpec((tm, GDN_WIDTH), tok),
        out_shape=jax.ShapeDtypeStruct((n, GDN_WIDTH), BF16),
        scratch_shapes=[pltpu.VMEM((GDN_HEADS, GDN_DK, GDN_DV), F32)],
        compiler_params=pltpu.CompilerParams(dimension_semantics=("arbitrary", "arbitrary")),
        name="gdn_scan",
    )(qt, kt, w, u, qk, dec, proj, onorm_w)


def _out_router_kernel(ya_ref, yb_ref, x_ref, mod_ref, wo_ref, n2w_ref, wrh_ref, wrl_ref, br_ref,
                       x1_ref, h2_ref, ridx_ref, rw_ref, cnt_ref, carry_ref):
    tm = TM_ROUTER

    @pl.when(pl.program_id(0) == 0)
    def _():
        carry_ref[...] = jnp.zeros(carry_ref.shape, F32)

    mix = _dot(ya_ref[...], wo_ref[0:GM_WIDTH, :]) + _dot(yb_ref[...], wo_ref[GM_WIDTH:, :])
    x1 = x_ref[...] + mod_ref[0, 2:3, :] * mix
    x1_ref[...] = x1
    h2 = _rms(x1, n2w_ref[...]) * (1.0 + mod_ref[0, 4:5, :]) + mod_ref[0, 3:4, :]
    for j in range(ROW_TILE):
        h2_ref[pl.ds(j, tm, stride=ROW_TILE), :] = h2[:, j * LANES:(j + 1) * LANES]

    h_hi = h2.astype(BF16)
    h_lo = (h2 - h_hi.astype(F32)).astype(BF16)
    logits = (_dot(h_hi, wrh_ref[...]) + _dot(h_hi, wrl_ref[...]) + _dot(h_lo, wrh_ref[...])
              + br_ref[...])
    lane = lax.broadcasted_iota(jnp.int32, (tm, LANES), 1)
    work = jnp.where(lane < N_EXPERTS, logits, -jnp.inf)
    sel_e, sel_v = [], []
    for _ in range(TOP_K):
        m = jnp.max(work, axis=-1, keepdims=True)
        e = jnp.min(jnp.where(work == m, lane, LANES), axis=-1, keepdims=True)
        sel_e.append(e)
        sel_v.append(m)
        work = jnp.where(lane == e, -jnp.inf, work)
    ex = [jnp.exp(v - sel_v[0]) for v in sel_v]
    den = ex[0] + ex[1] + ex[2] + ex[3]

    ridx = jnp.zeros((tm, LANES), jnp.int32)
    rw = jnp.zeros((tm, LANES), F32)
    onehot = jnp.zeros((tm, LANES), F32)
    for k in range(TOP_K):
        onehot = onehot + jnp.where(lane == sel_e[k], 1.0, 0.0)
        ridx = jnp.where(lane == k, sel_e[k], ridx)
        rw = jnp.where(lane == k, ex[k] / den, rw)
    ridx_ref[...] = ridx
    rw_ref[...] = rw
    carry = carry_ref[0:1, :] + jnp.sum(onehot, axis=0, keepdims=True)
    carry_ref[...] = jnp.broadcast_to(carry, carry_ref.shape)
    cnt_ref[...] = jnp.broadcast_to(carry, cnt_ref.shape)


def _out_router(ya, yb, x2d, mod3, w_out, norm2_w, wr_hi, wr_lo, br_pad, seq):
    n = x2d.shape[0]
    tm = TM_ROUTER
    tpb = seq // tm
    tok = lambda i: (i, 0)
    const2 = lambda i: (0, 0)
    return pl.pallas_call(
        _out_router_kernel,
        grid=(n // tm,),
        in_specs=[
            pl.BlockSpec((tm, GM_WIDTH), tok),
            pl.BlockSpec((tm, GDN_WIDTH), tok),
            pl.BlockSpec((tm, D_MODEL), tok),
            pl.BlockSpec((1, N_MOD, D_MODEL), lambda i: (i // tpb, 0, 0)),
            pl.BlockSpec((GM_WIDTH + GDN_WIDTH, D_MODEL), const2),
            pl.BlockSpec((1, D_MODEL), const2),
            pl.BlockSpec((D_MODEL, LANES), const2),
            pl.BlockSpec((D_MODEL, LANES), const2),
            pl.BlockSpec((1, LANES), const2),
        ],
        out_specs=[
            pl.BlockSpec((tm, D_MODEL), tok),
            pl.BlockSpec((tm * ROW_TILE, LANES), tok),
            pl.BlockSpec((tm, LANES), tok),
            pl.BlockSpec((tm, LANES), tok),
            pl.BlockSpec((SUBLANES, LANES), const2),
        ],
        out_shape=[
            jax.ShapeDtypeStruct((n, D_MODEL), F32),
            jax.ShapeDtypeStruct((n * ROW_TILE, LANES), F32),
            jax.ShapeDtypeStruct((n, LANES), jnp.int32),
            jax.ShapeDtypeStruct((n, LANES), F32),
            jax.ShapeDtypeStruct((SUBLANES, LANES), F32),
        ],
        scratch_shapes=[pltpu.VMEM((SUBLANES, LANES), F32)],
        compiler_params=pltpu.CompilerParams(
            dimension_semantics=("arbitrary",), vmem_limit_bytes=VMEM_LIMIT),
        name="out_router",
    )(ya, yb, x2d, mod3, w_out, norm2_w, wr_hi, wr_lo, br_pad)


def _moe_kernel(be_ref, rs_ref, off_ref, nv_ref, asg_hbm, h2_hbm, wgu_ref, bgu_ref, wd_ref, bd_ref,
                y4_hbm, idx_smem, x0, x1, y0, y1, gsem, ssem, isem, *, n_tokens):
    t = MOE_BLOCK
    b = pl.program_id(0)
    n_blocks = pl.num_programs(0)
    dump_row = TOP_K * n_tokens

    ring_row = t + LANES
    rt = ROW_TILE

    def idx_copy(block):
        slot = (block + IDX_RING) % IDX_RING
        first = pl.multiple_of(rs_ref[block + 1], LANES)
        return pltpu.make_async_copy(asg_hbm.at[pl.ds(first, ring_row)],
                                     idx_smem.at[pl.ds(slot * ring_row, ring_row)], isem.at[slot])

    def idx_base(block):
        return ((block + IDX_RING) % IDX_RING) * ring_row + off_ref[block + 1]

    def tile_of(row):
        first = row * rt
        return pl.ds(first if isinstance(first, int) else pl.multiple_of(first, rt), rt)

    def gather_row(base, r, dst, sem):
        tok = idx_smem[base + r] & (n_tokens - 1)
        return pltpu.make_async_copy(h2_hbm.at[tile_of(tok)], dst.at[tile_of(r)], sem)

    def scatter_row(base, n_valid, r, src, sem):
        row = jnp.where(r < n_valid, idx_smem[base + r], dump_row + r)
        return pltpu.make_async_copy(src.at[tile_of(r)], y4_hbm.at[tile_of(row)], sem)

    def wait_gather(dst, sem):
        pltpu.make_async_copy(h2_hbm.at[pl.ds(0, t * rt)], dst, sem).wait()

    def wait_scatter(src, sem):
        pltpu.make_async_copy(src, y4_hbm.at[pl.ds(0, t * rt)], sem).wait()

    def load_rows(buf):
        return jnp.concatenate([buf[pl.ds(j, t, stride=rt), :] for j in range(rt)], axis=1)

    def store_rows(buf, val):
        for j in range(rt):
            buf[pl.ds(j, t, stride=rt), :] = val[:, j * LANES:(j + 1) * LANES]

    @pl.when(b == 0)
    def _():
        idx_copy(-1).start()
        idx_copy(0).start()
        idx_copy(1).start()
        idx_copy(-1).wait()
        idx_copy(0).wait()
        y1[...] = jnp.zeros(y1.shape, F32)

        base0 = idx_base(0)

        def body(r, carry):
            gather_row(base0, r, x0, gsem.at[0]).start()
            return carry
        lax.fori_loop(0, t, body, 0)

    def step(p):
        xb, xo = (x0, x1) if p == 0 else (x1, x0)
        yb, yo = (y0, y1) if p == 0 else (y1, y0)
        idx_copy(b + 2).start()
        idx_copy(b + 1).wait()
        wait_gather(xb, gsem.at[p])

        @pl.when(b >= 1)
        def _():
            wait_scatter(yb, ssem.at[p])

        n_parts = 2 * (D_FF // FF_TILE)
        per = t // n_parts

        gbase = idx_base(b + 1)
        sbase = idx_base(b - 1)
        s_valid = nv_ref[b]

        def issue(part):
            for r in range(part * per, (part + 1) * per):
                gather_row(gbase, r, xo, gsem.at[1 - p]).start()
                scatter_row(sbase, s_valid, r, yo, ssem.at[1 - p]).start()

        xbf = load_rows(xb).astype(BF16)
        acc = None
        for f in range(D_FF // FF_TILE):
            gcols = slice(f * FF_TILE, (f + 1) * FF_TILE)
            ucols = slice(D_FF + f * FF_TILE, D_FF + (f + 1) * FF_TILE)
            issue(2 * f)
            gate = _dot(xbf, wgu_ref[0, :, gcols]) + bgu_ref[0, :, gcols]
            up = _dot(xbf, wgu_ref[0, :, ucols]) + bgu_ref[0, :, ucols]
            gate = jnp.minimum(gate, SWIGLU_LIMIT)
            up = jnp.clip(up, -SWIGLU_LIMIT, SWIGLU_LIMIT)
            act = gate * jax.nn.sigmoid(SWIGLU_ALPHA * gate) * (up + 1.0)
            issue(2 * f + 1)
            part = _dot(act.astype(BF16), wd_ref[0, gcols, :])
            acc = part if acc is None else acc + part
        store_rows(yb, acc + bd_ref[0])

        @pl.when(b == n_blocks - 1)
        def _():
            wait_scatter(yo, ssem.at[1 - p])
            base_b = idx_base(b)
            valid_b = nv_ref[b + 1]

            def body(r, carry):
                scatter_row(base_b, valid_b, r, yb, ssem.at[p]).start()
                return carry
            lax.fori_loop(0, t, body, 0)
            wait_scatter(yb, ssem.at[p])
            wait_gather(xo, gsem.at[1 - p])
            idx_copy(b + 2).wait()

    @pl.when(b % 2 == 0)
    def _():
        step(0)

    @pl.when(b % 2 == 1)
    def _():
        step(1)


def _moe_ffn(block_e, row_start, n_valid, asg, h2, w_gu, b_gu, w_down, b_down, n_blocks):
    t = MOE_BLOCK
    n_tokens = h2.shape[0] // ROW_TILE
    assert n_tokens & (n_tokens - 1) == 0, "token index is masked out of the assignment id"

    def w_map(b, be, rs, off, nv):
        return (be[b], 0, 0)

    grid_spec = pltpu.PrefetchScalarGridSpec(
        num_scalar_prefetch=4,
        grid=(n_blocks,),
        in_specs=[
            pl.BlockSpec(memory_space=pl.ANY),
            pl.BlockSpec(memory_space=pl.ANY),
            pl.BlockSpec((1, D_MODEL, 2 * D_FF), w_map),
            pl.BlockSpec((1, 1, 2 * D_FF), w_map),
            pl.BlockSpec((1, D_FF, D_MODEL), w_map),
            pl.BlockSpec((1, 1, D_MODEL), w_map),
        ],
        out_specs=pl.BlockSpec(memory_space=pl.ANY),
        scratch_shapes=[
            pltpu.SMEM((IDX_RING * (t + LANES),), jnp.int32),
            pltpu.VMEM((t * ROW_TILE, LANES), F32),
            pltpu.VMEM((t * ROW_TILE, LANES), F32),
            pltpu.VMEM((t * ROW_TILE, LANES), F32),
            pltpu.VMEM((t * ROW_TILE, LANES), F32),
            pltpu.SemaphoreType.DMA((2,)),
            pltpu.SemaphoreType.DMA((2,)),
            pltpu.SemaphoreType.DMA((IDX_RING,)),
        ],
    )
    return pl.pallas_call(
        functools.partial(_moe_kernel, n_tokens=n_tokens),
        grid_spec=grid_spec,
        out_shape=jax.ShapeDtypeStruct(((TOP_K * n_tokens + t) * ROW_TILE, LANES), F32),
        compiler_params=pltpu.CompilerParams(
            dimension_semantics=("arbitrary",), vmem_limit_bytes=VMEM_LIMIT),
        name="moe_ffn",
    )(block_e, row_start // LANES * LANES, row_start % LANES, n_valid, asg, h2, w_gu, b_gu, w_down,
      b_down)


def _combine_kernel(ya_ref, yb_ref, yc_ref, yd_ref, x1_ref, rw_ref, mod_ref, nfw_ref, out_ref):
    tm = TM_COMBINE
    rw = rw_ref[...]
    cols = []
    for j in range(ROW_TILE):
        rows = pl.ds(j, tm, stride=ROW_TILE)
        col = rw[:, 0:1] * ya_ref[rows, :]
        for k, y_ref in enumerate((yb_ref, yc_ref, yd_ref), start=1):
            col = col + rw[:, k:k + 1] * y_ref[rows, :]
        cols.append(col)
    x2 = x1_ref[...] + mod_ref[0, 5:6, :] * jnp.concatenate(cols, axis=1)
    out_ref[...] = _rms(x2, nfw_ref[...])


def _combine(y4, x1, rw, mod3, norm_f_w, seq):
    n = x1.shape[0]
    tm = TM_COMBINE
    tpb = seq // tm
    tok = lambda i: (i, 0)
    y_specs = [pl.BlockSpec((tm * ROW_TILE, LANES),
                            functools.partial(lambda i, k: (k * (n // tm) + i, 0), k=k))
               for k in range(TOP_K)]
    return pl.pallas_call(
        _combine_kernel,
        grid=(n // tm,),
        in_specs=y_specs + [
            pl.BlockSpec((tm, D_MODEL), tok),
            pl.BlockSpec((tm, LANES), tok),
            pl.BlockSpec((1, N_MOD, D_MODEL), lambda i: (i // tpb, 0, 0)),
            pl.BlockSpec((1, D_MODEL), lambda i: (0, 0)),
        ],
        out_specs=pl.BlockSpec((tm, D_MODEL), tok),
        out_shape=jax.ShapeDtypeStruct((n, D_MODEL), F32),
        compiler_params=pltpu.CompilerParams(
            dimension_semantics=("parallel",), vmem_limit_bytes=VMEM_LIMIT),
        name="combine",
    )(y4, y4, y4, y4, x1, rw, mod3, norm_f_w)


def _pad_lanes(v, fill=0.0):
    out = jnp.full((1, LANES), fill, F32)
    return out.at[0, :v.shape[0]].set(v.astype(F32))


def kernel(x, c, w_ada, b_ada, norm1_w, w_in, gm_vnorm_w, gm_w_spatial, gm_b_spatial, gdn_conv_w,
           gdn_a_log, gdn_dt_bias, gdn_onorm_w, w_out, norm2_w, w_router, b_router, w_gu, b_gu,
           w_down, b_down, norm_f_w):
    bsz, seq, d = x.shape
    n = bsz * seq
    assert w_ada.shape[0] == 1, "the closing RMSNorm is fused into the single layer's combine call"
    l = 0
    x2d = x.reshape(n, d)
    c_pad = jnp.zeros((SUBLANES, d), F32).at[:bsz].set(c)

    mod = _adaln(c_pad, w_ada[l], b_ada[l][None, :])[:bsz]
    mod3 = mod.reshape(bsz, N_MOD, d)

    w_main = w_in[l][:, :PROJ_MAIN].astype(BF16)
    w_ab = jnp.zeros((d, LANES), BF16).at[:, :2 * GDN_HEADS].set(
        w_in[l][:, PROJ_MAIN:].astype(BF16))
    proj, ab = _inproj(x2d, mod3, norm1_w[l][None, :], w_main, w_ab, seq)

    ya, qt, kt, wmat, umat, qk, dec = _mix_prep(
        proj, ab, gm_vnorm_w[l], gm_w_spatial[l], gm_b_spatial[l].T, gdn_conv_w[l],
        _pad_lanes(gdn_a_log[l]), _pad_lanes(gdn_dt_bias[l]), seq)
    yb = _gdn_scan(qt, kt, wmat, umat, qk, dec, proj, gdn_onorm_w[l][None, :], bsz, seq)

    wr_pad = jnp.zeros((d, LANES), F32).at[:, :N_EXPERTS].set(w_router[l])
    wr_hi = wr_pad.astype(BF16)
    wr_lo = (wr_pad - wr_hi.astype(F32)).astype(BF16)
    x1, h2, ridx, rw, cnt = _out_router(
        ya, yb, x2d, mod3, w_out[l].astype(BF16), norm2_w[l][None, :], wr_hi, wr_lo,
        _pad_lanes(b_router[l]), seq)

    t = MOE_BLOCK
    n_assign = n * TOP_K
    n_blocks = n_assign // t + N_EXPERTS
    e_flat = ridx[:, :TOP_K].T.reshape(n_assign)
    keys = e_flat * n_assign + jnp.arange(n_assign, dtype=jnp.int32)
    asg = jnp.sort(keys) % n_assign
    asg = jnp.concatenate([asg, jnp.arange(t + LANES, dtype=jnp.int32)])
    counts = cnt[0, :N_EXPERTS].astype(jnp.int32)
    start = jnp.cumsum(counts) - counts
    nblk = (counts + t - 1) // t
    blk_end = jnp.cumsum(nblk)
    blk_start = blk_end - nblk
    nvb = blk_end[-1]
    bid = jnp.arange(-1, n_blocks + 2, dtype=jnp.int32)
    live = (bid >= 0) & (bid < nvb)
    e_of = jnp.minimum(jnp.sum(blk_end[None, :] <= jnp.clip(bid, 0, nvb - 1)[:, None], axis=1),
                       N_EXPERTS - 1).astype(jnp.int32)
    j_of = bid - blk_start[e_of]
    row_start = jnp.where(live, start[e_of] + j_of * t, 0).astype(jnp.int32)
    n_valid = jnp.where(live, jnp.clip(counts[e_of] - j_of * t, 0, t), 0).astype(jnp.int32)
    block_e = e_of[1:n_blocks + 1]

    y4 = _moe_ffn(block_e, row_start, n_valid, asg, h2, w_gu[l].astype(BF16), b_gu[l][:, None, :],
                  w_down[l].astype(BF16), b_down[l][:, None, :], n_blocks)
    out = _combine(y4, x1, rw, mod3, norm_f_w[None, :], seq)
    return out.reshape(bsz, seq, d)
```

```python
import functools

import jax
import jax.numpy as jnp
from jax import lax
from jax.experimental import pallas as pl
from jax.experimental.pallas import tpu as pltpu

F32 = jnp.float32
BF16 = jnp.bfloat16
HIGHEST = lax.Precision.HIGHEST

D_MODEL = 1024
GM_GROUPS = 4
GM_DIM = 128
GM_WIDTH = GM_GROUPS * GM_DIM
GM_CHUNK = 128
GDN_HEADS = 4
GDN_DK = 128
GDN_DV = 128
GDN_WIDTH = GDN_HEADS * GDN_DK
GDN_CONV = 4
GDN_CHUNK = 64
GDN_PAIR = 2 * GDN_CHUNK
N_EXPERTS = 32
TOP_K = 4
D_FF = D_MODEL
SWIGLU_LIMIT = 7.0
SWIGLU_ALPHA = 1.702
N_MOD = 6
EPS = 1e-6

LANES = 128
SUBLANES = 8
PROJ_MAIN = 2 * GM_WIDTH + 4 * GDN_WIDTH

TM_INPROJ = 512
TM_PREP = 256
TM_SCAN = 256
TM_ROUTER = 256
MOE_BLOCK = 256
TM_COMBINE = 128
FF_TILE = 512
ROW_TILE = D_MODEL // LANES
IDX_RING = 4
VMEM_LIMIT = 48 * 1024 * 1024


def _dot(a, b):
    return jnp.dot(a, b, preferred_element_type=F32)


def _dot_nt(a, b):
    return lax.dot_general(a, b, (((1,), (1,)), ((), ())), preferred_element_type=F32)


def _dot_tn(a, b):
    return lax.dot_general(a, b, (((0,), (0,)), ((), ())), preferred_element_type=F32)


def _split3(x):
    hi = x.astype(BF16)
    r1 = x - hi.astype(F32)
    mid = r1.astype(BF16)
    lo = (r1 - mid.astype(F32)).astype(BF16)
    return hi, mid, lo


def _rms(x, w):
    return x * lax.rsqrt(jnp.mean(x * x, axis=-1, keepdims=True) + EPS) * w


def _gelu(x):
    return 0.5 * x * (1.0 + lax.erf(x * (2.0 ** -0.5)))


def _silu(x):
    return x * jax.nn.sigmoid(x)


def _adaln_kernel(c_ref, w_ref, b_ref, o_ref):
    c = c_ref[...]
    o_ref[...] = jnp.dot(_silu(c), w_ref[...], precision=HIGHEST,
                         preferred_element_type=F32) + b_ref[...]


def _adaln(c_pad, w_ada, b_ada):
    rows = c_pad.shape[0]
    n_out = w_ada.shape[1]
    return pl.pallas_call(
        _adaln_kernel,
        grid=(n_out // D_MODEL,),
        in_specs=[
            pl.BlockSpec((rows, D_MODEL), lambda j: (0, 0)),
            pl.BlockSpec((D_MODEL, D_MODEL), lambda j: (0, j)),
            pl.BlockSpec((1, D_MODEL), lambda j: (0, j)),
        ],
        out_specs=pl.BlockSpec((rows, D_MODEL), lambda j: (0, j)),
        out_shape=jax.ShapeDtypeStruct((rows, n_out), F32),
        name="adaln",
    )(c_pad, w_ada, b_ada)


def _inproj_kernel(x_ref, mod_ref, nw_ref, w_ref, wab_ref, proj_ref, ab_ref):
    h = _rms(x_ref[...], nw_ref[...]) * (1.0 + mod_ref[0, 1:2, :]) + mod_ref[0, 0:1, :]
    hb = h.astype(BF16)
    for j in range(PROJ_MAIN // 512):
        cols = slice(j * 512, (j + 1) * 512)
        proj_ref[:, cols] = _dot(hb, w_ref[:, cols]).astype(BF16)
    ab_ref[...] = _dot(hb, wab_ref[...])


def _inproj(x2d, mod3, norm_w, w_main, w_ab, seq):
    n = x2d.shape[0]
    tiles_per_batch = seq // TM_INPROJ
    return pl.pallas_call(
        _inproj_kernel,
        grid=(n // TM_INPROJ,),
        in_specs=[
            pl.BlockSpec((TM_INPROJ, D_MODEL), lambda i: (i, 0)),
            pl.BlockSpec((1, N_MOD, D_MODEL), lambda i: (i // tiles_per_batch, 0, 0)),
            pl.BlockSpec((1, D_MODEL), lambda i: (0, 0)),
            pl.BlockSpec((D_MODEL, PROJ_MAIN), lambda i: (0, 0)),
            pl.BlockSpec((D_MODEL, LANES), lambda i: (0, 0)),
        ],
        out_specs=[
            pl.BlockSpec((TM_INPROJ, PROJ_MAIN), lambda i: (i, 0)),
            pl.BlockSpec((TM_INPROJ, LANES), lambda i: (i, 0)),
        ],
        out_shape=[
            jax.ShapeDtypeStruct((n, PROJ_MAIN), BF16),
            jax.ShapeDtypeStruct((n, LANES), F32),
        ],
        compiler_params=pltpu.CompilerParams(
            dimension_semantics=("parallel",), vmem_limit_bytes=VMEM_LIMIT),
        name="in_proj",
    )(x2d, mod3, norm_w, w_main, w_ab)


def _unit_lower_inverses(a_list):
    c = a_list[0].shape[0]
    row = lax.broadcasted_iota(jnp.int32, (c, c), 0)
    col = lax.broadcasted_iota(jnp.int32, (c, c), 1)
    eye = jnp.where(row == col, 1.0, 0.0).astype(F32)
    ps = [eye - a for a in a_list]
    qs = [a.astype(BF16) for a in a_list]
    qs = [_dot(q, q) for q in qs]
    power = 2
    while 2 * power < GDN_CHUNK:
        qbs = [q.astype(BF16) for q in qs]
        ps = [p + _dot(p.astype(BF16), qb) for p, qb in zip(ps, qbs)]
        qs = [_dot(qb, qb) for qb in qbs]
        power *= 2
    return [p + _dot(p.astype(BF16), q.astype(BF16)) for p, q in zip(ps, qs)]


def _mix_prep_kernel(proj_ref, ab_ref, vnw_ref, wsp_ref, bsp_ref, cw_ref, alog_ref, dtb_ref,
                     ya_ref, qt_ref, kt_ref, w_ref, u_ref, qk_ref, dec_ref, ext_ref,
                     *, tiles_per_batch):
    tm = TM_PREP
    i = pl.program_id(0)

    row = lax.broadcasted_iota(jnp.int32, (GM_CHUNK, GM_CHUNK), 0)
    col = lax.broadcasted_iota(jnp.int32, (GM_CHUNK, GM_CHUNK), 1)
    causal = row >= col
    for g in range(GM_GROUPS):
        ws = jnp.where(causal, wsp_ref[g], 0.0).astype(BF16)
        bcol = bsp_ref[:, g:g + 1]
        cols_u = slice(g * GM_DIM, (g + 1) * GM_DIM)
        cols_v = slice(GM_WIDTH + g * GM_DIM, GM_WIDTH + (g + 1) * GM_DIM)
        for c in range(tm // GM_CHUNK):
            rows = slice(c * GM_CHUNK, (c + 1) * GM_CHUNK)
            u = _gelu(proj_ref[rows, cols_u].astype(F32))
            v = _rms(_gelu(proj_ref[rows, cols_v].astype(F32)), vnw_ref[g:g + 1, :])
            z = _dot(ws, v.astype(BF16)) + bcol
            ya_ref[rows, cols_u] = (u * z).astype(BF16)

    @pl.when(i % tiles_per_batch == 0)
    def _():
        ext_ref[0:SUBLANES, :] = jnp.zeros((SUBLANES, 3 * GDN_WIDTH), F32)

    qkv_cols = slice(2 * GM_WIDTH, 2 * GM_WIDTH + 3 * GDN_WIDTH)
    ext_ref[SUBLANES:SUBLANES + tm, :] = proj_ref[:, qkv_cols].astype(F32)
    conv = jnp.zeros((tm, 3 * GDN_WIDTH), F32)
    for j in range(GDN_CONV):
        start = SUBLANES - (GDN_CONV - 1) + j
        conv = conv + cw_ref[j:j + 1, :] * ext_ref[start:start + tm, :]
    ext_ref[0:SUBLANES, :] = ext_ref[tm:tm + SUBLANES, :]
    act = _silu(conv)

    ab = ab_ref[...]
    sp_in = ab + dtb_ref[...]
    g_all = -jnp.exp(alog_ref[...]) * (
        jnp.maximum(sp_in, 0.0) + jnp.log1p(jnp.exp(-jnp.abs(sp_in))))
    beta_all = jax.nn.sigmoid(ab)
    trow = lax.broadcasted_iota(jnp.int32, (tm, tm), 0)
    tcol = lax.broadcasted_iota(jnp.int32, (tm, tm), 1)
    blk_lower = jnp.where((trow >= tcol) & (trow // GDN_CHUNK == tcol // GDN_CHUNK),
                          1.0, 0.0).astype(BF16)
    g_hi, g_mid, g_lo = _split3(g_all)
    gc_all = _dot(blk_lower, g_hi) + _dot(blk_lower, g_mid) + _dot(blk_lower, g_lo)
    gc_all_t = gc_all.T

    prow = lax.broadcasted_iota(jnp.int32, (GDN_PAIR, GDN_PAIR), 0)
    pcol = lax.broadcasted_iota(jnp.int32, (GDN_PAIR, GDN_PAIR), 1)
    same_chunk = prow // GDN_CHUNK == pcol // GDN_CHUNK
    tri = (prow >= pcol) & same_chunk
    strict = (prow > pcol) & same_chunk
    first_half = lax.broadcasted_iota(jnp.int32, (GDN_PAIR, 1), 0) < GDN_CHUNK

    blocks = [(h, p) for h in range(GDN_HEADS) for p in range(tm // GDN_PAIR)]
    q_l, k_l, kb_l, kbf_l, beta_l, gcc_l, decay_l, v_l = [], [], [], [], [], [], [], []
    for h in range(GDN_HEADS):
        hq = slice(h * GDN_DK, (h + 1) * GDN_DK)
        hk = slice(GDN_WIDTH + h * GDN_DK, GDN_WIDTH + (h + 1) * GDN_DK)
        hv = slice(2 * GDN_WIDTH + h * GDN_DV, 2 * GDN_WIDTH + (h + 1) * GDN_DV)
        q_h = act[:, hq]
        k_h = act[:, hk]
        q_h = q_h * lax.rsqrt(jnp.sum(q_h * q_h, axis=-1, keepdims=True) + EPS) * (GDN_DK ** -0.5)
        k_h = k_h * lax.rsqrt(jnp.sum(k_h * k_h, axis=-1, keepdims=True) + EPS)
        v_h = act[:, hv]
        for p in range(tm // GDN_PAIR):
            rows = slice(p * GDN_PAIR, (p + 1) * GDN_PAIR)
            beta = beta_all[rows, GDN_HEADS + h:GDN_HEADS + h + 1]
            gcc = gc_all[rows, h:h + 1]
            gcr = gc_all_t[h:h + 1, p * GDN_PAIR:(p + 1) * GDN_PAIR]
            k = k_h[rows]
            q_l.append(q_h[rows])
            k_l.append(k)
            kb_l.append(k * beta)
            kbf_l.append(k.astype(BF16))
            beta_l.append(beta)
            gcc_l.append(gcc)
            v_l.append(v_h[rows])
            decay_l.append(jnp.where(tri, jnp.exp(jnp.where(tri, gcc - gcr, 0.0)), 0.0))

    kk_l = [_dot_nt(kb.astype(BF16), kbf) for kb, kbf in zip(kb_l, kbf_l)]
    a_l = [jnp.where(strict, kk * decay, 0.0) for kk, decay in zip(kk_l, decay_l)]
    t_l = _unit_lower_inverses(a_l)
    egc_l = [jnp.exp(gcc) for gcc in gcc_l]
    rhs_l = [jnp.concatenate([v * beta, kb * egc], axis=1).astype(BF16)
             for v, beta, kb, egc in zip(v_l, beta_l, kb_l, egc_l)]
    sol_l = [_dot(t.astype(BF16), rhs) for t, rhs in zip(t_l, rhs_l)]
    qk_l = [jnp.where(tri, _dot_nt(q.astype(BF16), kbf) * decay, 0.0)
            for q, kbf, decay in zip(q_l, kbf_l, decay_l)]

    for idx, (h, p) in enumerate(blocks):
        rows = slice(p * GDN_PAIR, (p + 1) * GDN_PAIR)
        cols = slice(h * GDN_DK, (h + 1) * GDN_DK)
        gcc = gcc_l[idx]
        gl0 = gcc[GDN_CHUNK - 1:GDN_CHUNK]
        gl1 = gcc[GDN_PAIR - 1:GDN_PAIR]
        g_last = jnp.where(first_half, gl0, gl1)
        u_ref[rows, cols] = sol_l[idx][:, :GDN_DV]
        w_ref[rows, cols] = sol_l[idx][:, GDN_DV:].astype(BF16)
        qt_ref[rows, cols] = (q_l[idx] * egc_l[idx]).astype(BF16)
        kt_ref[rows, cols] = (k_l[idx] * jnp.exp(g_last - gcc)).astype(BF16)
        qk_ref[rows, cols] = qk_l[idx].astype(BF16)
        dec_ref[2 * p, h:h + 1, :] = jnp.broadcast_to(jnp.exp(gl0), (1, LANES))
        dec_ref[2 * p + 1, h:h + 1, :] = jnp.broadcast_to(jnp.exp(gl1), (1, LANES))


def _mix_prep(proj, ab, vnorm_w, w_spatial, b_spatial_t, conv_w, alog_pad, dtb_pad, seq):
    n = proj.shape[0]
    tm = TM_PREP
    tiles_per_batch = seq // tm
    const2 = lambda i: (0, 0)
    return pl.pallas_call(
        functools.partial(_mix_prep_kernel, tiles_per_batch=tiles_per_batch),
        grid=(n // tm,),
        in_specs=[
            pl.BlockSpec((tm, PROJ_MAIN), lambda i: (i, 0)),
            pl.BlockSpec((tm, LANES), lambda i: (i, 0)),
            pl.BlockSpec((GM_GROUPS, GM_DIM), const2),
            pl.BlockSpec((GM_GROUPS, GM_CHUNK, GM_CHUNK), lambda i: (0, 0, 0)),
            pl.BlockSpec((GM_CHUNK, GM_GROUPS), const2),
            pl.BlockSpec((GDN_CONV, 3 * GDN_WIDTH), const2),
            pl.BlockSpec((1, LANES), const2),
            pl.BlockSpec((1, LANES), const2),
        ],
        out_specs=[
            pl.BlockSpec((tm, GM_WIDTH), lambda i: (i, 0)),
            pl.BlockSpec((tm, GDN_WIDTH), lambda i: (i, 0)),
            pl.BlockSpec((tm, GDN_WIDTH), lambda i: (i, 0)),
            pl.BlockSpec((tm, GDN_WIDTH), lambda i: (i, 0)),
            pl.BlockSpec((tm, GDN_WIDTH), lambda i: (i, 0)),
            pl.BlockSpec((tm, GDN_WIDTH), lambda i: (i, 0)),
            pl.BlockSpec((tm // GDN_CHUNK, GDN_HEADS, LANES), lambda i: (i, 0, 0)),
        ],
        out_shape=[
            jax.ShapeDtypeStruct((n, GM_WIDTH), BF16),
            jax.ShapeDtypeStruct((n, GDN_WIDTH), BF16),
            jax.ShapeDtypeStruct((n, GDN_WIDTH), BF16),
            jax.ShapeDtypeStruct((n, GDN_WIDTH), BF16),
            jax.ShapeDtypeStruct((n, GDN_WIDTH), F32),
            jax.ShapeDtypeStruct((n, GDN_WIDTH), BF16),
            jax.ShapeDtypeStruct((n // GDN_CHUNK, GDN_HEADS, LANES), F32),
        ],
        scratch_shapes=[pltpu.VMEM((tm + 2 * SUBLANES, 3 * GDN_WIDTH), F32)],
        compiler_params=pltpu.CompilerParams(
            dimension_semantics=("arbitrary",), vmem_limit_bytes=VMEM_LIMIT),
        name="mix_prep",
    )(proj, ab, vnorm_w, w_spatial, b_spatial_t, conv_w, alog_pad, dtb_pad)


def _gdn_scan_kernel(qt_ref, kt_ref, w_ref, u_ref, qk_ref, dec_ref, z_ref, onw_ref,
                     yb_ref, s_ref):
    @pl.when(pl.program_id(1) == 0)
    def _():
        s_ref[...] = jnp.zeros(s_ref.shape, F32)

    heads = range(GDN_HEADS)
    hcols = [slice(h * GDN_DK, (h + 1) * GDN_DK) for h in heads]
    states = [s_ref[h] for h in heads]
    v_prev = [None] * GDN_HEADS
    for c in range(TM_SCAN // GDN_CHUNK):
        rows = slice(c * GDN_CHUNK, (c + 1) * GDN_CHUNK)
        states_b = [s.astype(BF16) for s in states]
        ws = [_dot(w_ref[rows, hcols[h]], states_b[h]) for h in heads]
        qs = [_dot(qt_ref[rows, hcols[h]], states_b[h]) for h in heads]
        v_new = [(u_ref[rows, hcols[h]] - ws[h]).astype(BF16) for h in heads]
        if c % 2 == 0:
            o = [qs[h] + _dot(qk_ref[rows, h * GDN_DK:h * GDN_DK + GDN_CHUNK], v_new[h])
                 for h in heads]
        else:
            o = [qs[h] + _dot(qk_ref[rows, hcols[h]],
                              jnp.concatenate([v_prev[h], v_new[h]], axis=0)) for h in heads]
        states = [states[h] * dec_ref[c, h:h + 1, :] + _dot_tn(kt_ref[rows, hcols[h]], v_new[h])
                  for h in heads]
        v_prev = v_new
        for h in heads:
            zz = z_ref[rows, hcols[h]].astype(F32)
            yb_ref[rows, hcols[h]] = (_rms(o[h], onw_ref[...]) * _silu(zz)).astype(BF16)
    for h in heads:
        s_ref[h] = states[h]


def _gdn_scan(qt, kt, w, u, qk, dec, proj, onorm_w, bsz, seq):
    n = qt.shape[0]
    tm = TM_SCAN
    tpb = seq // tm
    tok = lambda b, j: (b * tpb + j, 0)
    z_block = (2 * GM_WIDTH + 3 * GDN_WIDTH) // GDN_WIDTH
    return pl.pallas_call(
        _gdn_scan_kernel,
        grid=(bsz, tpb),
        in_specs=[
            pl.BlockSpec((tm, GDN_WIDTH), tok),
            pl.BlockSpec((tm, GDN_WIDTH), tok),
            pl.BlockSpec((tm, GDN_WIDTH), tok),
            pl.BlockSpec((tm, GDN_WIDTH), tok),
            pl.BlockSpec((tm, GDN_WIDTH), tok),
            pl.BlockSpec((tm // GDN_CHUNK, GDN_HEADS, LANES), lambda b, j: (b * tpb + j, 0, 0)),
            pl.BlockSpec((tm, GDN_WIDTH), lambda b, j: (b * tpb + j, z_block)),
            pl.BlockSpec((1, GDN_DV), lambda b, j: (0, 0)),
        ],
        out_specs=pl.BlockSpec((tm, GDN_WIDTH), tok),
        out_shape=jax.ShapeDtypeStruct((n, GDN_WIDTH), BF16),
        scratch_shapes=[pltpu.VMEM((GDN_HEADS, GDN_DK, GDN_DV), F32)],
        compiler_params=pltpu.CompilerParams(dimension_semantics=("arbitrary", "arbitrary")),
        name="gdn_scan",
    )(qt, kt, w, u, qk, dec, proj, onorm_w)


def _out_router_kernel(ya_ref, yb_ref, x_ref, mod_ref, wo_ref, n2w_ref, wrh_ref, wrl_ref, br_ref,
                       x1_ref, h2_ref, ridx_ref, rw_ref, cnt_ref, carry_ref):
    tm = TM_ROUTER

    @pl.when(pl.program_id(0) == 0)
    def _():
        carry_ref[...] = jnp.zeros(carry_ref.shape, F32)

    mix = _dot(ya_ref[...], wo_ref[0:GM_WIDTH, :]) + _dot(yb_ref[...], wo_ref[GM_WIDTH:, :])
    x1 = x_ref[...] + mod_ref[0, 2:3, :] * mix
    x1_ref[...] = x1
    h2 = _rms(x1, n2w_ref[...]) * (1.0 + mod_ref[0, 4:5, :]) + mod_ref[0, 3:4, :]
    for j in range(ROW_TILE):
        h2_ref[pl.ds(j, tm, stride=ROW_TILE), :] = h2[:, j * LANES:(j + 1) * LANES]

    h_hi = h2.astype(BF16)
    h_lo = (h2 - h_hi.astype(F32)).astype(BF16)
    logits = (_dot(h_hi, wrh_ref[...]) + _dot(h_hi, wrl_ref[...]) + _dot(h_lo, wrh_ref[...])
              + br_ref[...])
    lane = lax.broadcasted_iota(jnp.int32, (tm, LANES), 1)
    work = jnp.where(lane < N_EXPERTS, logits, -jnp.inf)
    sel_e, sel_v = [], []
    for _ in range(TOP_K):
        m = jnp.max(work, axis=-1, keepdims=True)
        e = jnp.min(jnp.where(work == m, lane, LANES), axis=-1, keepdims=True)
        sel_e.append(e)
        sel_v.append(m)
        work = jnp.where(lane == e, -jnp.inf, work)
    ex = [jnp.exp(v - sel_v[0]) for v in sel_v]
    den = ex[0] + ex[1] + ex[2] + ex[3]

    ridx = jnp.zeros((tm, LANES), jnp.int32)
    rw = jnp.zeros((tm, LANES), F32)
    onehot = jnp.zeros((tm, LANES), F32)
    for k in range(TOP_K):
        onehot = onehot + jnp.where(lane == sel_e[k], 1.0, 0.0)
        ridx = jnp.where(lane == k, sel_e[k], ridx)
        rw = jnp.where(lane == k, ex[k] / den, rw)
    ridx_ref[...] = ridx
    rw_ref[...] = rw
    carry = carry_ref[0:1, :] + jnp.sum(onehot, axis=0, keepdims=True)
    carry_ref[...] = jnp.broadcast_to(carry, carry_ref.shape)
    cnt_ref[...] = jnp.broadcast_to(carry, cnt_ref.shape)


def _out_router(ya, yb, x2d, mod3, w_out, norm2_w, wr_hi, wr_lo, br_pad, seq):
    n = x2d.shape[0]
    tm = TM_ROUTER
    tpb = seq // tm
    tok = lambda i: (i, 0)
    const2 = lambda i: (0, 0)
    return pl.pallas_call(
        _out_router_kernel,
        grid=(n // tm,),
        in_specs=[
            pl.BlockSpec((tm, GM_WIDTH), tok),
            pl.BlockSpec((tm, GDN_WIDTH), tok),
            pl.BlockSpec((tm, D_MODEL), tok),
            pl.BlockSpec((1, N_MOD, D_MODEL), lambda i: (i // tpb, 0, 0)),
            pl.BlockSpec((GM_WIDTH + GDN_WIDTH, D_MODEL), const2),
            pl.BlockSpec((1, D_MODEL), const2),
            pl.BlockSpec((D_MODEL, LANES), const2),
            pl.BlockSpec((D_MODEL, LANES), const2),
            pl.BlockSpec((1, LANES), const2),
        ],
        out_specs=[
            pl.BlockSpec((tm, D_MODEL), tok),
            pl.BlockSpec((tm * ROW_TILE, LANES), tok),
            pl.BlockSpec((tm, LANES), tok),
            pl.BlockSpec((tm, LANES), tok),
            pl.BlockSpec((SUBLANES, LANES), const2),
        ],
        out_shape=[
            jax.ShapeDtypeStruct((n, D_MODEL), F32),
            jax.ShapeDtypeStruct((n * ROW_TILE, LANES), F32),
            jax.ShapeDtypeStruct((n, LANES), jnp.int32),
            jax.ShapeDtypeStruct((n, LANES), F32),
            jax.ShapeDtypeStruct((SUBLANES, LANES), F32),
        ],
        scratch_shapes=[pltpu.VMEM((SUBLANES, LANES), F32)],
        compiler_params=pltpu.CompilerParams(
            dimension_semantics=("arbitrary",), vmem_limit_bytes=VMEM_LIMIT),
        name="out_router",
    )(ya, yb, x2d, mod3, w_out, norm2_w, wr_hi, wr_lo, br_pad)


def _moe_kernel(be_ref, rs_ref, off_ref, nv_ref, asg_hbm, h2_hbm, wgu_ref, bgu_ref, wd_ref, bd_ref,
                y4_hbm, idx_smem, x0, x1, y0, y1, gsem, ssem, isem, *, n_tokens):
    t = MOE_BLOCK
    b = pl.program_id(0)
    n_blocks = pl.num_programs(0)
    dump_row = TOP_K * n_tokens

    ring_row = t + LANES
    rt = ROW_TILE

    def idx_copy(block):
        slot = (block + IDX_RING) % IDX_RING
        first = pl.multiple_of(rs_ref[block + 1], LANES)
        return pltpu.make_async_copy(asg_hbm.at[pl.ds(first, ring_row)],
                                     idx_smem.at[pl.ds(slot * ring_row, ring_row)], isem.at[slot])

    def idx_base(block):
        return ((block + IDX_RING) % IDX_RING) * ring_row + off_ref[block + 1]

    def tile_of(row):
        first = row * rt
        return pl.ds(first if isinstance(first, int) else pl.multiple_of(first, rt), rt)

    def gather_row(base, r, dst, sem):
        tok = idx_smem[base + r] & (n_tokens - 1)
        return pltpu.make_async_copy(h2_hbm.at[tile_of(tok)], dst.at[tile_of(r)], sem)

    def scatter_row(base, n_valid, r, src, sem):
        row = jnp.where(r < n_valid, idx_smem[base + r], dump_row + r)
        return pltpu.make_async_copy(src.at[tile_of(r)], y4_hbm.at[tile_of(row)], sem)

    def wait_gather(dst, sem):
        pltpu.make_async_copy(h2_hbm.at[pl.ds(0, t * rt)], dst, sem).wait()

    def wait_scatter(src, sem):
        pltpu.make_async_copy(src, y4_hbm.at[pl.ds(0, t * rt)], sem).wait()

    def load_rows(buf):
        return jnp.concatenate([buf[pl.ds(j, t, stride=rt), :] for j in range(rt)], axis=1)

    def store_rows(buf, val):
        for j in range(rt):
            buf[pl.ds(j, t, stride=rt), :] = val[:, j * LANES:(j + 1) * LANES]

    @pl.when(b == 0)
    def _():
        idx_copy(-1).start()
        idx_copy(0).start()
        idx_copy(1).start()
        idx_copy(-1).wait()
        idx_copy(0).wait()
        y1[...] = jnp.zeros(y1.shape, F32)

        base0 = idx_base(0)

        def body(r, carry):
            gather_row(base0, r, x0, gsem.at[0]).start()
            return carry
        lax.fori_loop(0, t, body, 0)

    def step(p):
        xb, xo = (x0, x1) if p == 0 else (x1, x0)
        yb, yo = (y0, y1) if p == 0 else (y1, y0)
        idx_copy(b + 2).start()
        idx_copy(b + 1).wait()
        wait_gather(xb, gsem.at[p])

        @pl.when(b >= 1)
        def _():
            wait_scatter(yb, ssem.at[p])

        n_parts = 2 * (D_FF // FF_TILE)
        per = t // n_parts

        gbase = idx_base(b + 1)
        sbase = idx_base(b - 1)
        s_valid = nv_ref[b]

        def issue(part):
            for r in range(part * per, (part + 1) * per):
                gather_row(gbase, r, xo, gsem.at[1 - p]).start(priority=0)
                scatter_row(sbase, s_valid, r, yo, ssem.at[1 - p]).start(priority=1)

        xbf = load_rows(xb).astype(BF16)
        acc = None
        for f in range(D_FF // FF_TILE):
            gcols = slice(f * FF_TILE, (f + 1) * FF_TILE)
            ucols = slice(D_FF + f * FF_TILE, D_FF + (f + 1) * FF_TILE)
            issue(2 * f)
            gate = _dot(xbf, wgu_ref[0, :, gcols]) + bgu_ref[0, :, gcols]
            up = _dot(xbf, wgu_ref[0, :, ucols]) + bgu_ref[0, :, ucols]
            gate = jnp.minimum(gate, SWIGLU_LIMIT)
            up = jnp.clip(up, -SWIGLU_LIMIT, SWIGLU_LIMIT)
            act = gate * jax.nn.sigmoid(SWIGLU_ALPHA * gate) * (up + 1.0)
            issue(2 * f + 1)
            part = _dot(act.astype(BF16), wd_ref[0, gcols, :])
            acc = part if acc is None else acc + part
        store_rows(yb, acc + bd_ref[0])

        @pl.when(b == n_blocks - 1)
        def _():
            wait_scatter(yo, ssem.at[1 - p])
            base_b = idx_base(b)
            valid_b = nv_ref[b + 1]

            def body(r, carry):
                scatter_row(base_b, valid_b, r, yb, ssem.at[p]).start()
                return carry
            lax.fori_loop(0, t, body, 0)
            wait_scatter(yb, ssem.at[p])
            wait_gather(xo, gsem.at[1 - p])
            idx_copy(b + 2).wait()

    @pl.when(b % 2 == 0)
    def _():
        step(0)

    @pl.when(b % 2 == 1)
    def _():
        step(1)


def _moe_ffn(block_e, row_start, n_valid, asg, h2, w_gu, b_gu, w_down, b_down, n_blocks):
    t = MOE_BLOCK
    n_tokens = h2.shape[0] // ROW_TILE
    assert n_tokens & (n_tokens - 1) == 0, "token index is masked out of the assignment id"

    def w_map(b, be, rs, off, nv):
        return (be[b], 0, 0)

    grid_spec = pltpu.PrefetchScalarGridSpec(
        num_scalar_prefetch=4,
        grid=(n_blocks,),
        in_specs=[
            pl.BlockSpec(memory_space=pl.ANY),
            pl.BlockSpec(memory_space=pl.ANY),
            pl.BlockSpec((1, D_MODEL, 2 * D_FF), w_map),
            pl.BlockSpec((1, 1, 2 * D_FF), w_map),
            pl.BlockSpec((1, D_FF, D_MODEL), w_map),
            pl.BlockSpec((1, 1, D_MODEL), w_map),
        ],
        out_specs=pl.BlockSpec(memory_space=pl.ANY),
        scratch_shapes=[
            pltpu.SMEM((IDX_RING * (t + LANES),), jnp.int32),
            pltpu.VMEM((t * ROW_TILE, LANES), F32),
            pltpu.VMEM((t * ROW_TILE, LANES), F32),
            pltpu.VMEM((t * ROW_TILE, LANES), F32),
            pltpu.VMEM((t * ROW_TILE, LANES), F32),
            pltpu.SemaphoreType.DMA((2,)),
            pltpu.SemaphoreType.DMA((2,)),
            pltpu.SemaphoreType.DMA((IDX_RING,)),
        ],
    )
    return pl.pallas_call(
        functools.partial(_moe_kernel, n_tokens=n_tokens),
        grid_spec=grid_spec,
        out_shape=jax.ShapeDtypeStruct(((TOP_K * n_tokens + t) * ROW_TILE, LANES), F32),
        compiler_params=pltpu.CompilerParams(
            dimension_semantics=("arbitrary",), vmem_limit_bytes=VMEM_LIMIT),
        name="moe_ffn",
    )(block_e, row_start // LANES * LANES, row_start % LANES, n_valid, asg, h2, w_gu, b_gu, w_down,
      b_down)


def _combine_kernel(ya_ref, yb_ref, yc_ref, yd_ref, x1_ref, rw_ref, mod_ref, nfw_ref, out_ref):
    tm = TM_COMBINE
    rw = rw_ref[...]
    cols = []
    for j in range(ROW_TILE):
        rows = pl.ds(j, tm, stride=ROW_TILE)
        col = rw[:, 0:1] * ya_ref[rows, :]
        for k, y_ref in enumerate((yb_ref, yc_ref, yd_ref), start=1):
            col = col + rw[:, k:k + 1] * y_ref[rows, :]
        cols.append(col)
    x2 = x1_ref[...] + mod_ref[0, 5:6, :] * jnp.concatenate(cols, axis=1)
    out_ref[...] = _rms(x2, nfw_ref[...])


def _combine(y4, x1, rw, mod3, norm_f_w, seq):
    n = x1.shape[0]
    tm = TM_COMBINE
    tpb = seq // tm
    tok = lambda i: (i, 0)
    y_specs = [pl.BlockSpec((tm * ROW_TILE, LANES),
                            functools.partial(lambda i, k: (k * (n // tm) + i, 0), k=k))
               for k in range(TOP_K)]
    return pl.pallas_call(
        _combine_kernel,
        grid=(n // tm,),
        in_specs=y_specs + [
            pl.BlockSpec((tm, D_MODEL), tok),
            pl.BlockSpec((tm, LANES), tok),
            pl.BlockSpec((1, N_MOD, D_MODEL), lambda i: (i // tpb, 0, 0)),
            pl.BlockSpec((1, D_MODEL), lambda i: (0, 0)),
        ],
        out_specs=pl.BlockSpec((tm, D_MODEL), tok),
        out_shape=jax.ShapeDtypeStruct((n, D_MODEL), F32),
        compiler_params=pltpu.CompilerParams(
            dimension_semantics=("parallel",), vmem_limit_bytes=VMEM_LIMIT),
        name="combine",
    )(y4, y4, y4, y4, x1, rw, mod3, norm_f_w)


def _pad_lanes(v, fill=0.0):
    out = jnp.full((1, LANES), fill, F32)
    return out.at[0, :v.shape[0]].set(v.astype(F32))


def kernel(x, c, w_ada, b_ada, norm1_w, w_in, gm_vnorm_w, gm_w_spatial, gm_b_spatial, gdn_conv_w,
           gdn_a_log, gdn_dt_bias, gdn_onorm_w, w_out, norm2_w, w_router, b_router, w_gu, b_gu,
           w_down, b_down, norm_f_w):
    bsz, seq, d = x.shape
    n = bsz * seq
    assert w_ada.shape[0] == 1, "the closing RMSNorm is fused into the single layer's combine call"
    l = 0
    x2d = x.reshape(n, d)
    c_pad = jnp.zeros((SUBLANES, d), F32).at[:bsz].set(c)

    mod = _adaln(c_pad, w_ada[l], b_ada[l][None, :])[:bsz]
    mod3 = mod.reshape(bsz, N_MOD, d)

    w_main = w_in[l][:, :PROJ_MAIN].astype(BF16)
    w_ab = jnp.zeros((d, LANES), BF16).at[:, :2 * GDN_HEADS].set(
        w_in[l][:, PROJ_MAIN:].astype(BF16))
    proj, ab = _inproj(x2d, mod3, norm1_w[l][None, :], w_main, w_ab, seq)

    ya, qt, kt, wmat, umat, qk, dec = _mix_prep(
        proj, ab, gm_vnorm_w[l], gm_w_spatial[l], gm_b_spatial[l].T, gdn_conv_w[l],
        _pad_lanes(gdn_a_log[l]), _pad_lanes(gdn_dt_bias[l]), seq)
    yb = _gdn_scan(qt, kt, wmat, umat, qk, dec, proj, gdn_onorm_w[l][None, :], bsz, seq)

    wr_pad = jnp.zeros((d, LANES), F32).at[:, :N_EXPERTS].set(w_router[l])
    wr_hi = wr_pad.astype(BF16)
    wr_lo = (wr_pad - wr_hi.astype(F32)).astype(BF16)
    x1, h2, ridx, rw, cnt = _out_router(
        ya, yb, x2d, mod3, w_out[l].astype(BF16), norm2_w[l][None, :], wr_hi, wr_lo,
        _pad_lanes(b_router[l]), seq)

    t = MOE_BLOCK
    n_assign = n * TOP_K
    n_blocks = n_assign // t + N_EXPERTS
    e_flat = ridx[:, :TOP_K].T.reshape(n_assign)
    keys = e_flat * n_assign + jnp.arange(n_assign, dtype=jnp.int32)
    asg = jnp.sort(keys) % n_assign
    asg = jnp.concatenate([asg, jnp.arange(t + LANES, dtype=jnp.int32)])
    counts = cnt[0, :N_EXPERTS].astype(jnp.int32)
    start = jnp.cumsum(counts) - counts
    nblk = (counts + t - 1) // t
    blk_end = jnp.cumsum(nblk)
    blk_start = blk_end - nblk
    nvb = blk_end[-1]
    bid = jnp.arange(-1, n_blocks + 2, dtype=jnp.int32)
    live = (bid >= 0) & (bid < nvb)
    e_of = jnp.minimum(jnp.sum(blk_end[None, :] <= jnp.clip(bid, 0, nvb - 1)[:, None], axis=1),
                       N_EXPERTS - 1).astype(jnp.int32)
    j_of = bid - blk_start[e_of]
    row_start = jnp.where(live, start[e_of] + j_of * t, 0).astype(jnp.int32)
    n_valid = jnp.where(live, jnp.clip(counts[e_of] - j_of * t, 0, t), 0).astype(jnp.int32)
    block_e = e_of[1:n_blocks + 1]

    y4 = _moe_ffn(block_e, row_start, n_valid, asg, h2, w_gu[l].astype(BF16), b_gu[l][:, None, :],
                  w_down[l].astype(BF16), b_down[l][:, None, :], n_blocks)
    out = _combine(y4, x1, rw, mod3, norm_f_w[None, :], seq)
    return out.reshape(bsz, seq, d)
```

```python
import functools

import jax
import jax.numpy as jnp
from jax import lax
from jax.experimental import pallas as pl
from jax.experimental.pallas import tpu as pltpu

F32 = jnp.float32
BF16 = jnp.bfloat16
HIGHEST = lax.Precision.HIGHEST

D_MODEL = 1024
GM_GROUPS = 4
GM_DIM = 128
GM_WIDTH = GM_GROUPS * GM_DIM
GM_CHUNK = 128
GDN_HEADS = 4
GDN_DK = 128
GDN_DV = 128
GDN_WIDTH = GDN_HEADS * GDN_DK
GDN_CONV = 4
GDN_CHUNK = 64
GDN_PAIR = 2 * GDN_CHUNK
N_EXPERTS = 32
TOP_K = 4
D_FF = D_MODEL
SWIGLU_LIMIT = 7.0
SWIGLU_ALPHA = 1.702
N_MOD = 6
EPS = 1e-6

LANES = 128
SUBLANES = 8
PROJ_MAIN = 2 * GM_WIDTH + 4 * GDN_WIDTH

TM_INPROJ = 512
TM_PREP = 256
TM_SCAN = 256
TM_ROUTER = 256
MOE_BLOCK = 256
TM_COMBINE = 128
FF_TILE = 512
ROW_TILE = D_MODEL // LANES
IDX_RING = 4
VMEM_LIMIT = 48 * 1024 * 1024
MOE_VMEM_LIMIT = 56 * 1024 * 1024


def _dot(a, b):
    return jnp.dot(a, b, preferred_element_type=F32)


def _dot_nt(a, b):
    return lax.dot_general(a, b, (((1,), (1,)), ((), ())), preferred_element_type=F32)


def _dot_tn(a, b):
    return lax.dot_general(a, b, (((0,), (0,)), ((), ())), preferred_element_type=F32)


def _split3(x):
    hi = x.astype(BF16)
    r1 = x - hi.astype(F32)
    mid = r1.astype(BF16)
    lo = (r1 - mid.astype(F32)).astype(BF16)
    return hi, mid, lo


def _rms(x, w):
    return x * lax.rsqrt(jnp.mean(x * x, axis=-1, keepdims=True) + EPS) * w


def _gelu(x):
    return 0.5 * x * (1.0 + lax.erf(x * (2.0 ** -0.5)))


def _silu(x):
    return x * jax.nn.sigmoid(x)


def _adaln_kernel(c_ref, w_ref, b_ref, o_ref):
    c = c_ref[...]
    o_ref[...] = jnp.dot(_silu(c), w_ref[...], precision=HIGHEST,
                         preferred_element_type=F32) + b_ref[...]


def _adaln(c_pad, w_ada, b_ada):
    rows = c_pad.shape[0]
    n_out = w_ada.shape[1]
    return pl.pallas_call(
        _adaln_kernel,
        grid=(n_out // D_MODEL,),
        in_specs=[
            pl.BlockSpec((rows, D_MODEL), lambda j: (0, 0)),
            pl.BlockSpec((D_MODEL, D_MODEL), lambda j: (0, j)),
            pl.BlockSpec((1, D_MODEL), lambda j: (0, j)),
        ],
        out_specs=pl.BlockSpec((rows, D_MODEL), lambda j: (0, j)),
        out_shape=jax.ShapeDtypeStruct((rows, n_out), F32),
        name="adaln",
    )(c_pad, w_ada, b_ada)


def _inproj_kernel(x_ref, mod_ref, nw_ref, w_ref, wab_ref, proj_ref, ab_ref):
    h = _rms(x_ref[...], nw_ref[...]) * (1.0 + mod_ref[0, 1:2, :]) + mod_ref[0, 0:1, :]
    hb = h.astype(BF16)
    for j in range(PROJ_MAIN // 512):
        cols = slice(j * 512, (j + 1) * 512)
        proj_ref[:, cols] = _dot(hb, w_ref[:, cols]).astype(BF16)
    ab_ref[...] = _dot(hb, wab_ref[...])


def _inproj(x2d, mod3, norm_w, w_main, w_ab, seq):
    n = x2d.shape[0]
    tiles_per_batch = seq // TM_INPROJ
    return pl.pallas_call(
        _inproj_kernel,
        grid=(n // TM_INPROJ,),
        in_specs=[
            pl.BlockSpec((TM_INPROJ, D_MODEL), lambda i: (i, 0)),
            pl.BlockSpec((1, N_MOD, D_MODEL), lambda i: (i // tiles_per_batch, 0, 0)),
            pl.BlockSpec((1, D_MODEL), lambda i: (0, 0)),
            pl.BlockSpec((D_MODEL, PROJ_MAIN), lambda i: (0, 0)),
            pl.BlockSpec((D_MODEL, LANES), lambda i: (0, 0)),
        ],
        out_specs=[
            pl.BlockSpec((TM_INPROJ, PROJ_MAIN), lambda i: (i, 0)),
            pl.BlockSpec((TM_INPROJ, LANES), lambda i: (i, 0)),
        ],
        out_shape=[
            jax.ShapeDtypeStruct((n, PROJ_MAIN), BF16),
            jax.ShapeDtypeStruct((n, LANES), F32),
        ],
        compiler_params=pltpu.CompilerParams(
            dimension_semantics=("parallel",), vmem_limit_bytes=VMEM_LIMIT),
        name="in_proj",
    )(x2d, mod3, norm_w, w_main, w_ab)


def _unit_lower_inverses(a_list):
    c = a_list[0].shape[0]
    row = lax.broadcasted_iota(jnp.int32, (c, c), 0)
    col = lax.broadcasted_iota(jnp.int32, (c, c), 1)
    eye = jnp.where(row == col, 1.0, 0.0).astype(F32)
    ps = [eye - a for a in a_list]
    qs = [a.astype(BF16) for a in a_list]
    qs = [_dot(q, q) for q in qs]
    power = 2
    while 2 * power < GDN_CHUNK:
        qbs = [q.astype(BF16) for q in qs]
        ps = [p + _dot(p.astype(BF16), qb) for p, qb in zip(ps, qbs)]
        qs = [_dot(qb, qb) for qb in qbs]
        power *= 2
    return [p + _dot(p.astype(BF16), q.astype(BF16)) for p, q in zip(ps, qs)]


def _mix_prep_kernel(proj_ref, ab_ref, vnw_ref, wsp_ref, bsp_ref, cw_ref, alog_ref, dtb_ref,
                     ya_ref, qt_ref, kt_ref, w_ref, u_ref, qk_ref, dec_ref, ext_ref,
                     *, tiles_per_batch):
    tm = TM_PREP
    i = pl.program_id(0)

    row = lax.broadcasted_iota(jnp.int32, (GM_CHUNK, GM_CHUNK), 0)
    col = lax.broadcasted_iota(jnp.int32, (GM_CHUNK, GM_CHUNK), 1)
    causal = row >= col
    for g in range(GM_GROUPS):
        ws = jnp.where(causal, wsp_ref[g], 0.0).astype(BF16)
        bcol = bsp_ref[:, g:g + 1]
        cols_u = slice(g * GM_DIM, (g + 1) * GM_DIM)
        cols_v = slice(GM_WIDTH + g * GM_DIM, GM_WIDTH + (g + 1) * GM_DIM)
        for c in range(tm // GM_CHUNK):
            rows = slice(c * GM_CHUNK, (c + 1) * GM_CHUNK)
            u = _gelu(proj_ref[rows, cols_u].astype(F32))
            v = _rms(_gelu(proj_ref[rows, cols_v].astype(F32)), vnw_ref[g:g + 1, :])
            z = _dot(ws, v.astype(BF16)) + bcol
            ya_ref[rows, cols_u] = (u * z).astype(BF16)

    @pl.when(i % tiles_per_batch == 0)
    def _():
        ext_ref[0:SUBLANES, :] = jnp.zeros((SUBLANES, 3 * GDN_WIDTH), F32)

    qkv_cols = slice(2 * GM_WIDTH, 2 * GM_WIDTH + 3 * GDN_WIDTH)
    ext_ref[SUBLANES:SUBLANES + tm, :] = proj_ref[:, qkv_cols].astype(F32)
    conv = jnp.zeros((tm, 3 * GDN_WIDTH), F32)
    for j in range(GDN_CONV):
        start = SUBLANES - (GDN_CONV - 1) + j
        conv = conv + cw_ref[j:j + 1, :] * ext_ref[start:start + tm, :]
    ext_ref[0:SUBLANES, :] = ext_ref[tm:tm + SUBLANES, :]
    act = _silu(conv)

    ab = ab_ref[...]
    sp_in = ab + dtb_ref[...]
    g_all = -jnp.exp(alog_ref[...]) * (
        jnp.maximum(sp_in, 0.0) + jnp.log1p(jnp.exp(-jnp.abs(sp_in))))
    beta_all = jax.nn.sigmoid(ab)
    trow = lax.broadcasted_iota(jnp.int32, (tm, tm), 0)
    tcol = lax.broadcasted_iota(jnp.int32, (tm, tm), 1)
    blk_lower = jnp.where((trow >= tcol) & (trow // GDN_CHUNK == tcol // GDN_CHUNK),
                          1.0, 0.0).astype(BF16)
    g_hi, g_mid, g_lo = _split3(g_all)
    gc_all = _dot(blk_lower, g_hi) + _dot(blk_lower, g_mid) + _dot(blk_lower, g_lo)
    gc_all_t = gc_all.T

    prow = lax.broadcasted_iota(jnp.int32, (GDN_PAIR, GDN_PAIR), 0)
    pcol = lax.broadcasted_iota(jnp.int32, (GDN_PAIR, GDN_PAIR), 1)
    same_chunk = prow // GDN_CHUNK == pcol // GDN_CHUNK
    tri = (prow >= pcol) & same_chunk
    strict = (prow > pcol) & same_chunk
    first_half = lax.broadcasted_iota(jnp.int32, (GDN_PAIR, 1), 0) < GDN_CHUNK

    blocks = [(h, p) for h in range(GDN_HEADS) for p in range(tm // GDN_PAIR)]
    q_l, k_l, kb_l, kbf_l, beta_l, gcc_l, decay_l, v_l = [], [], [], [], [], [], [], []
    for h in range(GDN_HEADS):
        hq = slice(h * GDN_DK, (h + 1) * GDN_DK)
        hk = slice(GDN_WIDTH + h * GDN_DK, GDN_WIDTH + (h + 1) * GDN_DK)
        hv = slice(2 * GDN_WIDTH + h * GDN_DV, 2 * GDN_WIDTH + (h + 1) * GDN_DV)
        q_h = act[:, hq]
        k_h = act[:, hk]
        q_h = q_h * lax.rsqrt(jnp.sum(q_h * q_h, axis=-1, keepdims=True) + EPS) * (GDN_DK ** -0.5)
        k_h = k_h * lax.rsqrt(jnp.sum(k_h * k_h, axis=-1, keepdims=True) + EPS)
        v_h = act[:, hv]
        for p in range(tm // GDN_PAIR):
            rows = slice(p * GDN_PAIR, (p + 1) * GDN_PAIR)
            beta = beta_all[rows, GDN_HEADS + h:GDN_HEADS + h + 1]
            gcc = gc_all[rows, h:h + 1]
            gcr = gc_all_t[h:h + 1, p * GDN_PAIR:(p + 1) * GDN_PAIR]
            k = k_h[rows]
            q_l.append(q_h[rows])
            k_l.append(k)
            kb_l.append(k * beta)
            kbf_l.append(k.astype(BF16))
            beta_l.append(beta)
            gcc_l.append(gcc)
            v_l.append(v_h[rows])
            decay_l.append(jnp.where(tri, jnp.exp(jnp.where(tri, gcc - gcr, 0.0)), 0.0))

    kk_l = [_dot_nt(kb.astype(BF16), kbf) for kb, kbf in zip(kb_l, kbf_l)]
    a_l = [jnp.where(strict, kk * decay, 0.0) for kk, decay in zip(kk_l, decay_l)]
    t_l = _unit_lower_inverses(a_l)
    egc_l = [jnp.exp(gcc) for gcc in gcc_l]
    rhs_l = [jnp.concatenate([v * beta, kb * egc], axis=1).astype(BF16)
             for v, beta, kb, egc in zip(v_l, beta_l, kb_l, egc_l)]
    sol_l = [_dot(t.astype(BF16), rhs) for t, rhs in zip(t_l, rhs_l)]
    qk_l = [jnp.where(tri, _dot_nt(q.astype(BF16), kbf) * decay, 0.0)
            for q, kbf, decay in zip(q_l, kbf_l, decay_l)]

    for idx, (h, p) in enumerate(blocks):
        rows = slice(p * GDN_PAIR, (p + 1) * GDN_PAIR)
        cols = slice(h * GDN_DK, (h + 1) * GDN_DK)
        gcc = gcc_l[idx]
        gl0 = gcc[GDN_CHUNK - 1:GDN_CHUNK]
        gl1 = gcc[GDN_PAIR - 1:GDN_PAIR]
        g_last = jnp.where(first_half, gl0, gl1)
        u_ref[rows, cols] = sol_l[idx][:, :GDN_DV]
        w_ref[rows, cols] = sol_l[idx][:, GDN_DV:].astype(BF16)
        qt_ref[rows, cols] = (q_l[idx] * egc_l[idx]).astype(BF16)
        kt_ref[rows, cols] = (k_l[idx] * jnp.exp(g_last - gcc)).astype(BF16)
        qk_ref[rows, cols] = qk_l[idx].astype(BF16)
        dec_ref[2 * p, h:h + 1, :] = jnp.broadcast_to(jnp.exp(gl0), (1, LANES))
        dec_ref[2 * p + 1, h:h + 1, :] = jnp.broadcast_to(jnp.exp(gl1), (1, LANES))


def _mix_prep(proj, ab, vnorm_w, w_spatial, b_spatial_t, conv_w, alog_pad, dtb_pad, seq):
    n = proj.shape[0]
    tm = TM_PREP
    tiles_per_batch = seq // tm
    const2 = lambda i: (0, 0)
    return pl.pallas_call(
        functools.partial(_mix_prep_kernel, tiles_per_batch=tiles_per_batch),
        grid=(n // tm,),
        in_specs=[
            pl.BlockSpec((tm, PROJ_MAIN), lambda i: (i, 0)),
            pl.BlockSpec((tm, LANES), lambda i: (i, 0)),
            pl.BlockSpec((GM_GROUPS, GM_DIM), const2),
            pl.BlockSpec((GM_GROUPS, GM_CHUNK, GM_CHUNK), lambda i: (0, 0, 0)),
            pl.BlockSpec((GM_CHUNK, GM_GROUPS), const2),
            pl.BlockSpec((GDN_CONV, 3 * GDN_WIDTH), const2),
            pl.BlockSpec((1, LANES), const2),
            pl.BlockSpec((1, LANES), const2),
        ],
        out_specs=[
            pl.BlockSpec((tm, GM_WIDTH), lambda i: (i, 0)),
            pl.BlockSpec((tm, GDN_WIDTH), lambda i: (i, 0)),
            pl.BlockSpec((tm, GDN_WIDTH), lambda i: (i, 0)),
            pl.BlockSpec((tm, GDN_WIDTH), lambda i: (i, 0)),
            pl.BlockSpec((tm, GDN_WIDTH), lambda i: (i, 0)),
            pl.BlockSpec((tm, GDN_WIDTH), lambda i: (i, 0)),
            pl.BlockSpec((tm // GDN_CHUNK, GDN_HEADS, LANES), lambda i: (i, 0, 0)),
        ],
        out_shape=[
            jax.ShapeDtypeStruct((n, GM_WIDTH), BF16),
            jax.ShapeDtypeStruct((n, GDN_WIDTH), BF16),
            jax.ShapeDtypeStruct((n, GDN_WIDTH), BF16),
            jax.ShapeDtypeStruct((n, GDN_WIDTH), BF16),
            jax.ShapeDtypeStruct((n, GDN_WIDTH), F32),
            jax.ShapeDtypeStruct((n, GDN_WIDTH), BF16),
            jax.ShapeDtypeStruct((n // GDN_CHUNK, GDN_HEADS, LANES), F32),
        ],
        scratch_shapes=[pltpu.VMEM((tm + 2 * SUBLANES, 3 * GDN_WIDTH), F32)],
        compiler_params=pltpu.CompilerParams(
            dimension_semantics=("arbitrary",), vmem_limit_bytes=VMEM_LIMIT),
        name="mix_prep",
    )(proj, ab, vnorm_w, w_spatial, b_spatial_t, conv_w, alog_pad, dtb_pad)


def _gdn_scan_kernel(qt_ref, kt_ref, w_ref, u_ref, qk_ref, dec_ref, z_ref, onw_ref,
                     yb_ref, s_ref):
    @pl.when(pl.program_id(1) == 0)
    def _():
        s_ref[...] = jnp.zeros(s_ref.shape, F32)

    heads = range(GDN_HEADS)
    hcols = [slice(h * GDN_DK, (h + 1) * GDN_DK) for h in heads]
    states = [s_ref[h] for h in heads]
    v_prev = [None] * GDN_HEADS
    for c in range(TM_SCAN // GDN_CHUNK):
        rows = slice(c * GDN_CHUNK, (c + 1) * GDN_CHUNK)
        states_b = [s.astype(BF16) for s in states]
        ws = [_dot(w_ref[rows, hcols[h]], states_b[h]) for h in heads]
        qs = [_dot(qt_ref[rows, hcols[h]], states_b[h]) for h in heads]
        v_new = [(u_ref[rows, hcols[h]] - ws[h]).astype(BF16) for h in heads]
        if c % 2 == 0:
            o = [qs[h] + _dot(qk_ref[rows, h * GDN_DK:h * GDN_DK + GDN_CHUNK], v_new[h])
                 for h in heads]
        else:
            o = [qs[h] + _dot(qk_ref[rows, hcols[h]],
                              jnp.concatenate([v_prev[h], v_new[h]], axis=0)) for h in heads]
        states = [states[h] * dec_ref[c, h:h + 1, :] + _dot_tn(kt_ref[rows, hcols[h]], v_new[h])
                  for h in heads]
        v_prev = v_new
        for h in heads:
            zz = z_ref[rows, hcols[h]].astype(F32)
            yb_ref[rows, hcols[h]] = (_rms(o[h], onw_ref[...]) * _silu(zz)).astype(BF16)
    for h in heads:
        s_ref[h] = states[h]


def _gdn_scan(qt, kt, w, u, qk, dec, proj, onorm_w, bsz, seq):
    n = qt.shape[0]
    tm = TM_SCAN
    tpb = seq // tm
    tok = lambda b, j: (b * tpb + j, 0)
    z_block = (2 * GM_WIDTH + 3 * GDN_WIDTH) // GDN_WIDTH
    return pl.pallas_call(
        _gdn_scan_kernel,
        grid=(bsz, tpb),
        in_specs=[
            pl.BlockSpec((tm, GDN_WIDTH), tok),
            pl.BlockSpec((tm, GDN_WIDTH), tok),
            pl.BlockSpec((tm, GDN_WIDTH), tok),
            pl.BlockSpec((tm, GDN_WIDTH), tok),
            pl.BlockSpec((tm, GDN_WIDTH), tok),
            pl.BlockSpec((tm // GDN_CHUNK, GDN_HEADS, LANES), lambda b, j: (b * tpb + j, 0, 0)),
            pl.BlockSpec((tm, GDN_WIDTH), lambda b, j: (b * tpb + j, z_block)),
            pl.BlockSpec((1, GDN_DV), lambda b, j: (0, 0)),
        ],
        out_specs=pl.BlockSpec((tm, GDN_WIDTH), tok),
        out_shape=jax.ShapeDtypeStruct((n, GDN_WIDTH), BF16),
        scratch_shapes=[pltpu.VMEM((GDN_HEADS, GDN_DK, GDN_DV), F32)],
        compiler_params=pltpu.CompilerParams(dimension_semantics=("arbitrary", "arbitrary")),
        name="gdn_scan",
    )(qt, kt, w, u, qk, dec, proj, onorm_w)


def _out_router_kernel(ya_ref, yb_ref, x_ref, mod_ref, wo_ref, n2w_ref, wrh_ref, wrl_ref, br_ref,
                       x1_ref, h2_ref, ridx_ref, rw_ref, cnt_ref, carry_ref):
    tm = TM_ROUTER

    @pl.when(pl.program_id(0) == 0)
    def _():
        carry_ref[...] = jnp.zeros(carry_ref.shape, F32)

    mix = _dot(ya_ref[...], wo_ref[0:GM_WIDTH, :]) + _dot(yb_ref[...], wo_ref[GM_WIDTH:, :])
    x1 = x_ref[...] + mod_ref[0, 2:3, :] * mix
    x1_ref[...] = x1
    h2 = _rms(x1, n2w_ref[...]) * (1.0 + mod_ref[0, 4:5, :]) + mod_ref[0, 3:4, :]
    for j in range(ROW_TILE):
        h2_ref[pl.ds(j, tm, stride=ROW_TILE), :] = h2[:, j * LANES:(j + 1) * LANES]

    h_hi = h2.astype(BF16)
    h_lo = (h2 - h_hi.astype(F32)).astype(BF16)
    logits = (_dot(h_hi, wrh_ref[...]) + _dot(h_hi, wrl_ref[...]) + _dot(h_lo, wrh_ref[...])
              + br_ref[...])
    lane = lax.broadcasted_iota(jnp.int32, (tm, LANES), 1)
    work = jnp.where(lane < N_EXPERTS, logits, -jnp.inf)
    sel_e, sel_v = [], []
    for _ in range(TOP_K):
        m = jnp.max(work, axis=-1, keepdims=True)
        e = jnp.min(jnp.where(work == m, lane, LANES), axis=-1, keepdims=True)
        sel_e.append(e)
        sel_v.append(m)
        work = jnp.where(lane == e, -jnp.inf, work)
    ex = [jnp.exp(v - sel_v[0]) for v in sel_v]
    den = ex[0] + ex[1] + ex[2] + ex[3]

    ridx = jnp.zeros((tm, LANES), jnp.int32)
    rw = jnp.zeros((tm, LANES), F32)
    onehot = jnp.zeros((tm, LANES), F32)
    for k in range(TOP_K):
        onehot = onehot + jnp.where(lane == sel_e[k], 1.0, 0.0)
        ridx = jnp.where(lane == k, sel_e[k], ridx)
        rw = jnp.where(lane == k, ex[k] / den, rw)
    ridx_ref[...] = ridx
    rw_ref[...] = rw
    carry = carry_ref[0:1, :] + jnp.sum(onehot, axis=0, keepdims=True)
    carry_ref[...] = jnp.broadcast_to(carry, carry_ref.shape)
    cnt_ref[...] = jnp.broadcast_to(carry, cnt_ref.shape)


def _out_router(ya, yb, x2d, mod3, w_out, norm2_w, wr_hi, wr_lo, br_pad, seq):
    n = x2d.shape[0]
    tm = TM_ROUTER
    tpb = seq // tm
    tok = lambda i: (i, 0)
    const2 = lambda i: (0, 0)
    return pl.pallas_call(
        _out_router_kernel,
        grid=(n // tm,),
        in_specs=[
            pl.BlockSpec((tm, GM_WIDTH), tok),
            pl.BlockSpec((tm, GDN_WIDTH), tok),
            pl.BlockSpec((tm, D_MODEL), tok),
            pl.BlockSpec((1, N_MOD, D_MODEL), lambda i: (i // tpb, 0, 0)),
            pl.BlockSpec((GM_WIDTH + GDN_WIDTH, D_MODEL), const2),
            pl.BlockSpec((1, D_MODEL), const2),
            pl.BlockSpec((D_MODEL, LANES), const2),
            pl.BlockSpec((D_MODEL, LANES), const2),
            pl.BlockSpec((1, LANES), const2),
        ],
        out_specs=[
            pl.BlockSpec((tm, D_MODEL), tok),
            pl.BlockSpec((tm * ROW_TILE, LANES), tok),
            pl.BlockSpec((tm, LANES), tok),
            pl.BlockSpec((tm, LANES), tok),
            pl.BlockSpec((SUBLANES, LANES), const2),
        ],
        out_shape=[
            jax.ShapeDtypeStruct((n, D_MODEL), F32),
            jax.ShapeDtypeStruct((n * ROW_TILE, LANES), F32),
            jax.ShapeDtypeStruct((n, LANES), jnp.int32),
            jax.ShapeDtypeStruct((n, LANES), F32),
            jax.ShapeDtypeStruct((SUBLANES, LANES), F32),
        ],
        scratch_shapes=[pltpu.VMEM((SUBLANES, LANES), F32)],
        compiler_params=pltpu.CompilerParams(
            dimension_semantics=("arbitrary",), vmem_limit_bytes=VMEM_LIMIT),
        name="out_router",
    )(ya, yb, x2d, mod3, w_out, norm2_w, wr_hi, wr_lo, br_pad)


def _moe_kernel(be_ref, rows_hbm, h2_hbm, wgu_ref, bgu_ref, wd_ref, bd_ref,
                y4_hbm, idx_smem, xbuf, ynew, ysrc, wgu_b, wd_b, gsem, ssem, isem):
    t = MOE_BLOCK
    b = pl.program_id(0)
    n_blocks = pl.num_programs(0)

    ring_row = 2 * t
    rt = ROW_TILE

    def idx_copy(block):
        slot = (block + IDX_RING) % IDX_RING
        first = pl.multiple_of((block + 1) * ring_row, ring_row)
        return pltpu.make_async_copy(rows_hbm.at[pl.ds(first, ring_row)],
                                     idx_smem.at[pl.ds(slot * ring_row, ring_row)], isem.at[slot])

    def idx_base(block):
        return ((block + IDX_RING) % IDX_RING) * ring_row

    def tile_of(row):
        first = row * rt
        return pl.ds(first if isinstance(first, int) else pl.multiple_of(first, rt), rt)

    def gather_row(base, r, dst, sem):
        return pltpu.make_async_copy(h2_hbm.at[tile_of(idx_smem[base + r])], dst.at[tile_of(r)], sem)

    def scatter_row(base, r, src, sem):
        return pltpu.make_async_copy(src.at[tile_of(r)], y4_hbm.at[tile_of(idx_smem[base + t + r])],
                                     sem)

    def wait_gather(dst, sem):
        pltpu.make_async_copy(h2_hbm.at[pl.ds(0, t * rt)], dst, sem).wait()

    def wait_scatter(src, sem):
        pltpu.make_async_copy(src, y4_hbm.at[pl.ds(0, t * rt)], sem).wait()

    def load_rows(buf):
        return jnp.concatenate([buf[pl.ds(j, t, stride=rt), :] for j in range(rt)], axis=1)

    def store_rows(buf, val):
        for j in range(rt):
            buf[pl.ds(j, t, stride=rt), :] = val[:, j * LANES:(j + 1) * LANES]

    @pl.when(b == 0)
    def _():
        idx_copy(-1).start()
        idx_copy(0).start()
        idx_copy(1).start()
        idx_copy(-1).wait()
        idx_copy(0).wait()
        ynew[...] = jnp.zeros(ynew.shape, F32)
        base0 = idx_base(0)

        def body(r, carry):
            gather_row(base0, r, xbuf, gsem).start()
            return carry
        lax.fori_loop(0, t, body, 0)

    @pl.when((b == 0) | (be_ref[b] != be_ref[jnp.maximum(b - 1, 0)]))
    def _():
        wgu_b[...] = wgu_ref[0].astype(BF16)
        wd_b[...] = wd_ref[0].astype(BF16)

    idx_copy(b + 2).start()
    idx_copy(b + 1).wait()
    wait_gather(xbuf, gsem)
    xbf = load_rows(xbuf).astype(BF16)

    gbase = idx_base(b + 1)
    sbase = idx_base(b - 1)

    def issue_gathers(lo, hi):
        for r in range(lo, hi):
            gather_row(gbase, r, xbuf, gsem).start(priority=0)

    def issue_scatters(lo, hi):
        for r in range(lo, hi):
            scatter_row(sbase, r, ysrc, ssem).start(priority=1)

    def ffn_part(f, issue_a, issue_b):
        gcols = slice(f * FF_TILE, (f + 1) * FF_TILE)
        ucols = slice(D_FF + f * FF_TILE, D_FF + (f + 1) * FF_TILE)
        issue_a()
        gate = _dot(xbf, wgu_b[:, gcols]) + bgu_ref[0, :, gcols]
        up = _dot(xbf, wgu_b[:, ucols]) + bgu_ref[0, :, ucols]
        gate = jnp.minimum(gate, SWIGLU_LIMIT)
        up = jnp.clip(up, -SWIGLU_LIMIT, SWIGLU_LIMIT)
        act = gate * jax.nn.sigmoid(SWIGLU_ALPHA * gate) * (up + 1.0)
        issue_b()
        return _dot(act.astype(BF16), wd_b[gcols, :])

    n_f = D_FF // FF_TILE
    acc = ffn_part(0, lambda: issue_gathers(0, t), lambda: None)
    for f in range(1, n_f - 1):
        acc = acc + ffn_part(f, lambda: None, lambda: None)

    @pl.when(b >= 1)
    def _():
        wait_scatter(ysrc, ssem)
    ysrc[...] = ynew[...]

    acc = acc + ffn_part(n_f - 1, lambda: issue_scatters(0, t), lambda: None)
    store_rows(ynew, acc + bd_ref[0])

    @pl.when(b == n_blocks - 1)
    def _():
        wait_scatter(ysrc, ssem)
        base_b = idx_base(b)

        def body(r, carry):
            scatter_row(base_b, r, ynew, ssem).start()
            return carry
        lax.fori_loop(0, t, body, 0)
        wait_scatter(ynew, ssem)
        wait_gather(xbuf, gsem)
        idx_copy(b + 2).wait()


def _moe_ffn(block_e, block_rows, h2, w_gu, b_gu, w_down, b_down, n_blocks, n_rows_out):
    t = MOE_BLOCK

    def w_map(b, be):
        return (be[b], 0, 0)

    grid_spec = pltpu.PrefetchScalarGridSpec(
        num_scalar_prefetch=1,
        grid=(n_blocks,),
        in_specs=[
            pl.BlockSpec(memory_space=pl.ANY),
            pl.BlockSpec(memory_space=pl.ANY),
            pl.BlockSpec((1, D_MODEL, 2 * D_FF), w_map),
            pl.BlockSpec((1, 1, 2 * D_FF), w_map),
            pl.BlockSpec((1, D_FF, D_MODEL), w_map),
            pl.BlockSpec((1, 1, D_MODEL), w_map),
        ],
        out_specs=pl.BlockSpec(memory_space=pl.ANY),
        scratch_shapes=[
            pltpu.SMEM((IDX_RING * 2 * t,), jnp.int32),
            pltpu.VMEM((t * ROW_TILE, LANES), F32),
            pltpu.VMEM((t * ROW_TILE, LANES), F32),
            pltpu.VMEM((t * ROW_TILE, LANES), F32),
            pltpu.VMEM((D_MODEL, 2 * D_FF), BF16),
            pltpu.VMEM((D_FF, D_MODEL), BF16),
            pltpu.SemaphoreType.DMA(()),
            pltpu.SemaphoreType.DMA(()),
            pltpu.SemaphoreType.DMA((IDX_RING,)),
        ],
    )
    return pl.pallas_call(
        _moe_kernel,
        grid_spec=grid_spec,
        out_shape=jax.ShapeDtypeStruct((n_rows_out * ROW_TILE, LANES), F32),
        compiler_params=pltpu.CompilerParams(
            dimension_semantics=("arbitrary",), vmem_limit_bytes=MOE_VMEM_LIMIT),
        name="moe_ffn",
    )(block_e, block_rows, h2, w_gu, b_gu, w_down, b_down)


def _combine_kernel(ya_ref, yb_ref, yc_ref, yd_ref, x1_ref, rw_ref, mod_ref, nfw_ref, out_ref):
    tm = TM_COMBINE
    rw = rw_ref[...]
    cols = []
    for j in range(ROW_TILE):
        rows = pl.ds(j, tm, stride=ROW_TILE)
        col = rw[:, 0:1] * ya_ref[rows, :]
        for k, y_ref in enumerate((yb_ref, yc_ref, yd_ref), start=1):
            col = col + rw[:, k:k + 1] * y_ref[rows, :]
        cols.append(col)
    x2 = x1_ref[...] + mod_ref[0, 5:6, :] * jnp.concatenate(cols, axis=1)
    out_ref[...] = _rms(x2, nfw_ref[...])


def _combine(y4, x1, rw, mod3, norm_f_w, seq):
    n = x1.shape[0]
    tm = TM_COMBINE
    tpb = seq // tm
    tok = lambda i: (i, 0)
    y_specs = [pl.BlockSpec((tm * ROW_TILE, LANES),
                            functools.partial(lambda i, k: (k * (n // tm) + i, 0), k=k))
               for k in range(TOP_K)]
    return pl.pallas_call(
        _combine_kernel,
        grid=(n // tm,),
        in_specs=y_specs + [
            pl.BlockSpec((tm, D_MODEL), tok),
            pl.BlockSpec((tm, LANES), tok),
            pl.BlockSpec((1, N_MOD, D_MODEL), lambda i: (i // tpb, 0, 0)),
            pl.BlockSpec((1, D_MODEL), lambda i: (0, 0)),
        ],
        out_specs=pl.BlockSpec((tm, D_MODEL), tok),
        out_shape=jax.ShapeDtypeStruct((n, D_MODEL), F32),
        compiler_params=pltpu.CompilerParams(
            dimension_semantics=("parallel",), vmem_limit_bytes=VMEM_LIMIT),
        name="combine",
    )(y4, y4, y4, y4, x1, rw, mod3, norm_f_w)


def _pad_lanes(v, fill=0.0):
    out = jnp.full((1, LANES), fill, F32)
    return out.at[0, :v.shape[0]].set(v.astype(F32))


def kernel(x, c, w_ada, b_ada, norm1_w, w_in, gm_vnorm_w, gm_w_spatial, gm_b_spatial, gdn_conv_w,
           gdn_a_log, gdn_dt_bias, gdn_onorm_w, w_out, norm2_w, w_router, b_router, w_gu, b_gu,
           w_down, b_down, norm_f_w):
    bsz, seq, d = x.shape
    n = bsz * seq
    assert w_ada.shape[0] == 1, "the closing RMSNorm is fused into the single layer's combine call"
    l = 0
    x2d = x.reshape(n, d)
    c_pad = jnp.zeros((SUBLANES, d), F32).at[:bsz].set(c)

    mod = _adaln(c_pad, w_ada[l], b_ada[l][None, :])[:bsz]
    mod3 = mod.reshape(bsz, N_MOD, d)

    w_main = w_in[l][:, :PROJ_MAIN].astype(BF16)
    w_ab = jnp.zeros((d, LANES), BF16).at[:, :2 * GDN_HEADS].set(
        w_in[l][:, PROJ_MAIN:].astype(BF16))
    proj, ab = _inproj(x2d, mod3, norm1_w[l][None, :], w_main, w_ab, seq)

    ya, qt, kt, wmat, umat, qk, dec = _mix_prep(
        proj, ab, gm_vnorm_w[l], gm_w_spatial[l], gm_b_spatial[l].T, gdn_conv_w[l],
        _pad_lanes(gdn_a_log[l]), _pad_lanes(gdn_dt_bias[l]), seq)
    yb = _gdn_scan(qt, kt, wmat, umat, qk, dec, proj, gdn_onorm_w[l][None, :], bsz, seq)

    wr_pad = jnp.zeros((d, LANES), F32).at[:, :N_EXPERTS].set(w_router[l])
    wr_hi = wr_pad.astype(BF16)
    wr_lo = (wr_pad - wr_hi.astype(F32)).astype(BF16)
    x1, h2, ridx, rw, cnt = _out_router(
        ya, yb, x2d, mod3, w_out[l].astype(BF16), norm2_w[l][None, :], wr_hi, wr_lo,
        _pad_lanes(b_router[l]), seq)

    t = MOE_BLOCK
    n_assign = n * TOP_K
    n_blocks = n_assign // t + N_EXPERTS
    e_flat = ridx[:, :TOP_K].T.reshape(n_assign)
    keys = e_flat * n_assign + jnp.arange(n_assign, dtype=jnp.int32)
    asg = jnp.sort(keys) % n_assign
    counts = cnt[0, :N_EXPERTS].astype(jnp.int32)
    start = jnp.cumsum(counts) - counts
    nblk = (counts + t - 1) // t
    blk_end = jnp.cumsum(nblk)
    blk_start = blk_end - nblk
    nvb = blk_end[-1]
    bid = jnp.arange(-1, n_blocks + 2, dtype=jnp.int32)
    live = (bid >= 0) & (bid < nvb)
    e_of = jnp.minimum(jnp.sum(blk_end[None, :] <= jnp.clip(bid, 0, nvb - 1)[:, None], axis=1),
                       N_EXPERTS - 1).astype(jnp.int32)
    j_of = bid - blk_start[e_of]
    row_start = jnp.where(live, start[e_of] + j_of * t, 0).astype(jnp.int32)
    n_valid = jnp.where(live, jnp.clip(counts[e_of] - j_of * t, 0, t), 0).astype(jnp.int32)
    block_e = e_of[1:n_blocks + 1]
    lane = jnp.arange(t, dtype=jnp.int32)
    a_blk = asg[jnp.minimum(row_start[:, None] + lane[None, :], n_assign - 1)]
    dst_rows = jnp.where(lane[None, :] < n_valid[:, None], a_blk, n_assign + lane[None, :])
    block_rows = jnp.concatenate([a_blk % n, dst_rows], axis=1).reshape(-1)

    y4 = _moe_ffn(block_e, block_rows, h2, w_gu[l], b_gu[l][:, None, :], w_down[l],
                  b_down[l][:, None, :], n_blocks, n_assign + t)
    out = _combine(y4, x1, rw, mod3, norm_f_w[None, :], seq)
    return out.reshape(bsz, seq, d)
```

```python
import functools

import jax
import jax.numpy as jnp
from jax import lax
from jax.experimental import pallas as pl
from jax.experimental.pallas import tpu as pltpu

F32 = jnp.float32
BF16 = jnp.bfloat16
HIGHEST = lax.Precision.HIGHEST

D_MODEL = 1024
GM_GROUPS = 4
GM_DIM = 128
GM_WIDTH = GM_GROUPS * GM_DIM
GM_CHUNK = 128
GDN_HEADS = 4
GDN_DK = 128
GDN_DV = 128
GDN_WIDTH = GDN_HEADS * GDN_DK
GDN_CONV = 4
GDN_CHUNK = 64
GDN_PAIR = 2 * GDN_CHUNK
N_EXPERTS = 32
TOP_K = 4
D_FF = D_MODEL
SWIGLU_LIMIT = 7.0
SWIGLU_ALPHA = 1.702
N_MOD = 6
EPS = 1e-6

LANES = 128
SUBLANES = 8
PROJ_MAIN = 2 * GM_WIDTH + 4 * GDN_WIDTH

TM_INPROJ = 512
TM_PREP = 256
TM_SCAN = 256
SCAN_BATCH = 4
TM_ROUTER = 512
MOE_BLOCK = 256
TM_COMBINE = 512
FF_TILE = 512
ROW_TILE = D_MODEL // LANES
IDX_RING = 4
VMEM_LIMIT = 48 * 1024 * 1024
MOE_VMEM_LIMIT = 56 * 1024 * 1024


def _dot(a, b):
    return jnp.dot(a, b, preferred_element_type=F32)


def _dot_nt(a, b):
    return lax.dot_general(a, b, (((1,), (1,)), ((), ())), preferred_element_type=F32)


def _dot_tn(a, b):
    return lax.dot_general(a, b, (((0,), (0,)), ((), ())), preferred_element_type=F32)


def _split3(x):
    hi = x.astype(BF16)
    r1 = x - hi.astype(F32)
    mid = r1.astype(BF16)
    lo = (r1 - mid.astype(F32)).astype(BF16)
    return hi, mid, lo


def _rms(x, w):
    return x * lax.rsqrt(jnp.mean(x * x, axis=-1, keepdims=True) + EPS) * w


def _gelu(x):
    return 0.5 * x * (1.0 + lax.erf(x * (2.0 ** -0.5)))


def _silu(x):
    return x * jax.nn.sigmoid(x)


def _adaln_kernel(c_ref, w_ref, b_ref, o_ref):
    c = c_ref[...]
    o_ref[...] = jnp.dot(_silu(c), w_ref[...], precision=HIGHEST,
                         preferred_element_type=F32) + b_ref[...]


def _adaln(c_pad, w_ada, b_ada):
    rows = c_pad.shape[0]
    n_out = w_ada.shape[1]
    return pl.pallas_call(
        _adaln_kernel,
        grid=(n_out // D_MODEL,),
        in_specs=[
            pl.BlockSpec((rows, D_MODEL), lambda j: (0, 0)),
            pl.BlockSpec((D_MODEL, D_MODEL), lambda j: (0, j)),
            pl.BlockSpec((1, D_MODEL), lambda j: (0, j)),
        ],
        out_specs=pl.BlockSpec((rows, D_MODEL), lambda j: (0, j)),
        out_shape=jax.ShapeDtypeStruct((rows, n_out), F32),
        name="adaln",
    )(c_pad, w_ada, b_ada)


def _inproj_kernel(x_ref, mod_ref, nw_ref, w_ref, wab_ref, proj_ref, ab_ref):
    h = _rms(x_ref[...], nw_ref[...]) * (1.0 + mod_ref[0, 1:2, :]) + mod_ref[0, 0:1, :]
    hb = h.astype(BF16)
    for j in range(PROJ_MAIN // 512):
        cols = slice(j * 512, (j + 1) * 512)
        proj_ref[:, cols] = _dot(hb, w_ref[:, cols]).astype(BF16)
    ab_ref[...] = _dot(hb, wab_ref[...])


def _inproj(x2d, mod3, norm_w, w_main, w_ab, seq):
    n = x2d.shape[0]
    tiles_per_batch = seq // TM_INPROJ
    return pl.pallas_call(
        _inproj_kernel,
        grid=(n // TM_INPROJ,),
        in_specs=[
            pl.BlockSpec((TM_INPROJ, D_MODEL), lambda i: (i, 0)),
            pl.BlockSpec((1, N_MOD, D_MODEL), lambda i: (i // tiles_per_batch, 0, 0)),
            pl.BlockSpec((1, D_MODEL), lambda i: (0, 0)),
            pl.BlockSpec((D_MODEL, PROJ_MAIN), lambda i: (0, 0)),
            pl.BlockSpec((D_MODEL, LANES), lambda i: (0, 0)),
        ],
        out_specs=[
            pl.BlockSpec((TM_INPROJ, PROJ_MAIN), lambda i: (i, 0)),
            pl.BlockSpec((TM_INPROJ, LANES), lambda i: (i, 0)),
        ],
        out_shape=[
            jax.ShapeDtypeStruct((n, PROJ_MAIN), BF16),
            jax.ShapeDtypeStruct((n, LANES), F32),
        ],
        compiler_params=pltpu.CompilerParams(
            dimension_semantics=("parallel",), vmem_limit_bytes=VMEM_LIMIT),
        name="in_proj",
    )(x2d, mod3, norm_w, w_main, w_ab)


def _unit_lower_inverses(a_list):
    c = a_list[0].shape[0]
    row = lax.broadcasted_iota(jnp.int32, (c, c), 0)
    col = lax.broadcasted_iota(jnp.int32, (c, c), 1)
    eye = jnp.where(row == col, 1.0, 0.0).astype(F32)
    ps = [eye - a for a in a_list]
    qs = [a.astype(BF16) for a in a_list]
    qs = [_dot(q, q) for q in qs]
    power = 2
    while 2 * power < GDN_CHUNK:
        qbs = [q.astype(BF16) for q in qs]
        ps = [p + _dot(p.astype(BF16), qb) for p, qb in zip(ps, qbs)]
        qs = [_dot(qb, qb) for qb in qbs]
        power *= 2
    return [p + _dot(p.astype(BF16), q.astype(BF16)) for p, q in zip(ps, qs)]


def _mix_prep_kernel(proj_ref, ab_ref, vnw_ref, wsp_ref, bsp_ref, cw_ref, alog_ref, dtb_ref,
                     ya_ref, qt_ref, kt_ref, w_ref, u_ref, qk_ref, dec_ref, ext_ref,
                     *, tiles_per_batch):
    tm = TM_PREP
    i = pl.program_id(0)

    row = lax.broadcasted_iota(jnp.int32, (GM_CHUNK, GM_CHUNK), 0)
    col = lax.broadcasted_iota(jnp.int32, (GM_CHUNK, GM_CHUNK), 1)
    causal = row >= col
    for g in range(GM_GROUPS):
        ws = jnp.where(causal, wsp_ref[g], 0.0).astype(BF16)
        bcol = bsp_ref[:, g:g + 1]
        cols_u = slice(g * GM_DIM, (g + 1) * GM_DIM)
        cols_v = slice(GM_WIDTH + g * GM_DIM, GM_WIDTH + (g + 1) * GM_DIM)
        for c in range(tm // GM_CHUNK):
            rows = slice(c * GM_CHUNK, (c + 1) * GM_CHUNK)
            u = _gelu(proj_ref[rows, cols_u].astype(F32))
            v = _rms(_gelu(proj_ref[rows, cols_v].astype(F32)), vnw_ref[g:g + 1, :])
            z = _dot(ws, v.astype(BF16)) + bcol
            ya_ref[rows, cols_u] = (u * z).astype(BF16)

    @pl.when(i % tiles_per_batch == 0)
    def _():
        ext_ref[0:SUBLANES, :] = jnp.zeros((SUBLANES, 3 * GDN_WIDTH), F32)

    qkv_cols = slice(2 * GM_WIDTH, 2 * GM_WIDTH + 3 * GDN_WIDTH)
    ext_ref[SUBLANES:SUBLANES + tm, :] = proj_ref[:, qkv_cols].astype(F32)
    conv = jnp.zeros((tm, 3 * GDN_WIDTH), F32)
    for j in range(GDN_CONV):
        start = SUBLANES - (GDN_CONV - 1) + j
        conv = conv + cw_ref[j:j + 1, :] * ext_ref[start:start + tm, :]
    ext_ref[0:SUBLANES, :] = ext_ref[tm:tm + SUBLANES, :]
    act = _silu(conv)

    ab = ab_ref[...]
    sp_in = ab + dtb_ref[...]
    g_all = -jnp.exp(alog_ref[...]) * (
        jnp.maximum(sp_in, 0.0) + jnp.log1p(jnp.exp(-jnp.abs(sp_in))))
    beta_all = jax.nn.sigmoid(ab)
    trow = lax.broadcasted_iota(jnp.int32, (tm, tm), 0)
    tcol = lax.broadcasted_iota(jnp.int32, (tm, tm), 1)
    blk_lower = jnp.where((trow >= tcol) & (trow // GDN_CHUNK == tcol // GDN_CHUNK),
                          1.0, 0.0).astype(BF16)
    g_hi, g_mid, g_lo = _split3(g_all)
    gc_all = _dot(blk_lower, g_hi) + _dot(blk_lower, g_mid) + _dot(blk_lower, g_lo)
    gc_all_t = gc_all.T

    prow = lax.broadcasted_iota(jnp.int32, (GDN_PAIR, GDN_PAIR), 0)
    pcol = lax.broadcasted_iota(jnp.int32, (GDN_PAIR, GDN_PAIR), 1)
    same_chunk = prow // GDN_CHUNK == pcol // GDN_CHUNK
    tri = (prow >= pcol) & same_chunk
    strict = (prow > pcol) & same_chunk
    first_half = lax.broadcasted_iota(jnp.int32, (GDN_PAIR, 1), 0) < GDN_CHUNK

    blocks = [(h, p) for h in range(GDN_HEADS) for p in range(tm // GDN_PAIR)]
    q_l, k_l, kb_l, kbf_l, beta_l, gcc_l, decay_l, v_l = [], [], [], [], [], [], [], []
    for h in range(GDN_HEADS):
        hq = slice(h * GDN_DK, (h + 1) * GDN_DK)
        hk = slice(GDN_WIDTH + h * GDN_DK, GDN_WIDTH + (h + 1) * GDN_DK)
        hv = slice(2 * GDN_WIDTH + h * GDN_DV, 2 * GDN_WIDTH + (h + 1) * GDN_DV)
        q_h = act[:, hq]
        k_h = act[:, hk]
        q_h = q_h * lax.rsqrt(jnp.sum(q_h * q_h, axis=-1, keepdims=True) + EPS) * (GDN_DK ** -0.5)
        k_h = k_h * lax.rsqrt(jnp.sum(k_h * k_h, axis=-1, keepdims=True) + EPS)
        v_h = act[:, hv]
        for p in range(tm // GDN_PAIR):
            rows = slice(p * GDN_PAIR, (p + 1) * GDN_PAIR)
            beta = beta_all[rows, GDN_HEADS + h:GDN_HEADS + h + 1]
            gcc = gc_all[rows, h:h + 1]
            gcr = gc_all_t[h:h + 1, p * GDN_PAIR:(p + 1) * GDN_PAIR]
            k = k_h[rows]
            q_l.append(q_h[rows])
            k_l.append(k)
            kb_l.append(k * beta)
            kbf_l.append(k.astype(BF16))
            beta_l.append(beta)
            gcc_l.append(gcc)
            v_l.append(v_h[rows])
            decay_l.append(jnp.where(tri, jnp.exp(jnp.where(tri, gcc - gcr, 0.0)), 0.0))

    kk_l = [_dot_nt(kb.astype(BF16), kbf) for kb, kbf in zip(kb_l, kbf_l)]
    a_l = [jnp.where(strict, kk * decay, 0.0) for kk, decay in zip(kk_l, decay_l)]
    t_l = _unit_lower_inverses(a_l)
    egc_l = [jnp.exp(gcc) for gcc in gcc_l]
    rhs_l = [jnp.concatenate([v * beta, kb * egc], axis=1).astype(BF16)
             for v, beta, kb, egc in zip(v_l, beta_l, kb_l, egc_l)]
    sol_l = [_dot(t.astype(BF16), rhs) for t, rhs in zip(t_l, rhs_l)]
    qk_l = [jnp.where(tri, _dot_nt(q.astype(BF16), kbf) * decay, 0.0)
            for q, kbf, decay in zip(q_l, kbf_l, decay_l)]

    for idx, (h, p) in enumerate(blocks):
        rows = slice(p * GDN_PAIR, (p + 1) * GDN_PAIR)
        cols = slice(h * GDN_DK, (h + 1) * GDN_DK)
        gcc = gcc_l[idx]
        gl0 = gcc[GDN_CHUNK - 1:GDN_CHUNK]
        gl1 = gcc[GDN_PAIR - 1:GDN_PAIR]
        g_last = jnp.where(first_half, gl0, gl1)
        u_ref[rows, cols] = sol_l[idx][:, :GDN_DV]
        w_ref[rows, cols] = sol_l[idx][:, GDN_DV:].astype(BF16)
        qt_ref[rows, cols] = (q_l[idx] * egc_l[idx]).astype(BF16)
        kt_ref[rows, cols] = (k_l[idx] * jnp.exp(g_last - gcc)).astype(BF16)
        qk_ref[rows, cols] = qk_l[idx].astype(BF16)
        dec_ref[2 * p, h:h + 1, :] = jnp.broadcast_to(jnp.exp(gl0), (1, LANES))
        dec_ref[2 * p + 1, h:h + 1, :] = jnp.broadcast_to(jnp.exp(gl1), (1, LANES))


def _mix_prep(proj, ab, vnorm_w, w_spatial, b_spatial_t, conv_w, alog_pad, dtb_pad, seq):
    n = proj.shape[0]
    tm = TM_PREP
    tiles_per_batch = seq // tm
    const2 = lambda i: (0, 0)
    return pl.pallas_call(
        functools.partial(_mix_prep_kernel, tiles_per_batch=tiles_per_batch),
        grid=(n // tm,),
        in_specs=[
            pl.BlockSpec((tm, PROJ_MAIN), lambda i: (i, 0)),
            pl.BlockSpec((tm, LANES), lambda i: (i, 0)),
            pl.BlockSpec((GM_GROUPS, GM_DIM), const2),
            pl.BlockSpec((GM_GROUPS, GM_CHUNK, GM_CHUNK), lambda i: (0, 0, 0)),
            pl.BlockSpec((GM_CHUNK, GM_GROUPS), const2),
            pl.BlockSpec((GDN_CONV, 3 * GDN_WIDTH), const2),
            pl.BlockSpec((1, LANES), const2),
            pl.BlockSpec((1, LANES), const2),
        ],
        out_specs=[
            pl.BlockSpec((tm, GM_WIDTH), lambda i: (i, 0)),
            pl.BlockSpec((tm, GDN_WIDTH), lambda i: (i, 0)),
            pl.BlockSpec((tm, GDN_WIDTH), lambda i: (i, 0)),
            pl.BlockSpec((tm, GDN_WIDTH), lambda i: (i, 0)),
            pl.BlockSpec((tm, GDN_WIDTH), lambda i: (i, 0)),
            pl.BlockSpec((tm, GDN_WIDTH), lambda i: (i, 0)),
            pl.BlockSpec((tm // GDN_CHUNK, GDN_HEADS, LANES), lambda i: (i, 0, 0)),
        ],
        out_shape=[
            jax.ShapeDtypeStruct((n, GM_WIDTH), BF16),
            jax.ShapeDtypeStruct((n, GDN_WIDTH), BF16),
            jax.ShapeDtypeStruct((n, GDN_WIDTH), BF16),
            jax.ShapeDtypeStruct((n, GDN_WIDTH), BF16),
            jax.ShapeDtypeStruct((n, GDN_WIDTH), F32),
            jax.ShapeDtypeStruct((n, GDN_WIDTH), BF16),
            jax.ShapeDtypeStruct((n // GDN_CHUNK, GDN_HEADS, LANES), F32),
        ],
        scratch_shapes=[pltpu.VMEM((tm + 2 * SUBLANES, 3 * GDN_WIDTH), F32)],
        compiler_params=pltpu.CompilerParams(
            dimension_semantics=("arbitrary",), vmem_limit_bytes=VMEM_LIMIT),
        name="mix_prep",
    )(proj, ab, vnorm_w, w_spatial, b_spatial_t, conv_w, alog_pad, dtb_pad)


def _gdn_scan_kernel(qt_ref, kt_ref, w_ref, u_ref, qk_ref, dec_ref, z_ref, onw_ref,
                     yb_ref, s_ref):
    @pl.when(pl.program_id(1) == 0)
    def _():
        s_ref[...] = jnp.zeros(s_ref.shape, F32)

    chains = [(i, h) for i in range(SCAN_BATCH) for h in range(GDN_HEADS)]
    hcols = [slice(h * GDN_DK, (h + 1) * GDN_DK) for h in range(GDN_HEADS)]
    states = [s_ref[i, h] for i, h in chains]
    v_prev = [None] * len(chains)
    for c in range(TM_SCAN // GDN_CHUNK):
        rows = slice(c * GDN_CHUNK, (c + 1) * GDN_CHUNK)
        states_b = [s.astype(BF16) for s in states]
        ws = [_dot(w_ref[i, rows, hcols[h]], sb) for (i, h), sb in zip(chains, states_b)]
        qs = [_dot(qt_ref[i, rows, hcols[h]], sb) for (i, h), sb in zip(chains, states_b)]
        v_new = [(u_ref[i, rows, hcols[h]] - wsn).astype(BF16) for (i, h), wsn in zip(chains, ws)]
        if c % 2 == 0:
            o = [q + _dot(qk_ref[i, rows, h * GDN_DK:h * GDN_DK + GDN_CHUNK], vn)
                 for (i, h), q, vn in zip(chains, qs, v_new)]
        else:
            o = [q + _dot(qk_ref[i, rows, hcols[h]], jnp.concatenate([vp, vn], axis=0))
                 for (i, h), q, vp, vn in zip(chains, qs, v_prev, v_new)]
        states = [s * dec_ref[i, c, h:h + 1, :] + _dot_tn(kt_ref[i, rows, hcols[h]], vn)
                  for (i, h), s, vn in zip(chains, states, v_new)]
        v_prev = v_new
        for (i, h), on in zip(chains, o):
            zz = z_ref[i, rows, hcols[h]].astype(F32)
            yb_ref[i, rows, hcols[h]] = (_rms(on, onw_ref[...]) * _silu(zz)).astype(BF16)
    for (i, h), s in zip(chains, states):
        s_ref[i, h] = s


def _gdn_scan(qt, kt, w, u, qk, dec, proj, onorm_w, bsz, seq):
    n = qt.shape[0]
    tm = TM_SCAN
    nb = SCAN_BATCH
    assert bsz % nb == 0
    as3d = lambda a: a.reshape(bsz, seq, a.shape[-1])
    tok = lambda b, j: (b, j, 0)
    z_block = (2 * GM_WIDTH + 3 * GDN_WIDTH) // GDN_WIDTH
    wide = pl.BlockSpec((nb, tm, GDN_WIDTH), tok)
    out = pl.pallas_call(
        _gdn_scan_kernel,
        grid=(bsz // nb, seq // tm),
        in_specs=[
            wide, wide, wide, wide, wide,
            pl.BlockSpec((nb, tm // GDN_CHUNK, GDN_HEADS, LANES), lambda b, j: (b, j, 0, 0)),
            pl.BlockSpec((nb, tm, GDN_WIDTH), lambda b, j: (b, j, z_block)),
            pl.BlockSpec((1, GDN_DV), lambda b, j: (0, 0)),
        ],
        out_specs=wide,
        out_shape=jax.ShapeDtypeStruct((bsz, seq, GDN_WIDTH), BF16),
        scratch_shapes=[pltpu.VMEM((nb, GDN_HEADS, GDN_DK, GDN_DV), F32)],
        compiler_params=pltpu.CompilerParams(dimension_semantics=("arbitrary", "arbitrary")),
        name="gdn_scan",
    )(as3d(qt), as3d(kt), as3d(w), as3d(u), as3d(qk),
      dec.reshape(bsz, seq // GDN_CHUNK, GDN_HEADS, LANES), as3d(proj), onorm_w)
    return out.reshape(n, GDN_WIDTH)


def _out_router_kernel(ya_ref, yb_ref, x_ref, mod_ref, wo_ref, n2w_ref, wrh_ref, wrl_ref, br_ref,
                       x1_ref, h2_ref, ridx_ref, rw_ref, cnt_ref, carry_ref):
    tm = TM_ROUTER

    @pl.when(pl.program_id(0) == 0)
    def _():
        carry_ref[...] = jnp.zeros(carry_ref.shape, F32)

    mix = _dot(ya_ref[...], wo_ref[0:GM_WIDTH, :]) + _dot(yb_ref[...], wo_ref[GM_WIDTH:, :])
    x1 = x_ref[...] + mod_ref[0, 2:3, :] * mix
    x1_ref[...] = x1
    h2 = _rms(x1, n2w_ref[...]) * (1.0 + mod_ref[0, 4:5, :]) + mod_ref[0, 3:4, :]
    for j in range(ROW_TILE):
        h2_ref[pl.ds(j, tm, stride=ROW_TILE), :] = h2[:, j * LANES:(j + 1) * LANES]

    h_hi = h2.astype(BF16)
    h_lo = (h2 - h_hi.astype(F32)).astype(BF16)
    logits = (_dot(h_hi, wrh_ref[...]) + _dot(h_hi, wrl_ref[...]) + _dot(h_lo, wrh_ref[...])
              + br_ref[...])
    lane = lax.broadcasted_iota(jnp.int32, (tm, LANES), 1)
    work = jnp.where(lane < N_EXPERTS, logits, -jnp.inf)
    sel_e, sel_v = [], []
    for _ in range(TOP_K):
        m = jnp.max(work, axis=-1, keepdims=True)
        e = jnp.min(jnp.where(work == m, lane, LANES), axis=-1, keepdims=True)
        sel_e.append(e)
        sel_v.append(m)
        work = jnp.where(lane == e, -jnp.inf, work)
    ex = [jnp.exp(v - sel_v[0]) for v in sel_v]
    den = ex[0] + ex[1] + ex[2] + ex[3]

    ridx = jnp.zeros((tm, LANES), jnp.int32)
    rw = jnp.zeros((tm, LANES), F32)
    onehot = jnp.zeros((tm, LANES), F32)
    for k in range(TOP_K):
        onehot = onehot + jnp.where(lane == sel_e[k], 1.0, 0.0)
        ridx = jnp.where(lane == k, sel_e[k], ridx)
        rw = jnp.where(lane == k, ex[k] / den, rw)
    ridx_ref[...] = ridx
    rw_ref[...] = rw
    carry = carry_ref[0:1, :] + jnp.sum(onehot, axis=0, keepdims=True)
    carry_ref[...] = jnp.broadcast_to(carry, carry_ref.shape)
    cnt_ref[...] = jnp.broadcast_to(carry, cnt_ref.shape)


def _out_router(ya, yb, x2d, mod3, w_out, norm2_w, wr_hi, wr_lo, br_pad, seq):
    n = x2d.shape[0]
    tm = TM_ROUTER
    tpb = seq // tm
    tok = lambda i: (i, 0)
    const2 = lambda i: (0, 0)
    return pl.pallas_call(
        _out_router_kernel,
        grid=(n // tm,),
        in_specs=[
            pl.BlockSpec((tm, GM_WIDTH), tok),
            pl.BlockSpec((tm, GDN_WIDTH), tok),
            pl.BlockSpec((tm, D_MODEL), tok),
            pl.BlockSpec((1, N_MOD, D_MODEL), lambda i: (i // tpb, 0, 0)),
            pl.BlockSpec((GM_WIDTH + GDN_WIDTH, D_MODEL), const2),
            pl.BlockSpec((1, D_MODEL), const2),
            pl.BlockSpec((D_MODEL, LANES), const2),
            pl.BlockSpec((D_MODEL, LANES), const2),
            pl.BlockSpec((1, LANES), const2),
        ],
        out_specs=[
            pl.BlockSpec((tm, D_MODEL), tok),
            pl.BlockSpec((tm * ROW_TILE, LANES), tok),
            pl.BlockSpec((tm, LANES), tok),
            pl.BlockSpec((tm, LANES), tok),
            pl.BlockSpec((SUBLANES, LANES), const2),
        ],
        out_shape=[
            jax.ShapeDtypeStruct((n, D_MODEL), F32),
            jax.ShapeDtypeStruct((n * ROW_TILE, LANES), F32),
            jax.ShapeDtypeStruct((n, LANES), jnp.int32),
            jax.ShapeDtypeStruct((n, LANES), F32),
            jax.ShapeDtypeStruct((SUBLANES, LANES), F32),
        ],
        scratch_shapes=[pltpu.VMEM((SUBLANES, LANES), F32)],
        compiler_params=pltpu.CompilerParams(
            dimension_semantics=("arbitrary",), vmem_limit_bytes=VMEM_LIMIT),
        name="out_router",
    )(ya, yb, x2d, mod3, w_out, norm2_w, wr_hi, wr_lo, br_pad)


def _moe_kernel(be_ref, rows_hbm, h2_hbm, wgu_ref, bgu_ref, wd_ref, bd_ref,
                y4_hbm, idx_smem, xbuf, ynew, ysrc, wgu_b, wd_b, gsem, ssem, isem):
    t = MOE_BLOCK
    b = pl.program_id(0)
    n_blocks = pl.num_programs(0)

    ring_row = 2 * t
    rt = ROW_TILE

    def idx_copy(block):
        slot = (block + IDX_RING) % IDX_RING
        first = pl.multiple_of((block + 1) * ring_row, ring_row)
        return pltpu.make_async_copy(rows_hbm.at[pl.ds(first, ring_row)],
                                     idx_smem.at[pl.ds(slot * ring_row, ring_row)], isem.at[slot])

    def idx_base(block):
        return ((block + IDX_RING) % IDX_RING) * ring_row

    def tile_of(row):
        first = row * rt
        return pl.ds(first if isinstance(first, int) else pl.multiple_of(first, rt), rt)

    def gather_row(base, r, dst, sem):
        return pltpu.make_async_copy(h2_hbm.at[tile_of(idx_smem[base + r])], dst.at[tile_of(r)], sem)

    def scatter_row(base, r, src, sem):
        return pltpu.make_async_copy(src.at[tile_of(r)], y4_hbm.at[tile_of(idx_smem[base + t + r])],
                                     sem)

    def wait_gather(dst, sem):
        pltpu.make_async_copy(h2_hbm.at[pl.ds(0, t * rt)], dst, sem).wait()

    def wait_scatter(src, sem):
        pltpu.make_async_copy(src, y4_hbm.at[pl.ds(0, t * rt)], sem).wait()

    def load_rows(buf):
        return jnp.concatenate([buf[pl.ds(j, t, stride=rt), :] for j in range(rt)], axis=1)

    def store_rows(buf, val):
        for j in range(rt):
            buf[pl.ds(j, t, stride=rt), :] = val[:, j * LANES:(j + 1) * LANES]

    @pl.when(b == 0)
    def _():
        idx_copy(-1).start()
        idx_copy(0).start()
        idx_copy(1).start()
        idx_copy(-1).wait()
        idx_copy(0).wait()
        ynew[...] = jnp.zeros(ynew.shape, F32)
        base0 = idx_base(0)

        def body(r, carry):
            gather_row(base0, r, xbuf, gsem).start()
            return carry
        lax.fori_loop(0, t, body, 0)

    @pl.when((b == 0) | (be_ref[b] != be_ref[jnp.maximum(b - 1, 0)]))
    def _():
        wgu_b[...] = wgu_ref[0].astype(BF16)
        wd_b[...] = wd_ref[0].astype(BF16)

    idx_copy(b + 2).start()
    idx_copy(b + 1).wait()
    wait_gather(xbuf, gsem)
    xbf = load_rows(xbuf).astype(BF16)

    gbase = idx_base(b + 1)
    sbase = idx_base(b - 1)

    def issue_gathers(lo, hi):
        for r in range(lo, hi):
            gather_row(gbase, r, xbuf, gsem).start(priority=0)

    def issue_scatters(lo, hi):
        for r in range(lo, hi):
            scatter_row(sbase, r, ysrc, ssem).start(priority=1)

    def ffn_part(f, issue_a, issue_b):
        gcols = slice(f * FF_TILE, (f + 1) * FF_TILE)
        ucols = slice(D_FF + f * FF_TILE, D_FF + (f + 1) * FF_TILE)
        issue_a()
        gate = _dot(xbf, wgu_b[:, gcols]) + bgu_ref[0, :, gcols]
        up = _dot(xbf, wgu_b[:, ucols]) + bgu_ref[0, :, ucols]
        gate = jnp.minimum(gate, SWIGLU_LIMIT)
        up = jnp.clip(up, -SWIGLU_LIMIT, SWIGLU_LIMIT)
        act = gate * jax.nn.sigmoid(SWIGLU_ALPHA * gate) * (up + 1.0)
        issue_b()
        return _dot(act.astype(BF16), wd_b[gcols, :])

    n_f = D_FF // FF_TILE
    acc = ffn_part(0, lambda: issue_gathers(0, t), lambda: None)
    for f in range(1, n_f - 1):
        acc = acc + ffn_part(f, lambda: None, lambda: None)

    @pl.when(b >= 1)
    def _():
        wait_scatter(ysrc, ssem)
    ysrc[...] = ynew[...]

    acc = acc + ffn_part(n_f - 1, lambda: issue_scatters(0, t), lambda: None)
    store_rows(ynew, acc + bd_ref[0])

    @pl.when(b == n_blocks - 1)
    def _():
        wait_scatter(ysrc, ssem)
        base_b = idx_base(b)

        def body(r, carry):
            scatter_row(base_b, r, ynew, ssem).start()
            return carry
        lax.fori_loop(0, t, body, 0)
        wait_scatter(ynew, ssem)
        wait_gather(xbuf, gsem)
        idx_copy(b + 2).wait()


def _moe_ffn(block_e, block_rows, h2, w_gu, b_gu, w_down, b_down, n_blocks, n_rows_out):
    t = MOE_BLOCK

    def w_map(b, be):
        return (be[b], 0, 0)

    grid_spec = pltpu.PrefetchScalarGridSpec(
        num_scalar_prefetch=1,
        grid=(n_blocks,),
        in_specs=[
            pl.BlockSpec(memory_space=pl.ANY),
            pl.BlockSpec(memory_space=pl.ANY),
            pl.BlockSpec((1, D_MODEL, 2 * D_FF), w_map),
            pl.BlockSpec((1, 1, 2 * D_FF), w_map),
            pl.BlockSpec((1, D_FF, D_MODEL), w_map),
            pl.BlockSpec((1, 1, D_MODEL), w_map),
        ],
        out_specs=pl.BlockSpec(memory_space=pl.ANY),
        scratch_shapes=[
            pltpu.SMEM((IDX_RING * 2 * t,), jnp.int32),
            pltpu.VMEM((t * ROW_TILE, LANES), F32),
            pltpu.VMEM((t * ROW_TILE, LANES), F32),
            pltpu.VMEM((t * ROW_TILE, LANES), F32),
            pltpu.VMEM((D_MODEL, 2 * D_FF), BF16),
            pltpu.VMEM((D_FF, D_MODEL), BF16),
            pltpu.SemaphoreType.DMA(()),
            pltpu.SemaphoreType.DMA(()),
            pltpu.SemaphoreType.DMA((IDX_RING,)),
        ],
    )
    return pl.pallas_call(
        _moe_kernel,
        grid_spec=grid_spec,
        out_shape=jax.ShapeDtypeStruct((n_rows_out * ROW_TILE, LANES), F32),
        compiler_params=pltpu.CompilerParams(
            dimension_semantics=("arbitrary",), vmem_limit_bytes=MOE_VMEM_LIMIT),
        name="moe_ffn",
    )(block_e, block_rows, h2, w_gu, b_gu, w_down, b_down)


def _combine_kernel(ya_ref, yb_ref, yc_ref, yd_ref, x1_ref, rw_ref, mod_ref, nfw_ref, out_ref):
    tm = TM_COMBINE
    rw = rw_ref[...]
    cols = []
    for j in range(ROW_TILE):
        rows = pl.ds(j, tm, stride=ROW_TILE)
        col = rw[:, 0:1] * ya_ref[rows, :]
        for k, y_ref in enumerate((yb_ref, yc_ref, yd_ref), start=1):
            col = col + rw[:, k:k + 1] * y_ref[rows, :]
        cols.append(col)
    x2 = x1_ref[...] + mod_ref[0, 5:6, :] * jnp.concatenate(cols, axis=1)
    out_ref[...] = _rms(x2, nfw_ref[...])


def _combine(y4, x1, rw, mod3, norm_f_w, seq):
    n = x1.shape[0]
    tm = TM_COMBINE
    tpb = seq // tm
    tok = lambda i: (i, 0)
    y_specs = [pl.BlockSpec((tm * ROW_TILE, LANES),
                            functools.partial(lambda i, k: (k * (n // tm) + i, 0), k=k))
               for k in range(TOP_K)]
    return pl.pallas_call(
        _combine_kernel,
        grid=(n // tm,),
        in_specs=y_specs + [
            pl.BlockSpec((tm, D_MODEL), tok),
            pl.BlockSpec((tm, LANES), tok),
            pl.BlockSpec((1, N_MOD, D_MODEL), lambda i: (i // tpb, 0, 0)),
            pl.BlockSpec((1, D_MODEL), lambda i: (0, 0)),
        ],
        out_specs=pl.BlockSpec((tm, D_MODEL), tok),
        out_shape=jax.ShapeDtypeStruct((n, D_MODEL), F32),
        compiler_params=pltpu.CompilerParams(
            dimension_semantics=("parallel",), vmem_limit_bytes=VMEM_LIMIT),
        name="combine",
    )(y4, y4, y4, y4, x1, rw, mod3, norm_f_w)


def _pad_lanes(v, fill=0.0):
    out = jnp.full((1, LANES), fill, F32)
    return out.at[0, :v.shape[0]].set(v.astype(F32))


def kernel(x, c, w_ada, b_ada, norm1_w, w_in, gm_vnorm_w, gm_w_spatial, gm_b_spatial, gdn_conv_w,
           gdn_a_log, gdn_dt_bias, gdn_onorm_w, w_out, norm2_w, w_router, b_router, w_gu, b_gu,
           w_down, b_down, norm_f_w):
    bsz, seq, d = x.shape
    n = bsz * seq
    assert w_ada.shape[0] == 1, "the closing RMSNorm is fused into the single layer's combine call"
    l = 0
    x2d = x.reshape(n, d)
    c_pad = jnp.zeros((SUBLANES, d), F32).at[:bsz].set(c)

    mod = _adaln(c_pad, w_ada[l], b_ada[l][None, :])[:bsz]
    mod3 = mod.reshape(bsz, N_MOD, d)

    w_main = w_in[l][:, :PROJ_MAIN].astype(BF16)
    w_ab = jnp.zeros((d, LANES), BF16).at[:, :2 * GDN_HEADS].set(
        w_in[l][:, PROJ_MAIN:].astype(BF16))
    proj, ab = _inproj(x2d, mod3, norm1_w[l][None, :], w_main, w_ab, seq)

    ya, qt, kt, wmat, umat, qk, dec = _mix_prep(
        proj, ab, gm_vnorm_w[l], gm_w_spatial[l], gm_b_spatial[l].T, gdn_conv_w[l],
        _pad_lanes(gdn_a_log[l]), _pad_lanes(gdn_dt_bias[l]), seq)
    yb = _gdn_scan(qt, kt, wmat, umat, qk, dec, proj, gdn_onorm_w[l][None, :], bsz, seq)

    wr_pad = jnp.zeros((d, LANES), F32).at[:, :N_EXPERTS].set(w_router[l])
    wr_hi = wr_pad.astype(BF16)
    wr_lo = (wr_pad - wr_hi.astype(F32)).astype(BF16)
    x1, h2, ridx, rw, cnt = _out_router(
        ya, yb, x2d, mod3, w_out[l].astype(BF16), norm2_w[l][None, :], wr_hi, wr_lo,
        _pad_lanes(b_router[l]), seq)

    t = MOE_BLOCK
    n_assign = n * TOP_K
    n_blocks = n_assign // t + N_EXPERTS
    e_flat = ridx[:, :TOP_K].T.reshape(n_assign)
    keys = e_flat * n_assign + jnp.arange(n_assign, dtype=jnp.int32)
    asg = jnp.sort(keys) % n_assign
    counts = cnt[0, :N_EXPERTS].astype(jnp.int32)
    start = jnp.cumsum(counts) - counts
    nblk = (counts + t - 1) // t
    blk_end = jnp.cumsum(nblk)
    blk_start = blk_end - nblk
    nvb = blk_end[-1]
    bid = jnp.arange(-1, n_blocks + 2, dtype=jnp.int32)
    live = (bid >= 0) & (bid < nvb)
    e_of = jnp.minimum(jnp.sum(blk_end[None, :] <= jnp.clip(bid, 0, nvb - 1)[:, None], axis=1),
                       N_EXPERTS - 1).astype(jnp.int32)
    j_of = bid - blk_start[e_of]
    row_start = jnp.where(live, start[e_of] + j_of * t, 0).astype(jnp.int32)
    n_valid = jnp.where(live, jnp.clip(counts[e_of] - j_of * t, 0, t), 0).astype(jnp.int32)
    block_e = e_of[1:n_blocks + 1]
    lane = jnp.arange(t, dtype=jnp.int32)
    a_blk = asg[jnp.minimum(row_start[:, None] + lane[None, :], n_assign - 1)]
    dst_rows = jnp.where(lane[None, :] < n_valid[:, None], a_blk, n_assign + lane[None, :])
    block_rows = jnp.concatenate([a_blk % n, dst_rows], axis=1).reshape(-1)

    y4 = _moe_ffn(block_e, block_rows, h2, w_gu[l], b_gu[l][:, None, :], w_down[l],
                  b_down[l][:, None, :], n_blocks, n_assign + t)
    out = _combine(y4, x1, rw, mod3, norm_f_w[None, :], seq)
    return out.reshape(bsz, seq, d)
```

```python
import functools

import jax
import jax.numpy as jnp
from jax import lax
from jax.experimental import pallas as pl
from jax.experimental.pallas import tpu as pltpu

F32 = jnp.float32
BF16 = jnp.bfloat16
HIGHEST = lax.Precision.HIGHEST

D_MODEL = 1024
GM_GROUPS = 4
GM_DIM = 128
GM_WIDTH = GM_GROUPS * GM_DIM
GM_CHUNK = 128
GDN_HEADS = 4
GDN_DK = 128
GDN_DV = 128
GDN_WIDTH = GDN_HEADS * GDN_DK
GDN_CONV = 4
GDN_CHUNK = 64
GDN_PAIR = 2 * GDN_CHUNK
N_EXPERTS = 32
TOP_K = 4
D_FF = D_MODEL
SWIGLU_LIMIT = 7.0
SWIGLU_ALPHA = 1.702
N_MOD = 6
EPS = 1e-6

LANES = 128
SUBLANES = 8
PROJ_MAIN = 2 * GM_WIDTH + 4 * GDN_WIDTH

TM_INPROJ = 512
TM_PREP = 256
TM_SCAN = 256
SCAN_BATCH = 4
TM_ROUTER = 512
MOE_BLOCK = 256
TM_COMBINE = 512
FF_TILE = 512
ROW_TILE = D_MODEL // LANES
IDX_RING = 4
VMEM_LIMIT = 48 * 1024 * 1024
MOE_VMEM_LIMIT = 56 * 1024 * 1024


def _dot(a, b):
    return jnp.dot(a, b, preferred_element_type=F32)


def _dot_nt(a, b):
    return lax.dot_general(a, b, (((1,), (1,)), ((), ())), preferred_element_type=F32)


def _dot_tn(a, b):
    return lax.dot_general(a, b, (((0,), (0,)), ((), ())), preferred_element_type=F32)


def _split3(x):
    hi = x.astype(BF16)
    r1 = x - hi.astype(F32)
    mid = r1.astype(BF16)
    lo = (r1 - mid.astype(F32)).astype(BF16)
    return hi, mid, lo


def _rms(x, w):
    return x * lax.rsqrt(jnp.mean(x * x, axis=-1, keepdims=True) + EPS) * w


def _gelu(x):
    return 0.5 * x * (1.0 + lax.erf(x * (2.0 ** -0.5)))


def _silu(x):
    return x * jax.nn.sigmoid(x)


def _adaln_kernel(c_ref, w_ref, b_ref, o_ref):
    c = c_ref[...]
    o_ref[...] = jnp.dot(_silu(c), w_ref[...], precision=HIGHEST,
                         preferred_element_type=F32) + b_ref[...]


def _adaln(c_pad, w_ada, b_ada):
    rows = c_pad.shape[0]
    n_out = w_ada.shape[1]
    return pl.pallas_call(
        _adaln_kernel,
        grid=(n_out // D_MODEL,),
        in_specs=[
            pl.BlockSpec((rows, D_MODEL), lambda j: (0, 0)),
            pl.BlockSpec((D_MODEL, D_MODEL), lambda j: (0, j)),
            pl.BlockSpec((1, D_MODEL), lambda j: (0, j)),
        ],
        out_specs=pl.BlockSpec((rows, D_MODEL), lambda j: (0, j)),
        out_shape=jax.ShapeDtypeStruct((rows, n_out), F32),
        name="adaln",
    )(c_pad, w_ada, b_ada)


def _inproj_kernel(x_ref, mod_ref, nw_ref, w_ref, wab_ref, proj_ref, ab_ref):
    h = _rms(x_ref[...], nw_ref[...]) * (1.0 + mod_ref[0, 1:2, :]) + mod_ref[0, 0:1, :]
    hb = h.astype(BF16)
    for j in range(PROJ_MAIN // 512):
        cols = slice(j * 512, (j + 1) * 512)
        proj_ref[:, cols] = _dot(hb, w_ref[:, cols]).astype(BF16)
    ab_ref[...] = _dot(hb, wab_ref[...])


def _inproj(x2d, mod3, norm_w, w_main, w_ab, seq):
    n = x2d.shape[0]
    tiles_per_batch = seq // TM_INPROJ
    return pl.pallas_call(
        _inproj_kernel,
        grid=(n // TM_INPROJ,),
        in_specs=[
            pl.BlockSpec((TM_INPROJ, D_MODEL), lambda i: (i, 0)),
            pl.BlockSpec((1, N_MOD, D_MODEL), lambda i: (i // tiles_per_batch, 0, 0)),
            pl.BlockSpec((1, D_MODEL), lambda i: (0, 0)),
            pl.BlockSpec((D_MODEL, PROJ_MAIN), lambda i: (0, 0)),
            pl.BlockSpec((D_MODEL, LANES), lambda i: (0, 0)),
        ],
        out_specs=[
            pl.BlockSpec((TM_INPROJ, PROJ_MAIN), lambda i: (i, 0)),
            pl.BlockSpec((TM_INPROJ, LANES), lambda i: (i, 0)),
        ],
        out_shape=[
            jax.ShapeDtypeStruct((n, PROJ_MAIN), BF16),
            jax.ShapeDtypeStruct((n, LANES), F32),
        ],
        compiler_params=pltpu.CompilerParams(
            dimension_semantics=("parallel",), vmem_limit_bytes=VMEM_LIMIT),
        name="in_proj",
    )(x2d, mod3, norm_w, w_main, w_ab)


def _unit_lower_inverses(a_list):
    c = a_list[0].shape[0]
    row = lax.broadcasted_iota(jnp.int32, (c, c), 0)
    col = lax.broadcasted_iota(jnp.int32, (c, c), 1)
    eye = jnp.where(row == col, 1.0, 0.0).astype(F32)
    ps = [eye - a for a in a_list]
    qs = [a.astype(BF16) for a in a_list]
    qs = [_dot(q, q) for q in qs]
    power = 2
    while 2 * power < GDN_CHUNK:
        qbs = [q.astype(BF16) for q in qs]
        ps = [p + _dot(p.astype(BF16), qb) for p, qb in zip(ps, qbs)]
        qs = [_dot(qb, qb) for qb in qbs]
        power *= 2
    return [p + _dot(p.astype(BF16), q.astype(BF16)) for p, q in zip(ps, qs)]


def _mix_prep_kernel(proj_ref, ab_ref, vnw_ref, wsp_ref, bsp_ref, cw_ref, shift_ref, alog_ref,
                     dtb_ref, ya_ref, qt_ref, kt_ref, w_ref, u_ref, qk_ref, dec_ref, ext_ref,
                     *, tiles_per_batch):
    tm = TM_PREP
    i = pl.program_id(0)

    row = lax.broadcasted_iota(jnp.int32, (GM_CHUNK, GM_CHUNK), 0)
    col = lax.broadcasted_iota(jnp.int32, (GM_CHUNK, GM_CHUNK), 1)
    causal = row >= col
    for g in range(GM_GROUPS):
        ws = jnp.where(causal, wsp_ref[g], 0.0).astype(BF16)
        bcol = bsp_ref[:, g:g + 1]
        cols_u = slice(g * GM_DIM, (g + 1) * GM_DIM)
        cols_v = slice(GM_WIDTH + g * GM_DIM, GM_WIDTH + (g + 1) * GM_DIM)
        for c in range(tm // GM_CHUNK):
            rows = slice(c * GM_CHUNK, (c + 1) * GM_CHUNK)
            u = _gelu(proj_ref[rows, cols_u].astype(F32))
            v = _rms(_gelu(proj_ref[rows, cols_v].astype(F32)), vnw_ref[g:g + 1, :])
            z = _dot(ws, v.astype(BF16)) + bcol
            ya_ref[rows, cols_u] = (u * z).astype(BF16)

    @pl.when(i % tiles_per_batch == 0)
    def _():
        ext_ref[0:SUBLANES, :] = jnp.zeros((SUBLANES, 3 * GDN_WIDTH), F32)

    qkv_cols = slice(2 * GM_WIDTH, 2 * GM_WIDTH + 3 * GDN_WIDTH)
    xq = proj_ref[:, qkv_cols]
    xf = xq.astype(F32)
    conv = jnp.zeros((tm, 3 * GDN_WIDTH), F32)
    for j in range(GDN_CONV - 1):
        conv = conv + cw_ref[j:j + 1, :] * _dot(shift_ref[j], xq)
    conv = conv + cw_ref[GDN_CONV - 1:GDN_CONV, :] * xf
    ext_ref[SUBLANES:2 * SUBLANES, :] = xf[0:SUBLANES]
    head = jnp.zeros((SUBLANES, 3 * GDN_WIDTH), F32)
    for j in range(GDN_CONV):
        start = SUBLANES - (GDN_CONV - 1) + j
        head = head + cw_ref[j:j + 1, :] * ext_ref[start:start + SUBLANES, :]
    conv = jnp.concatenate([head, conv[SUBLANES:]], axis=0)
    ext_ref[0:SUBLANES, :] = xf[tm - SUBLANES:tm]
    act = _silu(conv)

    ab = ab_ref[...]
    sp_in = ab + dtb_ref[...]
    g_all = -jnp.exp(alog_ref[...]) * (
        jnp.maximum(sp_in, 0.0) + jnp.log1p(jnp.exp(-jnp.abs(sp_in))))
    beta_all = jax.nn.sigmoid(ab)
    trow = lax.broadcasted_iota(jnp.int32, (tm, tm), 0)
    tcol = lax.broadcasted_iota(jnp.int32, (tm, tm), 1)
    blk_lower = jnp.where((trow >= tcol) & (trow // GDN_CHUNK == tcol // GDN_CHUNK),
                          1.0, 0.0).astype(BF16)
    g_hi, g_mid, g_lo = _split3(g_all)
    gc_all = _dot(blk_lower, g_hi) + _dot(blk_lower, g_mid) + _dot(blk_lower, g_lo)
    gc_all_t = gc_all.T

    prow = lax.broadcasted_iota(jnp.int32, (GDN_PAIR, GDN_PAIR), 0)
    pcol = lax.broadcasted_iota(jnp.int32, (GDN_PAIR, GDN_PAIR), 1)
    same_chunk = prow // GDN_CHUNK == pcol // GDN_CHUNK
    tri = (prow >= pcol) & same_chunk
    strict = (prow > pcol) & same_chunk
    first_half = lax.broadcasted_iota(jnp.int32, (GDN_PAIR, 1), 0) < GDN_CHUNK

    blocks = [(h, p) for h in range(GDN_HEADS) for p in range(tm // GDN_PAIR)]
    q_l, k_l, kb_l, kbf_l, beta_l, gcc_l, decay_l, v_l = [], [], [], [], [], [], [], []
    for h in range(GDN_HEADS):
        hq = slice(h * GDN_DK, (h + 1) * GDN_DK)
        hk = slice(GDN_WIDTH + h * GDN_DK, GDN_WIDTH + (h + 1) * GDN_DK)
        hv = slice(2 * GDN_WIDTH + h * GDN_DV, 2 * GDN_WIDTH + (h + 1) * GDN_DV)
        q_h = act[:, hq]
        k_h = act[:, hk]
        q_h = q_h * lax.rsqrt(jnp.sum(q_h * q_h, axis=-1, keepdims=True) + EPS) * (GDN_DK ** -0.5)
        k_h = k_h * lax.rsqrt(jnp.sum(k_h * k_h, axis=-1, keepdims=True) + EPS)
        v_h = act[:, hv]
        for p in range(tm // GDN_PAIR):
            rows = slice(p * GDN_PAIR, (p + 1) * GDN_PAIR)
            beta = beta_all[rows, GDN_HEADS + h:GDN_HEADS + h + 1]
            gcc = gc_all[rows, h:h + 1]
            gcr = gc_all_t[h:h + 1, p * GDN_PAIR:(p + 1) * GDN_PAIR]
            k = k_h[rows]
            q_l.append(q_h[rows])
            k_l.append(k)
            kb_l.append(k * beta)
            kbf_l.append(k.astype(BF16))
            beta_l.append(beta)
            gcc_l.append(gcc)
            v_l.append(v_h[rows])
            decay_l.append(jnp.where(tri, jnp.exp(jnp.where(tri, gcc - gcr, 0.0)), 0.0))

    kk_l = [_dot_nt(kb.astype(BF16), kbf) for kb, kbf in zip(kb_l, kbf_l)]
    a_l = [jnp.where(strict, kk * decay, 0.0) for kk, decay in zip(kk_l, decay_l)]
    t_l = _unit_lower_inverses(a_l)
    egc_l = [jnp.exp(gcc) for gcc in gcc_l]
    rhs_l = [jnp.concatenate([v * beta, kb * egc], axis=1).astype(BF16)
             for v, beta, kb, egc in zip(v_l, beta_l, kb_l, egc_l)]
    sol_l = [_dot(t.astype(BF16), rhs) for t, rhs in zip(t_l, rhs_l)]
    qk_l = [jnp.where(tri, _dot_nt(q.astype(BF16), kbf) * decay, 0.0)
            for q, kbf, decay in zip(q_l, kbf_l, decay_l)]

    for idx, (h, p) in enumerate(blocks):
        rows = slice(p * GDN_PAIR, (p + 1) * GDN_PAIR)
        cols = slice(h * GDN_DK, (h + 1) * GDN_DK)
        gcc = gcc_l[idx]
        gl0 = gcc[GDN_CHUNK - 1:GDN_CHUNK]
        gl1 = gcc[GDN_PAIR - 1:GDN_PAIR]
        g_last = jnp.where(first_half, gl0, gl1)
        u_ref[rows, cols] = sol_l[idx][:, :GDN_DV]
        w_ref[rows, cols] = sol_l[idx][:, GDN_DV:].astype(BF16)
        qt_ref[rows, cols] = (q_l[idx] * egc_l[idx]).astype(BF16)
        kt_ref[rows, cols] = (k_l[idx] * jnp.exp(g_last - gcc)).astype(BF16)
        qk_ref[rows, cols] = qk_l[idx].astype(BF16)
        dec_ref[2 * p, h:h + 1, :] = jnp.broadcast_to(jnp.exp(gl0), (1, LANES))
        dec_ref[2 * p + 1, h:h + 1, :] = jnp.broadcast_to(jnp.exp(gl1), (1, LANES))


def _mix_prep(proj, ab, vnorm_w, w_spatial, b_spatial_t, conv_w, alog_pad, dtb_pad, seq):
    n = proj.shape[0]
    tm = TM_PREP
    tiles_per_batch = seq // tm
    const2 = lambda i: (0, 0)
    shifts = jnp.stack([jnp.eye(tm, k=-(GDN_CONV - 1 - j), dtype=BF16)
                        for j in range(GDN_CONV - 1)])
    return pl.pallas_call(
        functools.partial(_mix_prep_kernel, tiles_per_batch=tiles_per_batch),
        grid=(n // tm,),
        in_specs=[
            pl.BlockSpec((tm, PROJ_MAIN), lambda i: (i, 0)),
            pl.BlockSpec((tm, LANES), lambda i: (i, 0)),
            pl.BlockSpec((GM_GROUPS, GM_DIM), const2),
            pl.BlockSpec((GM_GROUPS, GM_CHUNK, GM_CHUNK), lambda i: (0, 0, 0)),
            pl.BlockSpec((GM_CHUNK, GM_GROUPS), const2),
            pl.BlockSpec((GDN_CONV, 3 * GDN_WIDTH), const2),
            pl.BlockSpec((GDN_CONV - 1, tm, tm), lambda i: (0, 0, 0)),
            pl.BlockSpec((1, LANES), const2),
            pl.BlockSpec((1, LANES), const2),
        ],
        out_specs=[
            pl.BlockSpec((tm, GM_WIDTH), lambda i: (i, 0)),
            pl.BlockSpec((tm, GDN_WIDTH), lambda i: (i, 0)),
            pl.BlockSpec((tm, GDN_WIDTH), lambda i: (i, 0)),
            pl.BlockSpec((tm, GDN_WIDTH), lambda i: (i, 0)),
            pl.BlockSpec((tm, GDN_WIDTH), lambda i: (i, 0)),
            pl.BlockSpec((tm, GDN_WIDTH), lambda i: (i, 0)),
            pl.BlockSpec((tm // GDN_CHUNK, GDN_HEADS, LANES), lambda i: (i, 0, 0)),
        ],
        out_shape=[
            jax.ShapeDtypeStruct((n, GM_WIDTH), BF16),
            jax.ShapeDtypeStruct((n, GDN_WIDTH), BF16),
            jax.ShapeDtypeStruct((n, GDN_WIDTH), BF16),
            jax.ShapeDtypeStruct((n, GDN_WIDTH), BF16),
            jax.ShapeDtypeStruct((n, GDN_WIDTH), F32),
            jax.ShapeDtypeStruct((n, GDN_WIDTH), BF16),
            jax.ShapeDtypeStruct((n // GDN_CHUNK, GDN_HEADS, LANES), F32),
        ],
        scratch_shapes=[pltpu.VMEM((2 * SUBLANES, 3 * GDN_WIDTH), F32)],
        compiler_params=pltpu.CompilerParams(
            dimension_semantics=("arbitrary",), vmem_limit_bytes=VMEM_LIMIT),
        name="mix_prep",
    )(proj, ab, vnorm_w, w_spatial, b_spatial_t, conv_w, shifts, alog_pad, dtb_pad)


def _gdn_scan_kernel(qt_ref, kt_ref, w_ref, u_ref, qk_ref, dec_ref, z_ref, onw_ref,
                     yb_ref, s_ref):
    @pl.when(pl.program_id(1) == 0)
    def _():
        s_ref[...] = jnp.zeros(s_ref.shape, F32)

    chains = [(i, h) for i in range(SCAN_BATCH) for h in range(GDN_HEADS)]
    hcols = [slice(h * GDN_DK, (h + 1) * GDN_DK) for h in range(GDN_HEADS)]
    states = [s_ref[i, h] for i, h in chains]
    v_prev = [None] * len(chains)
    for c in range(TM_SCAN // GDN_CHUNK):
        rows = slice(c * GDN_CHUNK, (c + 1) * GDN_CHUNK)
        states_b = [s.astype(BF16) for s in states]
        ws = [_dot(w_ref[i, rows, hcols[h]], sb) for (i, h), sb in zip(chains, states_b)]
        qs = [_dot(qt_ref[i, rows, hcols[h]], sb) for (i, h), sb in zip(chains, states_b)]
        v_new = [(u_ref[i, rows, hcols[h]] - wsn).astype(BF16) for (i, h), wsn in zip(chains, ws)]
        if c % 2 == 0:
            o = [q + _dot(qk_ref[i, rows, h * GDN_DK:h * GDN_DK + GDN_CHUNK], vn)
                 for (i, h), q, vn in zip(chains, qs, v_new)]
        else:
            o = [q + _dot(qk_ref[i, rows, hcols[h]], jnp.concatenate([vp, vn], axis=0))
                 for (i, h), q, vp, vn in zip(chains, qs, v_prev, v_new)]
        states = [s * dec_ref[i, c, h:h + 1, :] + _dot_tn(kt_ref[i, rows, hcols[h]], vn)
                  for (i, h), s, vn in zip(chains, states, v_new)]
        v_prev = v_new
        for (i, h), on in zip(chains, o):
            zz = z_ref[i, rows, hcols[h]].astype(F32)
            yb_ref[i, rows, hcols[h]] = (_rms(on, onw_ref[...]) * _silu(zz)).astype(BF16)
    for (i, h), s in zip(chains, states):
        s_ref[i, h] = s


def _gdn_scan(qt, kt, w, u, qk, dec, proj, onorm_w, bsz, seq):
    n = qt.shape[0]
    tm = TM_SCAN
    nb = SCAN_BATCH
    assert bsz % nb == 0
    as3d = lambda a: a.reshape(bsz, seq, a.shape[-1])
    tok = lambda b, j: (b, j, 0)
    z_block = (2 * GM_WIDTH + 3 * GDN_WIDTH) // GDN_WIDTH
    wide = pl.BlockSpec((nb, tm, GDN_WIDTH), tok)
    out = pl.pallas_call(
        _gdn_scan_kernel,
        grid=(bsz // nb, seq // tm),
        in_specs=[
            wide, wide, wide, wide, wide,
            pl.BlockSpec((nb, tm // GDN_CHUNK, GDN_HEADS, LANES), lambda b, j: (b, j, 0, 0)),
            pl.BlockSpec((nb, tm, GDN_WIDTH), lambda b, j: (b, j, z_block)),
            pl.BlockSpec((1, GDN_DV), lambda b, j: (0, 0)),
        ],
        out_specs=wide,
        out_shape=jax.ShapeDtypeStruct((bsz, seq, GDN_WIDTH), BF16),
        scratch_shapes=[pltpu.VMEM((nb, GDN_HEADS, GDN_DK, GDN_DV), F32)],
        compiler_params=pltpu.CompilerParams(dimension_semantics=("arbitrary", "arbitrary")),
        name="gdn_scan",
    )(as3d(qt), as3d(kt), as3d(w), as3d(u), as3d(qk),
      dec.reshape(bsz, seq // GDN_CHUNK, GDN_HEADS, LANES), as3d(proj), onorm_w)
    return out.reshape(n, GDN_WIDTH)


def _out_router_kernel(ya_ref, yb_ref, x_ref, mod_ref, wo_ref, n2w_ref, wrh_ref, wrl_ref, br_ref,
                       x1_ref, h2_ref, ridx_ref, rw_ref, cnt_ref, carry_ref):
    tm = TM_ROUTER

    @pl.when(pl.program_id(0) == 0)
    def _():
        carry_ref[...] = jnp.zeros(carry_ref.shape, F32)

    mix = _dot(ya_ref[...], wo_ref[0:GM_WIDTH, :]) + _dot(yb_ref[...], wo_ref[GM_WIDTH:, :])
    x1 = x_ref[...] + mod_ref[0, 2:3, :] * mix
    x1_ref[...] = x1
    h2 = _rms(x1, n2w_ref[...]) * (1.0 + mod_ref[0, 4:5, :]) + mod_ref[0, 3:4, :]
    for j in range(ROW_TILE):
        h2_ref[pl.ds(j, tm, stride=ROW_TILE), :] = h2[:, j * LANES:(j + 1) * LANES]

    h_hi = h2.astype(BF16)
    h_lo = (h2 - h_hi.astype(F32)).astype(BF16)
    logits = (_dot(h_hi, wrh_ref[...]) + _dot(h_hi, wrl_ref[...]) + _dot(h_lo, wrh_ref[...])
              + br_ref[...])
    lane = lax.broadcasted_iota(jnp.int32, (tm, LANES), 1)
    work = jnp.where(lane < N_EXPERTS, logits, -jnp.inf)
    sel_e, sel_v = [], []
    for _ in range(TOP_K):
        m = jnp.max(work, axis=-1, keepdims=True)
        e = jnp.min(jnp.where(work == m, lane, LANES), axis=-1, keepdims=True)
        sel_e.append(e)
        sel_v.append(m)
        work = jnp.where(lane == e, -jnp.inf, work)
    ex = [jnp.exp(v - sel_v[0]) for v in sel_v]
    den = ex[0] + ex[1] + ex[2] + ex[3]

    ridx = jnp.zeros((tm, LANES), jnp.int32)
    rw = jnp.zeros((tm, LANES), F32)
    onehot = jnp.zeros((tm, LANES), F32)
    for k in range(TOP_K):
        onehot = onehot + jnp.where(lane == sel_e[k], 1.0, 0.0)
        ridx = jnp.where(lane == k, sel_e[k], ridx)
        rw = jnp.where(lane == k, ex[k] / den, rw)
    ridx_ref[...] = ridx
    rw_ref[...] = rw
    carry = carry_ref[0:1, :] + jnp.sum(onehot, axis=0, keepdims=True)
    carry_ref[...] = jnp.broadcast_to(carry, carry_ref.shape)
    cnt_ref[...] = jnp.broadcast_to(carry, cnt_ref.shape)


def _out_router(ya, yb, x2d, mod3, w_out, norm2_w, wr_hi, wr_lo, br_pad, seq):
    n = x2d.shape[0]
    tm = TM_ROUTER
    tpb = seq // tm
    tok = lambda i: (i, 0)
    const2 = lambda i: (0, 0)
    return pl.pallas_call(
        _out_router_kernel,
        grid=(n // tm,),
        in_specs=[
            pl.BlockSpec((tm, GM_WIDTH), tok),
            pl.BlockSpec((tm, GDN_WIDTH), tok),
            pl.BlockSpec((tm, D_MODEL), tok),
            pl.BlockSpec((1, N_MOD, D_MODEL), lambda i: (i // tpb, 0, 0)),
            pl.BlockSpec((GM_WIDTH + GDN_WIDTH, D_MODEL), const2),
            pl.BlockSpec((1, D_MODEL), const2),
            pl.BlockSpec((D_MODEL, LANES), const2),
            pl.BlockSpec((D_MODEL, LANES), const2),
            pl.BlockSpec((1, LANES), const2),
        ],
        out_specs=[
            pl.BlockSpec((tm, D_MODEL), tok),
            pl.BlockSpec((tm * ROW_TILE, LANES), tok),
            pl.BlockSpec((tm, LANES), tok),
            pl.BlockSpec((tm, LANES), tok),
            pl.BlockSpec((SUBLANES, LANES), const2),
        ],
        out_shape=[
            jax.ShapeDtypeStruct((n, D_MODEL), F32),
            jax.ShapeDtypeStruct((n * ROW_TILE, LANES), F32),
            jax.ShapeDtypeStruct((n, LANES), jnp.int32),
            jax.ShapeDtypeStruct((n, LANES), F32),
            jax.ShapeDtypeStruct((SUBLANES, LANES), F32),
        ],
        scratch_shapes=[pltpu.VMEM((SUBLANES, LANES), F32)],
        compiler_params=pltpu.CompilerParams(
            dimension_semantics=("arbitrary",), vmem_limit_bytes=VMEM_LIMIT),
        name="out_router",
    )(ya, yb, x2d, mod3, w_out, norm2_w, wr_hi, wr_lo, br_pad)


def _moe_kernel(be_ref, nvb_ref, rows_hbm, h2_hbm, wgu_ref, bgu_ref, wd_ref, bd_ref,
                y4_hbm, idx_smem, xbuf, ynew, ysrc, wgu_b, wd_b, gsem, ssem, isem):
    t = MOE_BLOCK
    b = pl.program_id(0)
    last = nvb_ref[0] - 1

    ring_row = 2 * t
    rt = ROW_TILE

    def idx_copy(block):
        slot = (block + IDX_RING) % IDX_RING
        first = pl.multiple_of((block + 1) * ring_row, ring_row)
        return pltpu.make_async_copy(rows_hbm.at[pl.ds(first, ring_row)],
                                     idx_smem.at[pl.ds(slot * ring_row, ring_row)], isem.at[slot])

    def idx_base(block):
        return ((block + IDX_RING) % IDX_RING) * ring_row

    def tile_of(row):
        first = row * rt
        return pl.ds(first if isinstance(first, int) else pl.multiple_of(first, rt), rt)

    def gather_row(base, r, dst, sem):
        return pltpu.make_async_copy(h2_hbm.at[tile_of(idx_smem[base + r])], dst.at[tile_of(r)], sem)

    def scatter_row(base, r, src, sem):
        return pltpu.make_async_copy(src.at[tile_of(r)], y4_hbm.at[tile_of(idx_smem[base + t + r])],
                                     sem)

    def wait_gather(dst, sem):
        pltpu.make_async_copy(h2_hbm.at[pl.ds(0, t * rt)], dst, sem).wait()

    def wait_scatter(src, sem):
        pltpu.make_async_copy(src, y4_hbm.at[pl.ds(0, t * rt)], sem).wait()

    def load_rows(buf):
        return jnp.concatenate([buf[pl.ds(j, t, stride=rt), :] for j in range(rt)], axis=1)

    def store_rows(buf, val):
        for j in range(rt):
            buf[pl.ds(j, t, stride=rt), :] = val[:, j * LANES:(j + 1) * LANES]

    @pl.when(b == 0)
    def _():
        idx_copy(-1).start()
        idx_copy(0).start()
        idx_copy(1).start()
        idx_copy(-1).wait()
        idx_copy(0).wait()
        ynew[...] = jnp.zeros(ynew.shape, F32)
        dump = pltpu.make_async_copy(ynew, y4_hbm.at[pl.ds(y4_hbm.shape[0] - t * rt, t * rt)], ssem)
        dump.start()
        dump.wait()
        base0 = idx_base(0)

        def body(r, carry):
            gather_row(base0, r, xbuf, gsem).start()
            return carry
        lax.fori_loop(0, t, body, 0)

    @pl.when(b <= last)
    def _():
        @pl.when((b == 0) | (be_ref[b] != be_ref[jnp.maximum(b - 1, 0)]))
        def _():
            wgu_b[...] = wgu_ref[0].astype(BF16)
            wd_b[...] = wd_ref[0].astype(BF16)

        idx_copy(b + 2).start()
        idx_copy(b + 1).wait()
        wait_gather(xbuf, gsem)
        xbf = load_rows(xbuf).astype(BF16)

        gbase = idx_base(b + 1)
        sbase = idx_base(b - 1)

        def issue_gathers(lo, hi):
            for r in range(lo, hi):
                gather_row(gbase, r, xbuf, gsem).start(priority=0)

        def issue_scatters(lo, hi):
            for r in range(lo, hi):
                scatter_row(sbase, r, ysrc, ssem).start(priority=1)

        def ffn_part(f, issue_a, issue_b):
            gcols = slice(f * FF_TILE, (f + 1) * FF_TILE)
            ucols = slice(D_FF + f * FF_TILE, D_FF + (f + 1) * FF_TILE)
            issue_a()
            gate = _dot(xbf, wgu_b[:, gcols]) + bgu_ref[0, :, gcols]
            up = _dot(xbf, wgu_b[:, ucols]) + bgu_ref[0, :, ucols]
            gate = jnp.minimum(gate, SWIGLU_LIMIT)
            up = jnp.clip(up, -SWIGLU_LIMIT, SWIGLU_LIMIT)
            act = gate * jax.nn.sigmoid(SWIGLU_ALPHA * gate) * (up + 1.0)
            issue_b()
            return _dot(act.astype(BF16), wd_b[gcols, :])

        n_f = D_FF // FF_TILE
        acc = ffn_part(0, lambda: issue_gathers(0, t), lambda: None)
        for f in range(1, n_f - 1):
            acc = acc + ffn_part(f, lambda: None, lambda: None)

        @pl.when(b >= 1)
        def _():
            wait_scatter(ysrc, ssem)
        ysrc[...] = ynew[...]

        acc = acc + ffn_part(n_f - 1, lambda: issue_scatters(0, t), lambda: None)
        store_rows(ynew, acc + bd_ref[0])

        @pl.when(b == last)
        def _():
            wait_scatter(ysrc, ssem)
            base_b = idx_base(b)

            def body(r, carry):
                scatter_row(base_b, r, ynew, ssem).start()
                return carry
            lax.fori_loop(0, t, body, 0)
            wait_scatter(ynew, ssem)
            wait_gather(xbuf, gsem)
            idx_copy(b + 2).wait()


def _moe_ffn(block_e, nvb, block_rows, h2, w_gu, b_gu, w_down, b_down, n_blocks, n_rows_out):
    t = MOE_BLOCK

    def w_map(b, be, nv):
        return (be[b], 0, 0)

    grid_spec = pltpu.PrefetchScalarGridSpec(
        num_scalar_prefetch=2,
        grid=(n_blocks,),
        in_specs=[
            pl.BlockSpec(memory_space=pl.ANY),
            pl.BlockSpec(memory_space=pl.ANY),
            pl.BlockSpec((1, D_MODEL, 2 * D_FF), w_map),
            pl.BlockSpec((1, 1, 2 * D_FF), w_map),
            pl.BlockSpec((1, D_FF, D_MODEL), w_map),
            pl.BlockSpec((1, 1, D_MODEL), w_map),
        ],
        out_specs=pl.BlockSpec(memory_space=pl.ANY),
        scratch_shapes=[
            pltpu.SMEM((IDX_RING * 2 * t,), jnp.int32),
            pltpu.VMEM((t * ROW_TILE, LANES), F32),
            pltpu.VMEM((t * ROW_TILE, LANES), F32),
            pltpu.VMEM((t * ROW_TILE, LANES), F32),
            pltpu.VMEM((D_MODEL, 2 * D_FF), BF16),
            pltpu.VMEM((D_FF, D_MODEL), BF16),
            pltpu.SemaphoreType.DMA(()),
            pltpu.SemaphoreType.DMA(()),
            pltpu.SemaphoreType.DMA((IDX_RING,)),
        ],
    )
    return pl.pallas_call(
        _moe_kernel,
        grid_spec=grid_spec,
        out_shape=jax.ShapeDtypeStruct((n_rows_out * ROW_TILE, LANES), F32),
        compiler_params=pltpu.CompilerParams(
            dimension_semantics=("arbitrary",), vmem_limit_bytes=MOE_VMEM_LIMIT),
        name="moe_ffn",
    )(block_e, nvb, block_rows, h2, w_gu, b_gu, w_down, b_down)


def _combine_kernel(ya_ref, yb_ref, yc_ref, yd_ref, x1_ref, rw_ref, mod_ref, nfw_ref, out_ref):
    tm = TM_COMBINE
    rw = rw_ref[...]
    cols = []
    for j in range(ROW_TILE):
        rows = pl.ds(j, tm, stride=ROW_TILE)
        col = rw[:, 0:1] * ya_ref[rows, :]
        for k, y_ref in enumerate((yb_ref, yc_ref, yd_ref), start=1):
            col = col + rw[:, k:k + 1] * y_ref[rows, :]
        cols.append(col)
    x2 = x1_ref[...] + mod_ref[0, 5:6, :] * jnp.concatenate(cols, axis=1)
    out_ref[...] = _rms(x2, nfw_ref[...])


def _combine(y4, x1, rw, mod3, norm_f_w, seq):
    n = x1.shape[0]
    tm = TM_COMBINE
    tpb = seq // tm
    tok = lambda i: (i, 0)
    y_specs = [pl.BlockSpec((tm * ROW_TILE, LANES),
                            functools.partial(lambda i, k: (k * (n // tm) + i, 0), k=k))
               for k in range(TOP_K)]
    return pl.pallas_call(
        _combine_kernel,
        grid=(n // tm,),
        in_specs=y_specs + [
            pl.BlockSpec((tm, D_MODEL), tok),
            pl.BlockSpec((tm, LANES), tok),
            pl.BlockSpec((1, N_MOD, D_MODEL), lambda i: (i // tpb, 0, 0)),
            pl.BlockSpec((1, D_MODEL), lambda i: (0, 0)),
        ],
        out_specs=pl.BlockSpec((tm, D_MODEL), tok),
        out_shape=jax.ShapeDtypeStruct((n, D_MODEL), F32),
        compiler_params=pltpu.CompilerParams(
            dimension_semantics=("parallel",), vmem_limit_bytes=VMEM_LIMIT),
        name="combine",
    )(y4, y4, y4, y4, x1, rw, mod3, norm_f_w)


def _pad_lanes(v, fill=0.0):
    out = jnp.full((1, LANES), fill, F32)
    return out.at[0, :v.shape[0]].set(v.astype(F32))


def kernel(x, c, w_ada, b_ada, norm1_w, w_in, gm_vnorm_w, gm_w_spatial, gm_b_spatial, gdn_conv_w,
           gdn_a_log, gdn_dt_bias, gdn_onorm_w, w_out, norm2_w, w_router, b_router, w_gu, b_gu,
           w_down, b_down, norm_f_w):
    bsz, seq, d = x.shape
    n = bsz * seq
    assert w_ada.shape[0] == 1, "the closing RMSNorm is fused into the single layer's combine call"
    l = 0
    x2d = x.reshape(n, d)
    c_pad = jnp.zeros((SUBLANES, d), F32).at[:bsz].set(c)

    mod = _adaln(c_pad, w_ada[l], b_ada[l][None, :])[:bsz]
    mod3 = mod.reshape(bsz, N_MOD, d)

    w_main = w_in[l][:, :PROJ_MAIN].astype(BF16)
    w_ab = jnp.zeros((d, LANES), BF16).at[:, :2 * GDN_HEADS].set(
        w_in[l][:, PROJ_MAIN:].astype(BF16))
    proj, ab = _inproj(x2d, mod3, norm1_w[l][None, :], w_main, w_ab, seq)

    ya, qt, kt, wmat, umat, qk, dec = _mix_prep(
        proj, ab, gm_vnorm_w[l], gm_w_spatial[l], gm_b_spatial[l].T, gdn_conv_w[l],
        _pad_lanes(gdn_a_log[l]), _pad_lanes(gdn_dt_bias[l]), seq)
    yb = _gdn_scan(qt, kt, wmat, umat, qk, dec, proj, gdn_onorm_w[l][None, :], bsz, seq)

    wr_pad = jnp.zeros((d, LANES), F32).at[:, :N_EXPERTS].set(w_router[l])
    wr_hi = wr_pad.astype(BF16)
    wr_lo = (wr_pad - wr_hi.astype(F32)).astype(BF16)
    x1, h2, ridx, rw, cnt = _out_router(
        ya, yb, x2d, mod3, w_out[l].astype(BF16), norm2_w[l][None, :], wr_hi, wr_lo,
        _pad_lanes(b_router[l]), seq)

    t = MOE_BLOCK
    n_assign = n * TOP_K
    n_blocks = n_assign // t + N_EXPERTS
    e_flat = ridx[:, :TOP_K].T.reshape(n_assign)
    keys = e_flat * n_assign + jnp.arange(n_assign, dtype=jnp.int32)
    asg = jnp.sort(keys) % n_assign
    counts = cnt[0, :N_EXPERTS].astype(jnp.int32)
    start = jnp.cumsum(counts) - counts
    nblk = (counts + t - 1) // t
    blk_end = jnp.cumsum(nblk)
    blk_start = blk_end - nblk
    nvb = blk_end[-1]
    bid = jnp.arange(-1, n_blocks + 2, dtype=jnp.int32)
    live = (bid >= 0) & (bid < nvb)
    e_of = jnp.minimum(jnp.sum(blk_end[None, :] <= jnp.clip(bid, 0, nvb - 1)[:, None], axis=1),
                       N_EXPERTS - 1).astype(jnp.int32)
    j_of = bid - blk_start[e_of]
    row_start = jnp.where(live, start[e_of] + j_of * t, 0).astype(jnp.int32)
    n_valid = jnp.where(live, jnp.clip(counts[e_of] - j_of * t, 0, t), 0).astype(jnp.int32)
    block_e = e_of[1:n_blocks + 1]
    lane = jnp.arange(t, dtype=jnp.int32)
    a_blk = asg[jnp.minimum(row_start[:, None] + lane[None, :], n_assign - 1)]
    dst_rows = jnp.where(lane[None, :] < n_valid[:, None], a_blk, n_assign + lane[None, :])
    block_rows = jnp.concatenate([a_blk % n, dst_rows], axis=1).reshape(-1)

    y4 = _moe_ffn(block_e, nvb.reshape(1).astype(jnp.int32), block_rows, h2, w_gu[l],
                  b_gu[l][:, None, :], w_down[l], b_down[l][:, None, :], n_blocks, n_assign + t)
    out = _combine(y4, x1, rw, mod3, norm_f_w[None, :], seq)
    return out.reshape(bsz, seq, d)
```

```python
import functools

import jax
import jax.numpy as jnp
from jax import lax
from jax.experimental import pallas as pl
from jax.experimental.pallas import tpu as pltpu

F32 = jnp.float32
BF16 = jnp.bfloat16
HIGHEST = lax.Precision.HIGHEST

D_MODEL = 1024
GM_GROUPS = 4
GM_DIM = 128
GM_WIDTH = GM_GROUPS * GM_DIM
GM_CHUNK = 128
GDN_HEADS = 4
GDN_DK = 128
GDN_DV = 128
GDN_WIDTH = GDN_HEADS * GDN_DK
GDN_CONV = 4
GDN_CHUNK = 64
GDN_PAIR = 2 * GDN_CHUNK
N_EXPERTS = 32
TOP_K = 4
D_FF = D_MODEL
SWIGLU_LIMIT = 7.0
SWIGLU_ALPHA = 1.702
N_MOD = 6
EPS = 1e-6

LANES = 128
SUBLANES = 8
PROJ_MAIN = 2 * GM_WIDTH + 4 * GDN_WIDTH

TM_INPROJ = 512
TM_PREP = 256
TM_SCAN = 256
SCAN_BATCH = 4
TM_ROUTER = 512
MOE_BLOCK = 256
TM_COMBINE = 512
FF_TILE = 512
ROW_TILE = D_MODEL // LANES
IDX_RING = 4
VMEM_LIMIT = 48 * 1024 * 1024
MOE_VMEM_LIMIT = 56 * 1024 * 1024


def _dot(a, b):
    return jnp.dot(a, b, preferred_element_type=F32)


def _dot_nt(a, b):
    return lax.dot_general(a, b, (((1,), (1,)), ((), ())), preferred_element_type=F32)


def _dot_tn(a, b):
    return lax.dot_general(a, b, (((0,), (0,)), ((), ())), preferred_element_type=F32)


def _split3(x):
    hi = x.astype(BF16)
    r1 = x - hi.astype(F32)
    mid = r1.astype(BF16)
    lo = (r1 - mid.astype(F32)).astype(BF16)
    return hi, mid, lo


def _rms(x, w):
    return x * lax.rsqrt(jnp.mean(x * x, axis=-1, keepdims=True) + EPS) * w


def _gelu(x):
    return 0.5 * x * (1.0 + lax.erf(x * (2.0 ** -0.5)))


def _silu(x):
    return x * jax.nn.sigmoid(x)


def _adaln_kernel(c_ref, w_ref, b_ref, o_ref):
    c = c_ref[...]
    o_ref[...] = jnp.dot(_silu(c), w_ref[...], precision=HIGHEST,
                         preferred_element_type=F32) + b_ref[...]


def _adaln(c_pad, w_ada, b_ada):
    rows = c_pad.shape[0]
    n_out = w_ada.shape[1]
    return pl.pallas_call(
        _adaln_kernel,
        grid=(n_out // D_MODEL,),
        in_specs=[
            pl.BlockSpec((rows, D_MODEL), lambda j: (0, 0)),
            pl.BlockSpec((D_MODEL, D_MODEL), lambda j: (0, j)),
            pl.BlockSpec((1, D_MODEL), lambda j: (0, j)),
        ],
        out_specs=pl.BlockSpec((rows, D_MODEL), lambda j: (0, j)),
        out_shape=jax.ShapeDtypeStruct((rows, n_out), F32),
        name="adaln",
    )(c_pad, w_ada, b_ada)


def _inproj_kernel(x_ref, mod_ref, nw_ref, w_ref, wab_ref, proj_ref, ab_ref):
    h = _rms(x_ref[...], nw_ref[...]) * (1.0 + mod_ref[0, 1:2, :]) + mod_ref[0, 0:1, :]
    hb = h.astype(BF16)
    for j in range(PROJ_MAIN // 512):
        cols = slice(j * 512, (j + 1) * 512)
        proj_ref[:, cols] = _dot(hb, w_ref[:, cols]).astype(BF16)
    ab_ref[...] = _dot(hb, wab_ref[...])


def _inproj(x2d, mod3, norm_w, w_main, w_ab, seq):
    n = x2d.shape[0]
    tiles_per_batch = seq // TM_INPROJ
    return pl.pallas_call(
        _inproj_kernel,
        grid=(n // TM_INPROJ,),
        in_specs=[
            pl.BlockSpec((TM_INPROJ, D_MODEL), lambda i: (i, 0)),
            pl.BlockSpec((1, N_MOD, D_MODEL), lambda i: (i // tiles_per_batch, 0, 0)),
            pl.BlockSpec((1, D_MODEL), lambda i: (0, 0)),
            pl.BlockSpec((D_MODEL, PROJ_MAIN), lambda i: (0, 0)),
            pl.BlockSpec((D_MODEL, LANES), lambda i: (0, 0)),
        ],
        out_specs=[
            pl.BlockSpec((TM_INPROJ, PROJ_MAIN), lambda i: (i, 0)),
            pl.BlockSpec((TM_INPROJ, LANES), lambda i: (i, 0)),
        ],
        out_shape=[
            jax.ShapeDtypeStruct((n, PROJ_MAIN), BF16),
            jax.ShapeDtypeStruct((n, LANES), F32),
        ],
        compiler_params=pltpu.CompilerParams(
            dimension_semantics=("parallel",), vmem_limit_bytes=VMEM_LIMIT),
        name="in_proj",
    )(x2d, mod3, norm_w, w_main, w_ab)


def _unit_lower_inverses(a_list):
    c = a_list[0].shape[0]
    row = lax.broadcasted_iota(jnp.int32, (c, c), 0)
    col = lax.broadcasted_iota(jnp.int32, (c, c), 1)
    eye = jnp.where(row == col, 1.0, 0.0).astype(F32)
    ps = [eye - a for a in a_list]
    qs = [a.astype(BF16) for a in a_list]
    qs = [_dot(q, q) for q in qs]
    power = 2
    while 2 * power < GDN_CHUNK:
        qbs = [q.astype(BF16) for q in qs]
        ps = [p + _dot(p.astype(BF16), qb) for p, qb in zip(ps, qbs)]
        qs = [_dot(qb, qb) for qb in qbs]
        power *= 2
    return [p + _dot(p.astype(BF16), q.astype(BF16)) for p, q in zip(ps, qs)]


def _mix_prep_kernel(proj_ref, ab_ref, vnw_ref, wsp_ref, bsp_ref, cw_ref, shift_ref, alog_ref,
                     dtb_ref, ya_ref, qt_ref, kt_ref, w_ref, u_ref, qk_ref, dec_ref, ext_ref,
                     *, tiles_per_batch):
    tm = TM_PREP
    i = pl.program_id(0)

    row = lax.broadcasted_iota(jnp.int32, (GM_CHUNK, GM_CHUNK), 0)
    col = lax.broadcasted_iota(jnp.int32, (GM_CHUNK, GM_CHUNK), 1)
    causal = row >= col
    for g in range(GM_GROUPS):
        ws = jnp.where(causal, wsp_ref[g], 0.0).astype(BF16)
        bcol = bsp_ref[:, g:g + 1]
        cols_u = slice(g * GM_DIM, (g + 1) * GM_DIM)
        cols_v = slice(GM_WIDTH + g * GM_DIM, GM_WIDTH + (g + 1) * GM_DIM)
        for c in range(tm // GM_CHUNK):
            rows = slice(c * GM_CHUNK, (c + 1) * GM_CHUNK)
            u = _gelu(proj_ref[rows, cols_u].astype(F32))
            v = _rms(_gelu(proj_ref[rows, cols_v].astype(F32)), vnw_ref[g:g + 1, :])
            z = _dot(ws, v.astype(BF16)) + bcol
            ya_ref[rows, cols_u] = (u * z).astype(BF16)

    @pl.when(i % tiles_per_batch == 0)
    def _():
        ext_ref[0:SUBLANES, :] = jnp.zeros((SUBLANES, 3 * GDN_WIDTH), F32)

    qkv_cols = slice(2 * GM_WIDTH, 2 * GM_WIDTH + 3 * GDN_WIDTH)
    xq = proj_ref[:, qkv_cols]
    xf = xq.astype(F32)
    conv = jnp.zeros((tm, 3 * GDN_WIDTH), F32)
    for j in range(GDN_CONV - 1):
        conv = conv + cw_ref[j:j + 1, :] * _dot(shift_ref[j], xq)
    conv = conv + cw_ref[GDN_CONV - 1:GDN_CONV, :] * xf
    ext_ref[SUBLANES:2 * SUBLANES, :] = xf[0:SUBLANES]
    head = jnp.zeros((SUBLANES, 3 * GDN_WIDTH), F32)
    for j in range(GDN_CONV):
        start = SUBLANES - (GDN_CONV - 1) + j
        head = head + cw_ref[j:j + 1, :] * ext_ref[start:start + SUBLANES, :]
    conv = jnp.concatenate([head, conv[SUBLANES:]], axis=0)
    ext_ref[0:SUBLANES, :] = xf[tm - SUBLANES:tm]
    act = _silu(conv)

    ab = ab_ref[...]
    sp_in = ab + dtb_ref[...]
    g_all = -jnp.exp(alog_ref[...]) * (
        jnp.maximum(sp_in, 0.0) + jnp.log1p(jnp.exp(-jnp.abs(sp_in))))
    beta_all = jax.nn.sigmoid(ab)
    trow = lax.broadcasted_iota(jnp.int32, (tm, tm), 0)
    tcol = lax.broadcasted_iota(jnp.int32, (tm, tm), 1)
    blk_lower = jnp.where((trow >= tcol) & (trow // GDN_CHUNK == tcol // GDN_CHUNK),
                          1.0, 0.0).astype(BF16)
    g_hi, g_mid, g_lo = _split3(g_all)
    gc_all = _dot(blk_lower, g_hi) + _dot(blk_lower, g_mid) + _dot(blk_lower, g_lo)
    gc_all_t = gc_all.T

    prow = lax.broadcasted_iota(jnp.int32, (GDN_PAIR, GDN_PAIR), 0)
    pcol = lax.broadcasted_iota(jnp.int32, (GDN_PAIR, GDN_PAIR), 1)
    same_chunk = prow // GDN_CHUNK == pcol // GDN_CHUNK
    tri = (prow >= pcol) & same_chunk
    strict = (prow > pcol) & same_chunk
    first_half = lax.broadcasted_iota(jnp.int32, (GDN_PAIR, 1), 0) < GDN_CHUNK

    blocks = [(h, p) for h in range(GDN_HEADS) for p in range(tm // GDN_PAIR)]
    q_l, k_l, kb_l, kbf_l, beta_l, gcc_l, decay_l, v_l = [], [], [], [], [], [], [], []
    for h in range(GDN_HEADS):
        hq = slice(h * GDN_DK, (h + 1) * GDN_DK)
        hk = slice(GDN_WIDTH + h * GDN_DK, GDN_WIDTH + (h + 1) * GDN_DK)
        hv = slice(2 * GDN_WIDTH + h * GDN_DV, 2 * GDN_WIDTH + (h + 1) * GDN_DV)
        q_h = act[:, hq]
        k_h = act[:, hk]
        q_h = q_h * lax.rsqrt(jnp.sum(q_h * q_h, axis=-1, keepdims=True) + EPS) * (GDN_DK ** -0.5)
        k_h = k_h * lax.rsqrt(jnp.sum(k_h * k_h, axis=-1, keepdims=True) + EPS)
        v_h = act[:, hv]
        for p in range(tm // GDN_PAIR):
            rows = slice(p * GDN_PAIR, (p + 1) * GDN_PAIR)
            beta = beta_all[rows, GDN_HEADS + h:GDN_HEADS + h + 1]
            gcc = gc_all[rows, h:h + 1]
            gcr = gc_all_t[h:h + 1, p * GDN_PAIR:(p + 1) * GDN_PAIR]
            k = k_h[rows]
            q_l.append(q_h[rows])
            k_l.append(k)
            kb_l.append(k * beta)
            kbf_l.append(k.astype(BF16))
            beta_l.append(beta)
            gcc_l.append(gcc)
            v_l.append(v_h[rows])
            decay_l.append(jnp.where(tri, jnp.exp(jnp.where(tri, gcc - gcr, 0.0)), 0.0))

    kk_l = [_dot_nt(kb.astype(BF16), kbf) for kb, kbf in zip(kb_l, kbf_l)]
    a_l = [jnp.where(strict, kk * decay, 0.0) for kk, decay in zip(kk_l, decay_l)]
    t_l = _unit_lower_inverses(a_l)
    egc_l = [jnp.exp(gcc) for gcc in gcc_l]
    rhs_l = [jnp.concatenate([v * beta, kb * egc], axis=1).astype(BF16)
             for v, beta, kb, egc in zip(v_l, beta_l, kb_l, egc_l)]
    sol_l = [_dot(t.astype(BF16), rhs) for t, rhs in zip(t_l, rhs_l)]
    qk_l = [jnp.where(tri, _dot_nt(q.astype(BF16), kbf) * decay, 0.0)
            for q, kbf, decay in zip(q_l, kbf_l, decay_l)]

    for idx, (h, p) in enumerate(blocks):
        rows = slice(p * GDN_PAIR, (p + 1) * GDN_PAIR)
        cols = slice(h * GDN_DK, (h + 1) * GDN_DK)
        gcc = gcc_l[idx]
        gl0 = gcc[GDN_CHUNK - 1:GDN_CHUNK]
        gl1 = gcc[GDN_PAIR - 1:GDN_PAIR]
        g_last = jnp.where(first_half, gl0, gl1)
        u_ref[rows, cols] = sol_l[idx][:, :GDN_DV]
        w_ref[rows, cols] = sol_l[idx][:, GDN_DV:].astype(BF16)
        qt_ref[rows, cols] = (q_l[idx] * egc_l[idx]).astype(BF16)
        kt_ref[rows, cols] = (k_l[idx] * jnp.exp(g_last - gcc)).astype(BF16)
        qk_ref[rows, cols] = qk_l[idx].astype(BF16)
        dec_ref[2 * p, h:h + 1, :] = jnp.broadcast_to(jnp.exp(gl0), (1, LANES))
        dec_ref[2 * p + 1, h:h + 1, :] = jnp.broadcast_to(jnp.exp(gl1), (1, LANES))


def _mix_prep(proj, ab, vnorm_w, w_spatial, b_spatial_t, conv_w, alog_pad, dtb_pad, seq):
    n = proj.shape[0]
    tm = TM_PREP
    tiles_per_batch = seq // tm
    const2 = lambda i: (0, 0)
    shifts = jnp.stack([jnp.eye(tm, k=-(GDN_CONV - 1 - j), dtype=BF16)
                        for j in range(GDN_CONV - 1)])
    return pl.pallas_call(
        functools.partial(_mix_prep_kernel, tiles_per_batch=tiles_per_batch),
        grid=(n // tm,),
        in_specs=[
            pl.BlockSpec((tm, PROJ_MAIN), lambda i: (i, 0)),
            pl.BlockSpec((tm, LANES), lambda i: (i, 0)),
            pl.BlockSpec((GM_GROUPS, GM_DIM), const2),
            pl.BlockSpec((GM_GROUPS, GM_CHUNK, GM_CHUNK), lambda i: (0, 0, 0)),
            pl.BlockSpec((GM_CHUNK, GM_GROUPS), const2),
            pl.BlockSpec((GDN_CONV, 3 * GDN_WIDTH), const2),
            pl.BlockSpec((GDN_CONV - 1, tm, tm), lambda i: (0, 0, 0)),
            pl.BlockSpec((1, LANES), const2),
            pl.BlockSpec((1, LANES), const2),
        ],
        out_specs=[
            pl.BlockSpec((tm, GM_WIDTH), lambda i: (i, 0)),
            pl.BlockSpec((tm, GDN_WIDTH), lambda i: (i, 0)),
            pl.BlockSpec((tm, GDN_WIDTH), lambda i: (i, 0)),
            pl.BlockSpec((tm, GDN_WIDTH), lambda i: (i, 0)),
            pl.BlockSpec((tm, GDN_WIDTH), lambda i: (i, 0)),
            pl.BlockSpec((tm, GDN_WIDTH), lambda i: (i, 0)),
            pl.BlockSpec((tm // GDN_CHUNK, GDN_HEADS, LANES), lambda i: (i, 0, 0)),
        ],
        out_shape=[
            jax.ShapeDtypeStruct((n, GM_WIDTH), BF16),
            jax.ShapeDtypeStruct((n, GDN_WIDTH), BF16),
            jax.ShapeDtypeStruct((n, GDN_WIDTH), BF16),
            jax.ShapeDtypeStruct((n, GDN_WIDTH), BF16),
            jax.ShapeDtypeStruct((n, GDN_WIDTH), F32),
            jax.ShapeDtypeStruct((n, GDN_WIDTH), BF16),
            jax.ShapeDtypeStruct((n // GDN_CHUNK, GDN_HEADS, LANES), F32),
        ],
        scratch_shapes=[pltpu.VMEM((2 * SUBLANES, 3 * GDN_WIDTH), F32)],
        compiler_params=pltpu.CompilerParams(
            dimension_semantics=("arbitrary",), vmem_limit_bytes=VMEM_LIMIT),
        name="mix_prep",
    )(proj, ab, vnorm_w, w_spatial, b_spatial_t, conv_w, shifts, alog_pad, dtb_pad)


def _gdn_scan_kernel(qt_ref, kt_ref, w_ref, u_ref, qk_ref, dec_ref, z_ref, onw_ref,
                     yb_ref, s_ref):
    @pl.when(pl.program_id(1) == 0)
    def _():
        s_ref[...] = jnp.zeros(s_ref.shape, F32)

    chains = [(i, h) for i in range(SCAN_BATCH) for h in range(GDN_HEADS)]
    hcols = [slice(h * GDN_DK, (h + 1) * GDN_DK) for h in range(GDN_HEADS)]
    states = [s_ref[i, h] for i, h in chains]
    v_prev = [None] * len(chains)
    for c in range(TM_SCAN // GDN_CHUNK):
        rows = slice(c * GDN_CHUNK, (c + 1) * GDN_CHUNK)
        states_b = [s.astype(BF16) for s in states]
        ws = [_dot(w_ref[i, rows, hcols[h]], sb) for (i, h), sb in zip(chains, states_b)]
        qs = [_dot(qt_ref[i, rows, hcols[h]], sb) for (i, h), sb in zip(chains, states_b)]
        v_new = [(u_ref[i, rows, hcols[h]] - wsn).astype(BF16) for (i, h), wsn in zip(chains, ws)]
        if c % 2 == 0:
            o = [q + _dot(qk_ref[i, rows, h * GDN_DK:h * GDN_DK + GDN_CHUNK], vn)
                 for (i, h), q, vn in zip(chains, qs, v_new)]
        else:
            o = [q + _dot(qk_ref[i, rows, hcols[h]], jnp.concatenate([vp, vn], axis=0))
                 for (i, h), q, vp, vn in zip(chains, qs, v_prev, v_new)]
        states = [s * dec_ref[i, c, h:h + 1, :] + _dot_tn(kt_ref[i, rows, hcols[h]], vn)
                  for (i, h), s, vn in zip(chains, states, v_new)]
        v_prev = v_new
        for (i, h), on in zip(chains, o):
            zz = z_ref[i, rows, hcols[h]].astype(F32)
            yb_ref[i, rows, hcols[h]] = (_rms(on, onw_ref[...]) * _silu(zz)).astype(BF16)
    for (i, h), s in zip(chains, states):
        s_ref[i, h] = s


def _gdn_scan(qt, kt, w, u, qk, dec, proj, onorm_w, bsz, seq):
    n = qt.shape[0]
    tm = TM_SCAN
    nb = SCAN_BATCH
    assert bsz % nb == 0
    as3d = lambda a: a.reshape(bsz, seq, a.shape[-1])
    tok = lambda b, j: (b, j, 0)
    z_block = (2 * GM_WIDTH + 3 * GDN_WIDTH) // GDN_WIDTH
    wide = pl.BlockSpec((nb, tm, GDN_WIDTH), tok)
    out = pl.pallas_call(
        _gdn_scan_kernel,
        grid=(bsz // nb, seq // tm),
        in_specs=[
            wide, wide, wide, wide, wide,
            pl.BlockSpec((nb, tm // GDN_CHUNK, GDN_HEADS, LANES), lambda b, j: (b, j, 0, 0)),
            pl.BlockSpec((nb, tm, GDN_WIDTH), lambda b, j: (b, j, z_block)),
            pl.BlockSpec((1, GDN_DV), lambda b, j: (0, 0)),
        ],
        out_specs=wide,
        out_shape=jax.ShapeDtypeStruct((bsz, seq, GDN_WIDTH), BF16),
        scratch_shapes=[pltpu.VMEM((nb, GDN_HEADS, GDN_DK, GDN_DV), F32)],
        compiler_params=pltpu.CompilerParams(dimension_semantics=("arbitrary", "arbitrary")),
        name="gdn_scan",
    )(as3d(qt), as3d(kt), as3d(w), as3d(u), as3d(qk),
      dec.reshape(bsz, seq // GDN_CHUNK, GDN_HEADS, LANES), as3d(proj), onorm_w)
    return out.reshape(n, GDN_WIDTH)


def _out_router_kernel(ya_ref, yb_ref, x_ref, mod_ref, wo_ref, n2w_ref, wrh_ref, wrl_ref, br_ref,
                       x1_ref, h2_ref, ridx_ref, rw_ref, cnt_ref, carry_ref):
    tm = TM_ROUTER

    @pl.when(pl.program_id(0) == 0)
    def _():
        carry_ref[...] = jnp.zeros(carry_ref.shape, F32)

    mix = _dot(ya_ref[...], wo_ref[0:GM_WIDTH, :]) + _dot(yb_ref[...], wo_ref[GM_WIDTH:, :])
    x1 = x_ref[...] + mod_ref[0, 2:3, :] * mix
    x1_ref[...] = x1
    h2 = _rms(x1, n2w_ref[...]) * (1.0 + mod_ref[0, 4:5, :]) + mod_ref[0, 3:4, :]
    for j in range(ROW_TILE):
        h2_ref[pl.ds(j, tm, stride=ROW_TILE), :] = h2[:, j * LANES:(j + 1) * LANES]

    h_hi = h2.astype(BF16)
    h_lo = (h2 - h_hi.astype(F32)).astype(BF16)
    logits = (_dot(h_hi, wrh_ref[...]) + _dot(h_hi, wrl_ref[...]) + _dot(h_lo, wrh_ref[...])
              + br_ref[...])
    lane = lax.broadcasted_iota(jnp.int32, (tm, LANES), 1)
    work = jnp.where(lane < N_EXPERTS, logits, -jnp.inf)
    sel_e, sel_v = [], []
    for _ in range(TOP_K):
        m = jnp.max(work, axis=-1, keepdims=True)
        e = jnp.min(jnp.where(work == m, lane, LANES), axis=-1, keepdims=True)
        sel_e.append(e)
        sel_v.append(m)
        work = jnp.where(lane == e, -jnp.inf, work)
    ex = [jnp.exp(v - sel_v[0]) for v in sel_v]
    den = ex[0] + ex[1] + ex[2] + ex[3]

    ridx = jnp.zeros((tm, LANES), jnp.int32)
    rw = jnp.zeros((tm, LANES), F32)
    onehot = jnp.zeros((tm, LANES), F32)
    for k in range(TOP_K):
        onehot = onehot + jnp.where(lane == sel_e[k], 1.0, 0.0)
        ridx = jnp.where(lane == k, sel_e[k], ridx)
        rw = jnp.where(lane == k, ex[k] / den, rw)
    ridx_ref[...] = ridx
    rw_ref[...] = rw
    carry = carry_ref[0:1, :] + jnp.sum(onehot, axis=0, keepdims=True)
    carry_ref[...] = jnp.broadcast_to(carry, carry_ref.shape)
    cnt_ref[...] = jnp.broadcast_to(carry, cnt_ref.shape)


def _out_router(ya, yb, x2d, mod3, w_out, norm2_w, wr_hi, wr_lo, br_pad, seq):
    n = x2d.shape[0]
    tm = TM_ROUTER
    tpb = seq // tm
    tok = lambda i: (i, 0)
    const2 = lambda i: (0, 0)
    return pl.pallas_call(
        _out_router_kernel,
        grid=(n // tm,),
        in_specs=[
            pl.BlockSpec((tm, GM_WIDTH), tok),
            pl.BlockSpec((tm, GDN_WIDTH), tok),
            pl.BlockSpec((tm, D_MODEL), tok),
            pl.BlockSpec((1, N_MOD, D_MODEL), lambda i: (i // tpb, 0, 0)),
            pl.BlockSpec((GM_WIDTH + GDN_WIDTH, D_MODEL), const2),
            pl.BlockSpec((1, D_MODEL), const2),
            pl.BlockSpec((D_MODEL, LANES), const2),
            pl.BlockSpec((D_MODEL, LANES), const2),
            pl.BlockSpec((1, LANES), const2),
        ],
        out_specs=[
            pl.BlockSpec((tm, D_MODEL), tok),
            pl.BlockSpec((tm * ROW_TILE, LANES), tok),
            pl.BlockSpec((tm, LANES), tok),
            pl.BlockSpec((tm, LANES), tok),
            pl.BlockSpec((SUBLANES, LANES), const2),
        ],
        out_shape=[
            jax.ShapeDtypeStruct((n, D_MODEL), F32),
            jax.ShapeDtypeStruct((n * ROW_TILE, LANES), F32),
            jax.ShapeDtypeStruct((n, LANES), jnp.int32),
            jax.ShapeDtypeStruct((n, LANES), F32),
            jax.ShapeDtypeStruct((SUBLANES, LANES), F32),
        ],
        scratch_shapes=[pltpu.VMEM((SUBLANES, LANES), F32)],
        compiler_params=pltpu.CompilerParams(
            dimension_semantics=("arbitrary",), vmem_limit_bytes=VMEM_LIMIT),
        name="out_router",
    )(ya, yb, x2d, mod3, w_out, norm2_w, wr_hi, wr_lo, br_pad)


def _moe_kernel(be_ref, nvb_ref, rows_hbm, h2_hbm, wgu_ref, bgu_ref, wd_ref, bd_ref,
                y4_hbm, idx_smem, xbuf, ynew, ysrc, wgu_b, wd_b, gsem, ssem, isem):
    t = MOE_BLOCK
    b = pl.program_id(0)
    last = nvb_ref[0] - 1

    ring_row = 2 * t
    rt = ROW_TILE

    def idx_copy(block):
        slot = (block + IDX_RING) % IDX_RING
        first = pl.multiple_of((block + 1) * ring_row, ring_row)
        return pltpu.make_async_copy(rows_hbm.at[pl.ds(first, ring_row)],
                                     idx_smem.at[pl.ds(slot * ring_row, ring_row)], isem.at[slot])

    def idx_base(block):
        return ((block + IDX_RING) % IDX_RING) * ring_row

    def tile_of(row):
        first = row * rt
        return pl.ds(first if isinstance(first, int) else pl.multiple_of(first, rt), rt)

    def gather_row(base, r, dst, sem):
        return pltpu.make_async_copy(h2_hbm.at[tile_of(idx_smem[base + r])], dst.at[tile_of(r)], sem)

    def scatter_row(base, r, src, sem):
        return pltpu.make_async_copy(src.at[tile_of(r)], y4_hbm.at[tile_of(idx_smem[base + t + r])],
                                     sem)

    def wait_gather(dst, sem):
        pltpu.make_async_copy(h2_hbm.at[pl.ds(0, t * rt)], dst, sem).wait()

    def wait_scatter(src, sem):
        pltpu.make_async_copy(src, y4_hbm.at[pl.ds(0, t * rt)], sem).wait()

    def load_rows(buf):
        return jnp.concatenate([buf[pl.ds(j, t, stride=rt), :] for j in range(rt)], axis=1)

    def store_rows(buf, val):
        for j in range(rt):
            buf[pl.ds(j, t, stride=rt), :] = val[:, j * LANES:(j + 1) * LANES]

    @pl.when(b == 0)
    def _():
        idx_copy(-1).start()
        idx_copy(0).start()
        idx_copy(1).start()
        idx_copy(-1).wait()
        idx_copy(0).wait()
        ynew[...] = jnp.zeros(ynew.shape, F32)
        dump = pltpu.make_async_copy(ynew, y4_hbm.at[pl.ds(y4_hbm.shape[0] - t * rt, t * rt)], ssem)
        dump.start()
        dump.wait()
        base0 = idx_base(0)

        def body(r, carry):
            gather_row(base0, r, xbuf, gsem).start()
            return carry
        lax.fori_loop(0, t, body, 0)

    @pl.when(b <= last)
    def _():
        @pl.when((b == 0) | (be_ref[b] != be_ref[jnp.maximum(b - 1, 0)]))
        def _():
            wgu_b[...] = wgu_ref[0].astype(BF16)
            wd_b[...] = wd_ref[0].astype(BF16)

        idx_copy(b + 2).start()
        idx_copy(b + 1).wait()
        wait_gather(xbuf, gsem)
        xbf = load_rows(xbuf).astype(BF16)

        gbase = idx_base(b + 1)
        sbase = idx_base(b - 1)

        def issue_gathers(lo, hi):
            for r in range(lo, hi):
                gather_row(gbase, r, xbuf, gsem).start(priority=0)

        def issue_scatters(lo, hi):
            for r in range(lo, hi):
                scatter_row(sbase, r, ysrc, ssem).start(priority=1)

        issue_gathers(0, t)
        gate = _dot(xbf, wgu_b[:, :D_FF]) + bgu_ref[0, :, :D_FF]
        up = _dot(xbf, wgu_b[:, D_FF:]) + bgu_ref[0, :, D_FF:]
        gate = jnp.minimum(gate, SWIGLU_LIMIT)
        up = jnp.clip(up, -SWIGLU_LIMIT, SWIGLU_LIMIT)
        act = (gate * jax.nn.sigmoid(SWIGLU_ALPHA * gate) * (up + 1.0)).astype(BF16)

        @pl.when(b >= 1)
        def _():
            wait_scatter(ysrc, ssem)
        ysrc[...] = ynew[...]

        issue_scatters(0, t)
        store_rows(ynew, _dot(act, wd_b[...]) + bd_ref[0])

        @pl.when(b == last)
        def _():
            wait_scatter(ysrc, ssem)
            base_b = idx_base(b)

            def body(r, carry):
                scatter_row(base_b, r, ynew, ssem).start()
                return carry
            lax.fori_loop(0, t, body, 0)
            wait_scatter(ynew, ssem)
            wait_gather(xbuf, gsem)
            idx_copy(b + 2).wait()


def _moe_ffn(block_e, nvb, block_rows, h2, w_gu, b_gu, w_down, b_down, n_blocks, n_rows_out):
    t = MOE_BLOCK

    def w_map(b, be, nv):
        return (be[b], 0, 0)

    grid_spec = pltpu.PrefetchScalarGridSpec(
        num_scalar_prefetch=2,
        grid=(n_blocks,),
        in_specs=[
            pl.BlockSpec(memory_space=pl.ANY),
            pl.BlockSpec(memory_space=pl.ANY),
            pl.BlockSpec((1, D_MODEL, 2 * D_FF), w_map),
            pl.BlockSpec((1, 1, 2 * D_FF), w_map),
            pl.BlockSpec((1, D_FF, D_MODEL), w_map),
            pl.BlockSpec((1, 1, D_MODEL), w_map),
        ],
        out_specs=pl.BlockSpec(memory_space=pl.ANY),
        scratch_shapes=[
            pltpu.SMEM((IDX_RING * 2 * t,), jnp.int32),
            pltpu.VMEM((t * ROW_TILE, LANES), F32),
            pltpu.VMEM((t * ROW_TILE, LANES), F32),
            pltpu.VMEM((t * ROW_TILE, LANES), F32),
            pltpu.VMEM((D_MODEL, 2 * D_FF), BF16),
            pltpu.VMEM((D_FF, D_MODEL), BF16),
            pltpu.SemaphoreType.DMA(()),
            pltpu.SemaphoreType.DMA(()),
            pltpu.SemaphoreType.DMA((IDX_RING,)),
        ],
    )
    return pl.pallas_call(
        _moe_kernel,
        grid_spec=grid_spec,
        out_shape=jax.ShapeDtypeStruct((n_rows_out * ROW_TILE, LANES), F32),
        compiler_params=pltpu.CompilerParams(
            dimension_semantics=("arbitrary",), vmem_limit_bytes=MOE_VMEM_LIMIT),
        name="moe_ffn",
    )(block_e, nvb, block_rows, h2, w_gu, b_gu, w_down, b_down)


def _combine_kernel(ya_ref, yb_ref, yc_ref, yd_ref, x1_ref, rw_ref, mod_ref, nfw_ref, out_ref):
    tm = TM_COMBINE
    rw = rw_ref[...]
    cols = []
    for j in range(ROW_TILE):
        rows = pl.ds(j, tm, stride=ROW_TILE)
        col = rw[:, 0:1] * ya_ref[rows, :]
        for k, y_ref in enumerate((yb_ref, yc_ref, yd_ref), start=1):
            col = col + rw[:, k:k + 1] * y_ref[rows, :]
        cols.append(col)
    x2 = x1_ref[...] + mod_ref[0, 5:6, :] * jnp.concatenate(cols, axis=1)
    out_ref[...] = _rms(x2, nfw_ref[...])


def _combine(y4, x1, rw, mod3, norm_f_w, seq):
    n = x1.shape[0]
    tm = TM_COMBINE
    tpb = seq // tm
    tok = lambda i: (i, 0)
    y_specs = [pl.BlockSpec((tm * ROW_TILE, LANES),
                            functools.partial(lambda i, k: (k * (n // tm) + i, 0), k=k))
               for k in range(TOP_K)]
    return pl.pallas_call(
        _combine_kernel,
        grid=(n // tm,),
        in_specs=y_specs + [
            pl.BlockSpec((tm, D_MODEL), tok),
            pl.BlockSpec((tm, LANES), tok),
            pl.BlockSpec((1, N_MOD, D_MODEL), lambda i: (i // tpb, 0, 0)),
            pl.BlockSpec((1, D_MODEL), lambda i: (0, 0)),
        ],
        out_specs=pl.BlockSpec((tm, D_MODEL), tok),
        out_shape=jax.ShapeDtypeStruct((n, D_MODEL), F32),
        compiler_params=pltpu.CompilerParams(
            dimension_semantics=("parallel",), vmem_limit_bytes=VMEM_LIMIT),
        name="combine",
    )(y4, y4, y4, y4, x1, rw, mod3, norm_f_w)


def _pad_lanes(v, fill=0.0):
    out = jnp.full((1, LANES), fill, F32)
    return out.at[0, :v.shape[0]].set(v.astype(F32))


def kernel(x, c, w_ada, b_ada, norm1_w, w_in, gm_vnorm_w, gm_w_spatial, gm_b_spatial, gdn_conv_w,
           gdn_a_log, gdn_dt_bias, gdn_onorm_w, w_out, norm2_w, w_router, b_router, w_gu, b_gu,
           w_down, b_down, norm_f_w):
    bsz, seq, d = x.shape
    n = bsz * seq
    assert w_ada.shape[0] == 1, "the closing RMSNorm is fused into the single layer's combine call"
    l = 0
    x2d = x.reshape(n, d)
    c_pad = jnp.zeros((SUBLANES, d), F32).at[:bsz].set(c)

    mod = _adaln(c_pad, w_ada[l], b_ada[l][None, :])[:bsz]
    mod3 = mod.reshape(bsz, N_MOD, d)

    w_main = w_in[l][:, :PROJ_MAIN].astype(BF16)
    w_ab = jnp.zeros((d, LANES), BF16).at[:, :2 * GDN_HEADS].set(
        w_in[l][:, PROJ_MAIN:].astype(BF16))
    proj, ab = _inproj(x2d, mod3, norm1_w[l][None, :], w_main, w_ab, seq)

    ya, qt, kt, wmat, umat, qk, dec = _mix_prep(
        proj, ab, gm_vnorm_w[l], gm_w_spatial[l], gm_b_spatial[l].T, gdn_conv_w[l],
        _pad_lanes(gdn_a_log[l]), _pad_lanes(gdn_dt_bias[l]), seq)
    yb = _gdn_scan(qt, kt, wmat, umat, qk, dec, proj, gdn_onorm_w[l][None, :], bsz, seq)

    wr_pad = jnp.zeros((d, LANES), F32).at[:, :N_EXPERTS].set(w_router[l])
    wr_hi = wr_pad.astype(BF16)
    wr_lo = (wr_pad - wr_hi.astype(F32)).astype(BF16)
    x1, h2, ridx, rw, cnt = _out_router(
        ya, yb, x2d, mod3, w_out[l].astype(BF16), norm2_w[l][None, :], wr_hi, wr_lo,
        _pad_lanes(b_router[l]), seq)

    t = MOE_BLOCK
    n_assign = n * TOP_K
    n_blocks = n_assign // t + N_EXPERTS
    e_flat = ridx[:, :TOP_K].T.reshape(n_assign)
    keys = e_flat * n_assign + jnp.arange(n_assign, dtype=jnp.int32)
    asg = jnp.sort(keys) % n_assign
    counts = cnt[0, :N_EXPERTS].astype(jnp.int32)
    start = jnp.cumsum(counts) - counts
    nblk = (counts + t - 1) // t
    blk_end = jnp.cumsum(nblk)
    blk_start = blk_end - nblk
    nvb = blk_end[-1]
    bid = jnp.arange(-1, n_blocks + 2, dtype=jnp.int32)
    live = (bid >= 0) & (bid < nvb)
    e_of = jnp.minimum(jnp.sum(blk_end[None, :] <= jnp.clip(bid, 0, nvb - 1)[:, None], axis=1),
                       N_EXPERTS - 1).astype(jnp.int32)
    j_of = bid - blk_start[e_of]
    row_start = jnp.where(live, start[e_of] + j_of * t, 0).astype(jnp.int32)
    n_valid = jnp.where(live, jnp.clip(counts[e_of] - j_of * t, 0, t), 0).astype(jnp.int32)
    block_e = e_of[1:n_blocks + 1]
    lane = jnp.arange(t, dtype=jnp.int32)
    a_blk = asg[jnp.minimum(row_start[:, None] + lane[None, :], n_assign - 1)]
    dst_rows = jnp.where(lane[None, :] < n_valid[:, None], a_blk, n_assign + lane[None, :])
    block_rows = jnp.concatenate([a_blk % n, dst_rows], axis=1).reshape(-1)

    y4 = _moe_ffn(block_e, nvb.reshape(1).astype(jnp.int32), block_rows, h2, w_gu[l],
                  b_gu[l][:, None, :], w_down[l], b_down[l][:, None, :], n_blocks, n_assign + t)
    out = _combine(y4, x1, rw, mod3, norm_f_w[None, :], seq)
    return out.reshape(bsz, seq, d)
```

```python
import functools

import jax
import jax.numpy as jnp
from jax import lax
from jax.experimental import pallas as pl
from jax.experimental.pallas import tpu as pltpu

F32 = jnp.float32
BF16 = jnp.bfloat16
HIGHEST = lax.Precision.HIGHEST

D_MODEL = 1024
GM_GROUPS = 4
GM_DIM = 128
GM_WIDTH = GM_GROUPS * GM_DIM
GM_CHUNK = 128
GDN_HEADS = 4
GDN_DK = 128
GDN_DV = 128
GDN_WIDTH = GDN_HEADS * GDN_DK
GDN_CONV = 4
GDN_CHUNK = 64
GDN_PAIR = 2 * GDN_CHUNK
N_EXPERTS = 32
TOP_K = 4
D_FF = D_MODEL
SWIGLU_LIMIT = 7.0
SWIGLU_ALPHA = 1.702
N_MOD = 6
EPS = 1e-6

LANES = 128
SUBLANES = 8
PROJ_MAIN = 2 * GM_WIDTH + 4 * GDN_WIDTH

TM_INPROJ = 512
TM_PREP = 256
TM_SCAN = 256
SCAN_BATCH = 4
TM_ROUTER = 1024
MOE_BLOCK = 256
TM_COMBINE = 512
ROW_TILE = D_MODEL // LANES
IDX_RING = 4
VMEM_LIMIT = 48 * 1024 * 1024
MOE_VMEM_LIMIT = 56 * 1024 * 1024


def _dot(a, b):
    return jnp.dot(a, b, preferred_element_type=F32)


def _dot_nt(a, b):
    return lax.dot_general(a, b, (((1,), (1,)), ((), ())), preferred_element_type=F32)


def _dot_tn(a, b):
    return lax.dot_general(a, b, (((0,), (0,)), ((), ())), preferred_element_type=F32)


def _split3(x):
    hi = x.astype(BF16)
    r1 = x - hi.astype(F32)
    mid = r1.astype(BF16)
    lo = (r1 - mid.astype(F32)).astype(BF16)
    return hi, mid, lo


def _rms(x, w):
    return x * lax.rsqrt(jnp.mean(x * x, axis=-1, keepdims=True) + EPS) * w


def _gelu(x):
    return 0.5 * x * (1.0 + lax.erf(x * (2.0 ** -0.5)))


def _silu(x):
    return x * jax.nn.sigmoid(x)


def _adaln_kernel(c_ref, w_ref, b_ref, o_ref):
    c = c_ref[...]
    o_ref[...] = jnp.dot(_silu(c), w_ref[...], precision=HIGHEST,
                         preferred_element_type=F32) + b_ref[...]


def _adaln(c_pad, w_ada, b_ada):
    rows = c_pad.shape[0]
    n_out = w_ada.shape[1]
    return pl.pallas_call(
        _adaln_kernel,
        grid=(n_out // D_MODEL,),
        in_specs=[
            pl.BlockSpec((rows, D_MODEL), lambda j: (0, 0)),
            pl.BlockSpec((D_MODEL, D_MODEL), lambda j: (0, j)),
            pl.BlockSpec((1, D_MODEL), lambda j: (0, j)),
        ],
        out_specs=pl.BlockSpec((rows, D_MODEL), lambda j: (0, j)),
        out_shape=jax.ShapeDtypeStruct((rows, n_out), F32),
        name="adaln",
    )(c_pad, w_ada, b_ada)


def _inproj_kernel(x_ref, mod_ref, nw_ref, w_ref, wab_ref, proj_ref, ab_ref):
    h = _rms(x_ref[...], nw_ref[...]) * (1.0 + mod_ref[0, 1:2, :]) + mod_ref[0, 0:1, :]
    hb = h.astype(BF16)
    for j in range(PROJ_MAIN // 512):
        cols = slice(j * 512, (j + 1) * 512)
        proj_ref[:, cols] = _dot(hb, w_ref[:, cols]).astype(BF16)
    ab_ref[...] = _dot(hb, wab_ref[...])


def _inproj(x2d, mod3, norm_w, w_main, w_ab, seq):
    n = x2d.shape[0]
    tiles_per_batch = seq // TM_INPROJ
    return pl.pallas_call(
        _inproj_kernel,
        grid=(n // TM_INPROJ,),
        in_specs=[
            pl.BlockSpec((TM_INPROJ, D_MODEL), lambda i: (i, 0)),
            pl.BlockSpec((1, N_MOD, D_MODEL), lambda i: (i // tiles_per_batch, 0, 0)),
            pl.BlockSpec((1, D_MODEL), lambda i: (0, 0)),
            pl.BlockSpec((D_MODEL, PROJ_MAIN), lambda i: (0, 0)),
            pl.BlockSpec((D_MODEL, LANES), lambda i: (0, 0)),
        ],
        out_specs=[
            pl.BlockSpec((TM_INPROJ, PROJ_MAIN), lambda i: (i, 0)),
            pl.BlockSpec((TM_INPROJ, LANES), lambda i: (i, 0)),
        ],
        out_shape=[
            jax.ShapeDtypeStruct((n, PROJ_MAIN), BF16),
            jax.ShapeDtypeStruct((n, LANES), F32),
        ],
        compiler_params=pltpu.CompilerParams(
            dimension_semantics=("parallel",), vmem_limit_bytes=VMEM_LIMIT),
        name="in_proj",
    )(x2d, mod3, norm_w, w_main, w_ab)


def _unit_lower_inverses(a_list):
    c = a_list[0].shape[0]
    row = lax.broadcasted_iota(jnp.int32, (c, c), 0)
    col = lax.broadcasted_iota(jnp.int32, (c, c), 1)
    eye = jnp.where(row == col, 1.0, 0.0).astype(F32)
    ps = [eye - a for a in a_list]
    qs = [a.astype(BF16) for a in a_list]
    qs = [_dot(q, q) for q in qs]
    power = 2
    while 2 * power < GDN_CHUNK:
        qbs = [q.astype(BF16) for q in qs]
        ps = [p + _dot(p.astype(BF16), qb) for p, qb in zip(ps, qbs)]
        qs = [_dot(qb, qb) for qb in qbs]
        power *= 2
    return [p + _dot(p.astype(BF16), q.astype(BF16)) for p, q in zip(ps, qs)]


def _mix_prep_kernel(proj_ref, ab_ref, vnw_ref, wsp_ref, bsp_ref, cw_ref, shift_ref, alog_ref,
                     dtb_ref, ya_ref, qt_ref, kt_ref, w_ref, u_ref, qk_ref, dec_ref, ext_ref,
                     *, tiles_per_batch):
    tm = TM_PREP
    i = pl.program_id(0)

    row = lax.broadcasted_iota(jnp.int32, (GM_CHUNK, GM_CHUNK), 0)
    col = lax.broadcasted_iota(jnp.int32, (GM_CHUNK, GM_CHUNK), 1)
    causal = row >= col
    for g in range(GM_GROUPS):
        ws = jnp.where(causal, wsp_ref[g], 0.0).astype(BF16)
        bcol = bsp_ref[:, g:g + 1]
        cols_u = slice(g * GM_DIM, (g + 1) * GM_DIM)
        cols_v = slice(GM_WIDTH + g * GM_DIM, GM_WIDTH + (g + 1) * GM_DIM)
        for c in range(tm // GM_CHUNK):
            rows = slice(c * GM_CHUNK, (c + 1) * GM_CHUNK)
            u = _gelu(proj_ref[rows, cols_u].astype(F32))
            v = _rms(_gelu(proj_ref[rows, cols_v].astype(F32)), vnw_ref[g:g + 1, :])
            z = _dot(ws, v.astype(BF16)) + bcol
            ya_ref[rows, cols_u] = (u * z).astype(BF16)

    @pl.when(i % tiles_per_batch == 0)
    def _():
        ext_ref[0:SUBLANES, :] = jnp.zeros((SUBLANES, 3 * GDN_WIDTH), F32)

    qkv_cols = slice(2 * GM_WIDTH, 2 * GM_WIDTH + 3 * GDN_WIDTH)
    xq = proj_ref[:, qkv_cols]
    xf = xq.astype(F32)
    conv = jnp.zeros((tm, 3 * GDN_WIDTH), F32)
    for j in range(GDN_CONV - 1):
        conv = conv + cw_ref[j:j + 1, :] * _dot(shift_ref[j], xq)
    conv = conv + cw_ref[GDN_CONV - 1:GDN_CONV, :] * xf
    ext_ref[SUBLANES:2 * SUBLANES, :] = xf[0:SUBLANES]
    head = jnp.zeros((SUBLANES, 3 * GDN_WIDTH), F32)
    for j in range(GDN_CONV):
        start = SUBLANES - (GDN_CONV - 1) + j
        head = head + cw_ref[j:j + 1, :] * ext_ref[start:start + SUBLANES, :]
    conv = jnp.concatenate([head, conv[SUBLANES:]], axis=0)
    ext_ref[0:SUBLANES, :] = xf[tm - SUBLANES:tm]
    act = _silu(conv)

    ab = ab_ref[...]
    sp_in = ab + dtb_ref[...]
    g_all = -jnp.exp(alog_ref[...]) * (
        jnp.maximum(sp_in, 0.0) + jnp.log1p(jnp.exp(-jnp.abs(sp_in))))
    beta_all = jax.nn.sigmoid(ab)
    trow = lax.broadcasted_iota(jnp.int32, (tm, tm), 0)
    tcol = lax.broadcasted_iota(jnp.int32, (tm, tm), 1)
    blk_lower = jnp.where((trow >= tcol) & (trow // GDN_CHUNK == tcol // GDN_CHUNK),
                          1.0, 0.0).astype(BF16)
    g_hi, g_mid, g_lo = _split3(g_all)
    gc_all = _dot(blk_lower, g_hi) + _dot(blk_lower, g_mid) + _dot(blk_lower, g_lo)
    gc_all_t = gc_all.T

    prow = lax.broadcasted_iota(jnp.int32, (GDN_PAIR, GDN_PAIR), 0)
    pcol = lax.broadcasted_iota(jnp.int32, (GDN_PAIR, GDN_PAIR), 1)
    same_chunk = prow // GDN_CHUNK == pcol // GDN_CHUNK
    tri = (prow >= pcol) & same_chunk
    strict = (prow > pcol) & same_chunk
    first_half = lax.broadcasted_iota(jnp.int32, (GDN_PAIR, 1), 0) < GDN_CHUNK

    blocks = [(h, p) for h in range(GDN_HEADS) for p in range(tm // GDN_PAIR)]
    q_l, k_l, kb_l, kbf_l, beta_l, gcc_l, decay_l, v_l = [], [], [], [], [], [], [], []
    for h in range(GDN_HEADS):
        hq = slice(h * GDN_DK, (h + 1) * GDN_DK)
        hk = slice(GDN_WIDTH + h * GDN_DK, GDN_WIDTH + (h + 1) * GDN_DK)
        hv = slice(2 * GDN_WIDTH + h * GDN_DV, 2 * GDN_WIDTH + (h + 1) * GDN_DV)
        q_h = act[:, hq]
        k_h = act[:, hk]
        q_h = q_h * lax.rsqrt(jnp.sum(q_h * q_h, axis=-1, keepdims=True) + EPS) * (GDN_DK ** -0.5)
        k_h = k_h * lax.rsqrt(jnp.sum(k_h * k_h, axis=-1, keepdims=True) + EPS)
        v_h = act[:, hv]
        for p in range(tm // GDN_PAIR):
            rows = slice(p * GDN_PAIR, (p + 1) * GDN_PAIR)
            beta = beta_all[rows, GDN_HEADS + h:GDN_HEADS + h + 1]
            gcc = gc_all[rows, h:h + 1]
            gcr = gc_all_t[h:h + 1, p * GDN_PAIR:(p + 1) * GDN_PAIR]
            k = k_h[rows]
            q_l.append(q_h[rows])
            k_l.append(k)
            kb_l.append(k * beta)
            kbf_l.append(k.astype(BF16))
            beta_l.append(beta)
            gcc_l.append(gcc)
            v_l.append(v_h[rows])
            decay_l.append(jnp.where(tri, jnp.exp(jnp.where(tri, gcc - gcr, 0.0)), 0.0))

    kk_l = [_dot_nt(kb.astype(BF16), kbf) for kb, kbf in zip(kb_l, kbf_l)]
    a_l = [jnp.where(strict, kk * decay, 0.0) for kk, decay in zip(kk_l, decay_l)]
    t_l = _unit_lower_inverses(a_l)
    egc_l = [jnp.exp(gcc) for gcc in gcc_l]
    rhs_l = [jnp.concatenate([v * beta, kb * egc], axis=1).astype(BF16)
             for v, beta, kb, egc in zip(v_l, beta_l, kb_l, egc_l)]
    sol_l = [_dot(t.astype(BF16), rhs) for t, rhs in zip(t_l, rhs_l)]
    qk_l = [jnp.where(tri, _dot_nt(q.astype(BF16), kbf) * decay, 0.0)
            for q, kbf, decay in zip(q_l, kbf_l, decay_l)]

    for idx, (h, p) in enumerate(blocks):
        rows = slice(p * GDN_PAIR, (p + 1) * GDN_PAIR)
        cols = slice(h * GDN_DK, (h + 1) * GDN_DK)
        gcc = gcc_l[idx]
        gl0 = gcc[GDN_CHUNK - 1:GDN_CHUNK]
        gl1 = gcc[GDN_PAIR - 1:GDN_PAIR]
        g_last = jnp.where(first_half, gl0, gl1)
        u_ref[rows, cols] = sol_l[idx][:, :GDN_DV]
        w_ref[rows, cols] = sol_l[idx][:, GDN_DV:].astype(BF16)
        qt_ref[rows, cols] = (q_l[idx] * egc_l[idx]).astype(BF16)
        kt_ref[rows, cols] = (k_l[idx] * jnp.exp(g_last - gcc)).astype(BF16)
        qk_ref[rows, cols] = qk_l[idx].astype(BF16)
        dec_ref[2 * p, h:h + 1, :] = jnp.broadcast_to(jnp.exp(gl0), (1, LANES))
        dec_ref[2 * p + 1, h:h + 1, :] = jnp.broadcast_to(jnp.exp(gl1), (1, LANES))


def _mix_prep(proj, ab, vnorm_w, w_spatial, b_spatial_t, conv_w, alog_pad, dtb_pad, seq):
    n = proj.shape[0]
    tm = TM_PREP
    tiles_per_batch = seq // tm
    const2 = lambda i: (0, 0)
    shifts = jnp.stack([jnp.eye(tm, k=-(GDN_CONV - 1 - j), dtype=BF16)
                        for j in range(GDN_CONV - 1)])
    return pl.pallas_call(
        functools.partial(_mix_prep_kernel, tiles_per_batch=tiles_per_batch),
        grid=(n // tm,),
        in_specs=[
            pl.BlockSpec((tm, PROJ_MAIN), lambda i: (i, 0)),
            pl.BlockSpec((tm, LANES), lambda i: (i, 0)),
            pl.BlockSpec((GM_GROUPS, GM_DIM), const2),
            pl.BlockSpec((GM_GROUPS, GM_CHUNK, GM_CHUNK), lambda i: (0, 0, 0)),
            pl.BlockSpec((GM_CHUNK, GM_GROUPS), const2),
            pl.BlockSpec((GDN_CONV, 3 * GDN_WIDTH), const2),
            pl.BlockSpec((GDN_CONV - 1, tm, tm), lambda i: (0, 0, 0)),
            pl.BlockSpec((1, LANES), const2),
            pl.BlockSpec((1, LANES), const2),
        ],
        out_specs=[
            pl.BlockSpec((tm, GM_WIDTH), lambda i: (i, 0)),
            pl.BlockSpec((tm, GDN_WIDTH), lambda i: (i, 0)),
            pl.BlockSpec((tm, GDN_WIDTH), lambda i: (i, 0)),
            pl.BlockSpec((tm, GDN_WIDTH), lambda i: (i, 0)),
            pl.BlockSpec((tm, GDN_WIDTH), lambda i: (i, 0)),
            pl.BlockSpec((tm, GDN_WIDTH), lambda i: (i, 0)),
            pl.BlockSpec((tm // GDN_CHUNK, GDN_HEADS, LANES), lambda i: (i, 0, 0)),
        ],
        out_shape=[
            jax.ShapeDtypeStruct((n, GM_WIDTH), BF16),
            jax.ShapeDtypeStruct((n, GDN_WIDTH), BF16),
            jax.ShapeDtypeStruct((n, GDN_WIDTH), BF16),
            jax.ShapeDtypeStruct((n, GDN_WIDTH), BF16),
            jax.ShapeDtypeStruct((n, GDN_WIDTH), F32),
            jax.ShapeDtypeStruct((n, GDN_WIDTH), BF16),
            jax.ShapeDtypeStruct((n // GDN_CHUNK, GDN_HEADS, LANES), F32),
        ],
        scratch_shapes=[pltpu.VMEM((2 * SUBLANES, 3 * GDN_WIDTH), F32)],
        compiler_params=pltpu.CompilerParams(
            dimension_semantics=("arbitrary",), vmem_limit_bytes=VMEM_LIMIT),
        name="mix_prep",
    )(proj, ab, vnorm_w, w_spatial, b_spatial_t, conv_w, shifts, alog_pad, dtb_pad)


def _gdn_scan_kernel(qt_ref, kt_ref, w_ref, u_ref, qk_ref, dec_ref, z_ref, onw_ref,
                     yb_ref, s_ref):
    @pl.when(pl.program_id(1) == 0)
    def _():
        s_ref[...] = jnp.zeros(s_ref.shape, F32)

    chains = [(i, h) for i in range(SCAN_BATCH) for h in range(GDN_HEADS)]
    hcols = [slice(h * GDN_DK, (h + 1) * GDN_DK) for h in range(GDN_HEADS)]
    states = [s_ref[i, h] for i, h in chains]
    v_prev = [None] * len(chains)
    for c in range(TM_SCAN // GDN_CHUNK):
        rows = slice(c * GDN_CHUNK, (c + 1) * GDN_CHUNK)
        states_b = [s.astype(BF16) for s in states]
        ws = [_dot(w_ref[i, rows, hcols[h]], sb) for (i, h), sb in zip(chains, states_b)]
        qs = [_dot(qt_ref[i, rows, hcols[h]], sb) for (i, h), sb in zip(chains, states_b)]
        v_new = [(u_ref[i, rows, hcols[h]] - wsn).astype(BF16) for (i, h), wsn in zip(chains, ws)]
        if c % 2 == 0:
            o = [q + _dot(qk_ref[i, rows, h * GDN_DK:h * GDN_DK + GDN_CHUNK], vn)
                 for (i, h), q, vn in zip(chains, qs, v_new)]
        else:
            o = [q + _dot(qk_ref[i, rows, hcols[h]], jnp.concatenate([vp, vn], axis=0))
                 for (i, h), q, vp, vn in zip(chains, qs, v_prev, v_new)]
        states = [s * dec_ref[i, c, h:h + 1, :] + _dot_tn(kt_ref[i, rows, hcols[h]], vn)
                  for (i, h), s, vn in zip(chains, states, v_new)]
        v_prev = v_new
        for (i, h), on in zip(chains, o):
            zz = z_ref[i, rows, hcols[h]].astype(F32)
            yb_ref[i, rows, hcols[h]] = (_rms(on, onw_ref[...]) * _silu(zz)).astype(BF16)
    for (i, h), s in zip(chains, states):
        s_ref[i, h] = s


def _gdn_scan(qt, kt, w, u, qk, dec, proj, onorm_w, bsz, seq):
    n = qt.shape[0]
    tm = TM_SCAN
    nb = SCAN_BATCH
    assert bsz % nb == 0
    as3d = lambda a: a.reshape(bsz, seq, a.shape[-1])
    tok = lambda b, j: (b, j, 0)
    z_block = (2 * GM_WIDTH + 3 * GDN_WIDTH) // GDN_WIDTH
    wide = pl.BlockSpec((nb, tm, GDN_WIDTH), tok)
    out = pl.pallas_call(
        _gdn_scan_kernel,
        grid=(bsz // nb, seq // tm),
        in_specs=[
            wide, wide, wide, wide, wide,
            pl.BlockSpec((nb, tm // GDN_CHUNK, GDN_HEADS, LANES), lambda b, j: (b, j, 0, 0)),
            pl.BlockSpec((nb, tm, GDN_WIDTH), lambda b, j: (b, j, z_block)),
            pl.BlockSpec((1, GDN_DV), lambda b, j: (0, 0)),
        ],
        out_specs=wide,
        out_shape=jax.ShapeDtypeStruct((bsz, seq, GDN_WIDTH), BF16),
        scratch_shapes=[pltpu.VMEM((nb, GDN_HEADS, GDN_DK, GDN_DV), F32)],
        compiler_params=pltpu.CompilerParams(dimension_semantics=("arbitrary", "arbitrary")),
        name="gdn_scan",
    )(as3d(qt), as3d(kt), as3d(w), as3d(u), as3d(qk),
      dec.reshape(bsz, seq // GDN_CHUNK, GDN_HEADS, LANES), as3d(proj), onorm_w)
    return out.reshape(n, GDN_WIDTH)


def _out_router_kernel(ya_ref, yb_ref, x_ref, mod_ref, wo_ref, n2w_ref, wrh_ref, wrl_ref, br_ref,
                       x1_ref, h2_ref, ridx_ref, rw_ref, cnt_ref, carry_ref):
    tm = TM_ROUTER

    @pl.when(pl.program_id(0) == 0)
    def _():
        carry_ref[...] = jnp.zeros(carry_ref.shape, F32)

    mix = _dot(ya_ref[...], wo_ref[0:GM_WIDTH, :]) + _dot(yb_ref[...], wo_ref[GM_WIDTH:, :])
    x1 = x_ref[...] + mod_ref[0, 2:3, :] * mix
    x1_ref[...] = x1
    h2 = _rms(x1, n2w_ref[...]) * (1.0 + mod_ref[0, 4:5, :]) + mod_ref[0, 3:4, :]
    for j in range(ROW_TILE):
        h2_ref[pl.ds(j, tm, stride=ROW_TILE), :] = h2[:, j * LANES:(j + 1) * LANES]

    h_hi = h2.astype(BF16)
    h_lo = (h2 - h_hi.astype(F32)).astype(BF16)
    logits = (_dot(h_hi, wrh_ref[...]) + _dot(h_hi, wrl_ref[...]) + _dot(h_lo, wrh_ref[...])
              + br_ref[...])
    lane = lax.broadcasted_iota(jnp.int32, (tm, LANES), 1)
    work = jnp.where(lane < N_EXPERTS, logits, -jnp.inf)
    sel_e, sel_v = [], []
    for _ in range(TOP_K):
        m = jnp.max(work, axis=-1, keepdims=True)
        e = jnp.min(jnp.where(work == m, lane, LANES), axis=-1, keepdims=True)
        sel_e.append(e)
        sel_v.append(m)
        work = jnp.where(lane == e, -jnp.inf, work)
    ex = [jnp.exp(v - sel_v[0]) for v in sel_v]
    den = ex[0] + ex[1] + ex[2] + ex[3]

    ridx = jnp.zeros((tm, LANES), jnp.int32)
    rw = jnp.zeros((tm, LANES), F32)
    onehot = jnp.zeros((tm, LANES), F32)
    for k in range(TOP_K):
        onehot = onehot + jnp.where(lane == sel_e[k], 1.0, 0.0)
        ridx = jnp.where(lane == k, sel_e[k], ridx)
        rw = jnp.where(lane == k, ex[k] / den, rw)
    ridx_ref[...] = ridx
    rw_ref[...] = rw
    carry = carry_ref[0:1, :] + jnp.sum(onehot, axis=0, keepdims=True)
    carry_ref[...] = jnp.broadcast_to(carry, carry_ref.shape)
    cnt_ref[...] = jnp.broadcast_to(carry, cnt_ref.shape)


def _out_router(ya, yb, x2d, mod3, w_out, norm2_w, wr_hi, wr_lo, br_pad, seq):
    n = x2d.shape[0]
    tm = TM_ROUTER
    tpb = seq // tm
    tok = lambda i: (i, 0)
    const2 = lambda i: (0, 0)
    return pl.pallas_call(
        _out_router_kernel,
        grid=(n // tm,),
        in_specs=[
            pl.BlockSpec((tm, GM_WIDTH), tok),
            pl.BlockSpec((tm, GDN_WIDTH), tok),
            pl.BlockSpec((tm, D_MODEL), tok),
            pl.BlockSpec((1, N_MOD, D_MODEL), lambda i: (i // tpb, 0, 0)),
            pl.BlockSpec((GM_WIDTH + GDN_WIDTH, D_MODEL), const2),
            pl.BlockSpec((1, D_MODEL), const2),
            pl.BlockSpec((D_MODEL, LANES), const2),
            pl.BlockSpec((D_MODEL, LANES), const2),
            pl.BlockSpec((1, LANES), const2),
        ],
        out_specs=[
            pl.BlockSpec((tm, D_MODEL), tok),
            pl.BlockSpec((tm * ROW_TILE, LANES), tok),
            pl.BlockSpec((tm, LANES), tok),
            pl.BlockSpec((tm, LANES), tok),
            pl.BlockSpec((SUBLANES, LANES), const2),
        ],
        out_shape=[
            jax.ShapeDtypeStruct((n, D_MODEL), F32),
            jax.ShapeDtypeStruct((n * ROW_TILE, LANES), F32),
            jax.ShapeDtypeStruct((n, LANES), jnp.int32),
            jax.ShapeDtypeStruct((n, LANES), F32),
            jax.ShapeDtypeStruct((SUBLANES, LANES), F32),
        ],
        scratch_shapes=[pltpu.VMEM((SUBLANES, LANES), F32)],
        compiler_params=pltpu.CompilerParams(
            dimension_semantics=("arbitrary",), vmem_limit_bytes=VMEM_LIMIT),
        name="out_router",
    )(ya, yb, x2d, mod3, w_out, norm2_w, wr_hi, wr_lo, br_pad)


def _moe_kernel(be_ref, nvb_ref, rows_hbm, h2_hbm, wgu_ref, bgu_ref, wd_ref, bd_ref,
                y4_hbm, idx_smem, xbuf, ynew, ysrc, wgu_b, wd_b, gsem, ssem, isem):
    t = MOE_BLOCK
    b = pl.program_id(0)
    last = nvb_ref[0] - 1

    ring_row = 2 * t
    rt = ROW_TILE

    def idx_copy(block):
        slot = (block + IDX_RING) % IDX_RING
        first = pl.multiple_of((block + 1) * ring_row, ring_row)
        return pltpu.make_async_copy(rows_hbm.at[pl.ds(first, ring_row)],
                                     idx_smem.at[pl.ds(slot * ring_row, ring_row)], isem.at[slot])

    def idx_base(block):
        return ((block + IDX_RING) % IDX_RING) * ring_row

    def tile_of(row):
        first = row * rt
        return pl.ds(first if isinstance(first, int) else pl.multiple_of(first, rt), rt)

    def gather_row(base, r, dst, sem):
        return pltpu.make_async_copy(h2_hbm.at[tile_of(idx_smem[base + r])], dst.at[tile_of(r)], sem)

    def scatter_row(base, r, src, sem):
        return pltpu.make_async_copy(src.at[tile_of(r)], y4_hbm.at[tile_of(idx_smem[base + t + r])],
                                     sem)

    def wait_gather(dst, sem):
        pltpu.make_async_copy(h2_hbm.at[pl.ds(0, t * rt)], dst, sem).wait()

    def wait_scatter(src, sem):
        pltpu.make_async_copy(src, y4_hbm.at[pl.ds(0, t * rt)], sem).wait()

    def load_rows(buf):
        return jnp.concatenate([buf[pl.ds(j, t, stride=rt), :] for j in range(rt)], axis=1)

    def store_rows(buf, val):
        for j in range(rt):
            buf[pl.ds(j, t, stride=rt), :] = val[:, j * LANES:(j + 1) * LANES]

    @pl.when(b == 0)
    def _():
        idx_copy(-1).start()
        idx_copy(0).start()
        idx_copy(1).start()
        idx_copy(-1).wait()
        idx_copy(0).wait()
        ynew[...] = jnp.zeros(ynew.shape, F32)
        dump = pltpu.make_async_copy(ynew, y4_hbm.at[pl.ds(y4_hbm.shape[0] - t * rt, t * rt)], ssem)
        dump.start()
        dump.wait()
        base0 = idx_base(0)

        def body(r, carry):
            gather_row(base0, r, xbuf, gsem).start()
            return carry
        lax.fori_loop(0, t, body, 0)

    @pl.when(b <= last)
    def _():
        @pl.when((b == 0) | (be_ref[b] != be_ref[jnp.maximum(b - 1, 0)]))
        def _():
            wgu_b[...] = wgu_ref[0].astype(BF16)
            wd_b[...] = wd_ref[0].astype(BF16)

        idx_copy(b + 2).start()
        idx_copy(b + 1).wait()
        wait_gather(xbuf, gsem)
        xbf = load_rows(xbuf).astype(BF16)

        gbase = idx_base(b + 1)
        sbase = idx_base(b - 1)

        def issue_gathers(lo, hi):
            for r in range(lo, hi):
                gather_row(gbase, r, xbuf, gsem).start(priority=0)

        def issue_scatters(lo, hi):
            for r in range(lo, hi):
                scatter_row(sbase, r, ysrc, ssem).start(priority=1)

        issue_gathers(0, t)
        gate = _dot(xbf, wgu_b[:, :D_FF]) + bgu_ref[0, :, :D_FF]
        up = _dot(xbf, wgu_b[:, D_FF:]) + bgu_ref[0, :, D_FF:]
        gate = jnp.minimum(gate, SWIGLU_LIMIT)
        up = jnp.clip(up, -SWIGLU_LIMIT, SWIGLU_LIMIT)
        act = (gate * jax.nn.sigmoid(SWIGLU_ALPHA * gate) * (up + 1.0)).astype(BF16)

        @pl.when(b >= 1)
        def _():
            wait_scatter(ysrc, ssem)
        ysrc[...] = ynew[...]

        issue_scatters(0, t)
        store_rows(ynew, _dot(act, wd_b[...]) + bd_ref[0])

        @pl.when(b == last)
        def _():
            wait_scatter(ysrc, ssem)
            base_b = idx_base(b)

            def body(r, carry):
                scatter_row(base_b, r, ynew, ssem).start()
                return carry
            lax.fori_loop(0, t, body, 0)
            wait_scatter(ynew, ssem)
            wait_gather(xbuf, gsem)
            idx_copy(b + 2).wait()


def _moe_ffn(block_e, nvb, block_rows, h2, w_gu, b_gu, w_down, b_down, n_blocks, n_rows_out):
    t = MOE_BLOCK

    def w_map(b, be, nv):
        return (be[b], 0, 0)

    grid_spec = pltpu.PrefetchScalarGridSpec(
        num_scalar_prefetch=2,
        grid=(n_blocks,),
        in_specs=[
            pl.BlockSpec(memory_space=pl.ANY),
            pl.BlockSpec(memory_space=pl.ANY),
            pl.BlockSpec((1, D_MODEL, 2 * D_FF), w_map),
            pl.BlockSpec((1, 1, 2 * D_FF), w_map),
            pl.BlockSpec((1, D_FF, D_MODEL), w_map),
            pl.BlockSpec((1, 1, D_MODEL), w_map),
        ],
        out_specs=pl.BlockSpec(memory_space=pl.ANY),
        scratch_shapes=[
            pltpu.SMEM((IDX_RING * 2 * t,), jnp.int32),
            pltpu.VMEM((t * ROW_TILE, LANES), F32),
            pltpu.VMEM((t * ROW_TILE, LANES), F32),
            pltpu.VMEM((t * ROW_TILE, LANES), F32),
            pltpu.VMEM((D_MODEL, 2 * D_FF), BF16),
            pltpu.VMEM((D_FF, D_MODEL), BF16),
            pltpu.SemaphoreType.DMA(()),
            pltpu.SemaphoreType.DMA(()),
            pltpu.SemaphoreType.DMA((IDX_RING,)),
        ],
    )
    return pl.pallas_call(
        _moe_kernel,
        grid_spec=grid_spec,
        out_shape=jax.ShapeDtypeStruct((n_rows_out * ROW_TILE, LANES), F32),
        compiler_params=pltpu.CompilerParams(
            dimension_semantics=("arbitrary",), vmem_limit_bytes=MOE_VMEM_LIMIT),
        name="moe_ffn",
    )(block_e, nvb, block_rows, h2, w_gu, b_gu, w_down, b_down)


def _combine_kernel(ya_ref, yb_ref, yc_ref, yd_ref, x1_ref, rw_ref, mod_ref, nfw_ref, out_ref):
    tm = TM_COMBINE
    rw = rw_ref[...]
    cols = []
    for j in range(ROW_TILE):
        rows = pl.ds(j, tm, stride=ROW_TILE)
        col = rw[:, 0:1] * ya_ref[rows, :]
        for k, y_ref in enumerate((yb_ref, yc_ref, yd_ref), start=1):
            col = col + rw[:, k:k + 1] * y_ref[rows, :]
        cols.append(col)
    x2 = x1_ref[...] + mod_ref[0, 5:6, :] * jnp.concatenate(cols, axis=1)
    out_ref[...] = _rms(x2, nfw_ref[...])


def _combine(y4, x1, rw, mod3, norm_f_w, seq):
    n = x1.shape[0]
    tm = TM_COMBINE
    tpb = seq // tm
    tok = lambda i: (i, 0)
    y_specs = [pl.BlockSpec((tm * ROW_TILE, LANES),
                            functools.partial(lambda i, k: (k * (n // tm) + i, 0), k=k))
               for k in range(TOP_K)]
    return pl.pallas_call(
        _combine_kernel,
        grid=(n // tm,),
        in_specs=y_specs + [
            pl.BlockSpec((tm, D_MODEL), tok),
            pl.BlockSpec((tm, LANES), tok),
            pl.BlockSpec((1, N_MOD, D_MODEL), lambda i: (i // tpb, 0, 0)),
            pl.BlockSpec((1, D_MODEL), lambda i: (0, 0)),
        ],
        out_specs=pl.BlockSpec((tm, D_MODEL), tok),
        out_shape=jax.ShapeDtypeStruct((n, D_MODEL), F32),
        compiler_params=pltpu.CompilerParams(
            dimension_semantics=("parallel",), vmem_limit_bytes=VMEM_LIMIT),
        name="combine",
    )(y4, y4, y4, y4, x1, rw, mod3, norm_f_w)


def _pad_lanes(v, fill=0.0):
    out = jnp.full((1, LANES), fill, F32)
    return out.at[0, :v.shape[0]].set(v.astype(F32))


def kernel(x, c, w_ada, b_ada, norm1_w, w_in, gm_vnorm_w, gm_w_spatial, gm_b_spatial, gdn_conv_w,
           gdn_a_log, gdn_dt_bias, gdn_onorm_w, w_out, norm2_w, w_router, b_router, w_gu, b_gu,
           w_down, b_down, norm_f_w):
    bsz, seq, d = x.shape
    n = bsz * seq
    assert w_ada.shape[0] == 1, "the closing RMSNorm is fused into the single layer's combine call"
    l = 0
    x2d = x.reshape(n, d)
    c_pad = jnp.zeros((SUBLANES, d), F32).at[:bsz].set(c)

    mod = _adaln(c_pad, w_ada[l], b_ada[l][None, :])[:bsz]
    mod3 = mod.reshape(bsz, N_MOD, d)

    w_main = w_in[l][:, :PROJ_MAIN].astype(BF16)
    w_ab = jnp.zeros((d, LANES), BF16).at[:, :2 * GDN_HEADS].set(
        w_in[l][:, PROJ_MAIN:].astype(BF16))
    proj, ab = _inproj(x2d, mod3, norm1_w[l][None, :], w_main, w_ab, seq)

    ya, qt, kt, wmat, umat, qk, dec = _mix_prep(
        proj, ab, gm_vnorm_w[l], gm_w_spatial[l], gm_b_spatial[l].T, gdn_conv_w[l],
        _pad_lanes(gdn_a_log[l]), _pad_lanes(gdn_dt_bias[l]), seq)
    yb = _gdn_scan(qt, kt, wmat, umat, qk, dec, proj, gdn_onorm_w[l][None, :], bsz, seq)

    wr_pad = jnp.zeros((d, LANES), F32).at[:, :N_EXPERTS].set(w_router[l])
    wr_hi = wr_pad.astype(BF16)
    wr_lo = (wr_pad - wr_hi.astype(F32)).astype(BF16)
    x1, h2, ridx, rw, cnt = _out_router(
        ya, yb, x2d, mod3, w_out[l].astype(BF16), norm2_w[l][None, :], wr_hi, wr_lo,
        _pad_lanes(b_router[l]), seq)

    t = MOE_BLOCK
    n_assign = n * TOP_K
    n_blocks = n_assign // t + N_EXPERTS
    e_flat = ridx[:, :TOP_K].T.reshape(n_assign)
    keys = e_flat * n_assign + jnp.arange(n_assign, dtype=jnp.int32)
    asg = jnp.sort(keys) % n_assign
    counts = cnt[0, :N_EXPERTS].astype(jnp.int32)
    start = jnp.cumsum(counts) - counts
    nblk = (counts + t - 1) // t
    blk_end = jnp.cumsum(nblk)
    blk_start = blk_end - nblk
    nvb = blk_end[-1]
    bid = jnp.arange(-1, n_blocks + 2, dtype=jnp.int32)
    live = (bid >= 0) & (bid < nvb)
    e_of = jnp.minimum(jnp.sum(blk_end[None, :] <= jnp.clip(bid, 0, nvb - 1)[:, None], axis=1),
                       N_EXPERTS - 1).astype(jnp.int32)
    j_of = bid - blk_start[e_of]
    row_start = jnp.where(live, start[e_of] + j_of * t, 0).astype(jnp.int32)
    n_valid = jnp.where(live, jnp.clip(counts[e_of] - j_of * t, 0, t), 0).astype(jnp.int32)
    block_e = e_of[1:n_blocks + 1]
    lane = jnp.arange(t, dtype=jnp.int32)
    a_blk = asg[jnp.minimum(row_start[:, None] + lane[None, :], n_assign - 1)]
    dst_rows = jnp.where(lane[None, :] < n_valid[:, None], a_blk, n_assign + lane[None, :])
    block_rows = jnp.concatenate([a_blk % n, dst_rows], axis=1).reshape(-1)

    y4 = _moe_ffn(block_e, nvb.reshape(1).astype(jnp.int32), block_rows, h2, w_gu[l],
                  b_gu[l][:, None, :], w_down[l], b_down[l][:, None, :], n_blocks, n_assign + t)
    out = _combine(y4, x1, rw, mod3, norm_f_w[None, :], seq)
    return out.reshape(bsz, seq, d)
```

```python
import functools

import jax
import jax.numpy as jnp
from jax import lax
from jax.experimental import pallas as pl
from jax.experimental.pallas import tpu as pltpu

F32 = jnp.float32
BF16 = jnp.bfloat16
HIGHEST = lax.Precision.HIGHEST

D_MODEL = 1024
GM_GROUPS = 4
GM_DIM = 128
GM_WIDTH = GM_GROUPS * GM_DIM
GM_CHUNK = 128
GDN_HEADS = 4
GDN_DK = 128
GDN_DV = 128
GDN_WIDTH = GDN_HEADS * GDN_DK
GDN_CONV = 4
GDN_CHUNK = 64
GDN_PAIR = 2 * GDN_CHUNK
N_EXPERTS = 32
TOP_K = 4
D_FF = D_MODEL
SWIGLU_LIMIT = 7.0
SWIGLU_ALPHA = 1.702
N_MOD = 6
EPS = 1e-6

LANES = 128
SUBLANES = 8
PROJ_MAIN = 2 * GM_WIDTH + 4 * GDN_WIDTH

TM_INPROJ = 512
TM_PREP = 256
TM_SCAN = 256
SCAN_BATCH = 4
TM_ROUTER = 1024
MOE_BLOCK = 512
TM_COMBINE = 512
ROW_TILE = D_MODEL // LANES
IDX_RING = 4
VMEM_LIMIT = 48 * 1024 * 1024
MOE_VMEM_LIMIT = 56 * 1024 * 1024


def _dot(a, b):
    return jnp.dot(a, b, preferred_element_type=F32)


def _dot_nt(a, b):
    return lax.dot_general(a, b, (((1,), (1,)), ((), ())), preferred_element_type=F32)


def _dot_tn(a, b):
    return lax.dot_general(a, b, (((0,), (0,)), ((), ())), preferred_element_type=F32)


def _split3(x):
    hi = x.astype(BF16)
    r1 = x - hi.astype(F32)
    mid = r1.astype(BF16)
    lo = (r1 - mid.astype(F32)).astype(BF16)
    return hi, mid, lo


def _rms(x, w):
    return x * lax.rsqrt(jnp.mean(x * x, axis=-1, keepdims=True) + EPS) * w


def _gelu(x):
    return 0.5 * x * (1.0 + lax.erf(x * (2.0 ** -0.5)))


def _silu(x):
    return x * jax.nn.sigmoid(x)


def _adaln_kernel(c_ref, w_ref, b_ref, o_ref):
    c = c_ref[...]
    o_ref[...] = jnp.dot(_silu(c), w_ref[...], precision=HIGHEST,
                         preferred_element_type=F32) + b_ref[...]


def _adaln(c_pad, w_ada, b_ada):
    rows = c_pad.shape[0]
    n_out = w_ada.shape[1]
    return pl.pallas_call(
        _adaln_kernel,
        grid=(n_out // D_MODEL,),
        in_specs=[
            pl.BlockSpec((rows, D_MODEL), lambda j: (0, 0)),
            pl.BlockSpec((D_MODEL, D_MODEL), lambda j: (0, j)),
            pl.BlockSpec((1, D_MODEL), lambda j: (0, j)),
        ],
        out_specs=pl.BlockSpec((rows, D_MODEL), lambda j: (0, j)),
        out_shape=jax.ShapeDtypeStruct((rows, n_out), F32),
        name="adaln",
    )(c_pad, w_ada, b_ada)


def _inproj_kernel(x_ref, mod_ref, nw_ref, w_ref, wab_ref, proj_ref, ab_ref):
    h = _rms(x_ref[...], nw_ref[...]) * (1.0 + mod_ref[0, 1:2, :]) + mod_ref[0, 0:1, :]
    hb = h.astype(BF16)
    for j in range(PROJ_MAIN // 512):
        cols = slice(j * 512, (j + 1) * 512)
        proj_ref[:, cols] = _dot(hb, w_ref[:, cols]).astype(BF16)
    ab_ref[...] = _dot(hb, wab_ref[...])


def _inproj(x2d, mod3, norm_w, w_main, w_ab, seq):
    n = x2d.shape[0]
    tiles_per_batch = seq // TM_INPROJ
    return pl.pallas_call(
        _inproj_kernel,
        grid=(n // TM_INPROJ,),
        in_specs=[
            pl.BlockSpec((TM_INPROJ, D_MODEL), lambda i: (i, 0)),
            pl.BlockSpec((1, N_MOD, D_MODEL), lambda i: (i // tiles_per_batch, 0, 0)),
            pl.BlockSpec((1, D_MODEL), lambda i: (0, 0)),
            pl.BlockSpec((D_MODEL, PROJ_MAIN), lambda i: (0, 0)),
            pl.BlockSpec((D_MODEL, LANES), lambda i: (0, 0)),
        ],
        out_specs=[
            pl.BlockSpec((TM_INPROJ, PROJ_MAIN), lambda i: (i, 0)),
            pl.BlockSpec((TM_INPROJ, LANES), lambda i: (i, 0)),
        ],
        out_shape=[
            jax.ShapeDtypeStruct((n, PROJ_MAIN), BF16),
            jax.ShapeDtypeStruct((n, LANES), F32),
        ],
        compiler_params=pltpu.CompilerParams(
            dimension_semantics=("parallel",), vmem_limit_bytes=VMEM_LIMIT),
        name="in_proj",
    )(x2d, mod3, norm_w, w_main, w_ab)


def _unit_lower_inverses(a_list):
    c = a_list[0].shape[0]
    row = lax.broadcasted_iota(jnp.int32, (c, c), 0)
    col = lax.broadcasted_iota(jnp.int32, (c, c), 1)
    eye = jnp.where(row == col, 1.0, 0.0).astype(F32)
    ps = [eye - a for a in a_list]
    qs = [a.astype(BF16) for a in a_list]
    qs = [_dot(q, q) for q in qs]
    power = 2
    while 2 * power < GDN_CHUNK:
        qbs = [q.astype(BF16) for q in qs]
        ps = [p + _dot(p.astype(BF16), qb) for p, qb in zip(ps, qbs)]
        qs = [_dot(qb, qb) for qb in qbs]
        power *= 2
    return [p + _dot(p.astype(BF16), q.astype(BF16)) for p, q in zip(ps, qs)]


def _mix_prep_kernel(proj_ref, ab_ref, vnw_ref, wsp_ref, bsp_ref, cw_ref, shift_ref, alog_ref,
                     dtb_ref, ya_ref, qt_ref, kt_ref, w_ref, u_ref, qk_ref, dec_ref, ext_ref,
                     *, tiles_per_batch):
    tm = TM_PREP
    i = pl.program_id(0)

    row = lax.broadcasted_iota(jnp.int32, (GM_CHUNK, GM_CHUNK), 0)
    col = lax.broadcasted_iota(jnp.int32, (GM_CHUNK, GM_CHUNK), 1)
    causal = row >= col
    for g in range(GM_GROUPS):
        ws = jnp.where(causal, wsp_ref[g], 0.0).astype(BF16)
        bcol = bsp_ref[:, g:g + 1]
        cols_u = slice(g * GM_DIM, (g + 1) * GM_DIM)
        cols_v = slice(GM_WIDTH + g * GM_DIM, GM_WIDTH + (g + 1) * GM_DIM)
        for c in range(tm // GM_CHUNK):
            rows = slice(c * GM_CHUNK, (c + 1) * GM_CHUNK)
            u = _gelu(proj_ref[rows, cols_u].astype(F32))
            v = _rms(_gelu(proj_ref[rows, cols_v].astype(F32)), vnw_ref[g:g + 1, :])
            z = _dot(ws, v.astype(BF16)) + bcol
            ya_ref[rows, cols_u] = (u * z).astype(BF16)

    @pl.when(i % tiles_per_batch == 0)
    def _():
        ext_ref[0:SUBLANES, :] = jnp.zeros((SUBLANES, 3 * GDN_WIDTH), F32)

    qkv_cols = slice(2 * GM_WIDTH, 2 * GM_WIDTH + 3 * GDN_WIDTH)
    xq = proj_ref[:, qkv_cols]
    xf = xq.astype(F32)
    conv = jnp.zeros((tm, 3 * GDN_WIDTH), F32)
    for j in range(GDN_CONV - 1):
        conv = conv + cw_ref[j:j + 1, :] * _dot(shift_ref[j], xq)
    conv = conv + cw_ref[GDN_CONV - 1:GDN_CONV, :] * xf
    ext_ref[SUBLANES:2 * SUBLANES, :] = xf[0:SUBLANES]
    head = jnp.zeros((SUBLANES, 3 * GDN_WIDTH), F32)
    for j in range(GDN_CONV):
        start = SUBLANES - (GDN_CONV - 1) + j
        head = head + cw_ref[j:j + 1, :] * ext_ref[start:start + SUBLANES, :]
    conv = jnp.concatenate([head, conv[SUBLANES:]], axis=0)
    ext_ref[0:SUBLANES, :] = xf[tm - SUBLANES:tm]
    act = _silu(conv)

    ab = ab_ref[...]
    sp_in = ab + dtb_ref[...]
    g_all = -jnp.exp(alog_ref[...]) * (
        jnp.maximum(sp_in, 0.0) + jnp.log1p(jnp.exp(-jnp.abs(sp_in))))
    beta_all = jax.nn.sigmoid(ab)
    trow = lax.broadcasted_iota(jnp.int32, (tm, tm), 0)
    tcol = lax.broadcasted_iota(jnp.int32, (tm, tm), 1)
    blk_lower = jnp.where((trow >= tcol) & (trow // GDN_CHUNK == tcol // GDN_CHUNK),
                          1.0, 0.0).astype(BF16)
    g_hi, g_mid, g_lo = _split3(g_all)
    gc_all = _dot(blk_lower, g_hi) + _dot(blk_lower, g_mid) + _dot(blk_lower, g_lo)
    gc_all_t = gc_all.T

    prow = lax.broadcasted_iota(jnp.int32, (GDN_PAIR, GDN_PAIR), 0)
    pcol = lax.broadcasted_iota(jnp.int32, (GDN_PAIR, GDN_PAIR), 1)
    same_chunk = prow // GDN_CHUNK == pcol // GDN_CHUNK
    tri = (prow >= pcol) & same_chunk
    strict = (prow > pcol) & same_chunk
    first_half = lax.broadcasted_iota(jnp.int32, (GDN_PAIR, 1), 0) < GDN_CHUNK

    blocks = [(h, p) for h in range(GDN_HEADS) for p in range(tm // GDN_PAIR)]
    q_l, k_l, kb_l, kbf_l, beta_l, gcc_l, decay_l, v_l = [], [], [], [], [], [], [], []
    for h in range(GDN_HEADS):
        hq = slice(h * GDN_DK, (h + 1) * GDN_DK)
        hk = slice(GDN_WIDTH + h * GDN_DK, GDN_WIDTH + (h + 1) * GDN_DK)
        hv = slice(2 * GDN_WIDTH + h * GDN_DV, 2 * GDN_WIDTH + (h + 1) * GDN_DV)
        q_h = act[:, hq]
        k_h = act[:, hk]
        q_h = q_h * lax.rsqrt(jnp.sum(q_h * q_h, axis=-1, keepdims=True) + EPS) * (GDN_DK ** -0.5)
        k_h = k_h * lax.rsqrt(jnp.sum(k_h * k_h, axis=-1, keepdims=True) + EPS)
        v_h = act[:, hv]
        for p in range(tm // GDN_PAIR):
            rows = slice(p * GDN_PAIR, (p + 1) * GDN_PAIR)
            beta = beta_all[rows, GDN_HEADS + h:GDN_HEADS + h + 1]
            gcc = gc_all[rows, h:h + 1]
            gcr = gc_all_t[h:h + 1, p * GDN_PAIR:(p + 1) * GDN_PAIR]
            k = k_h[rows]
            q_l.append(q_h[rows])
            k_l.append(k)
            kb_l.append(k * beta)
            kbf_l.append(k.astype(BF16))
            beta_l.append(beta)
            gcc_l.append(gcc)
            v_l.append(v_h[rows])
            decay_l.append(jnp.where(tri, jnp.exp(jnp.where(tri, gcc - gcr, 0.0)), 0.0))

    kk_l = [_dot_nt(kb.astype(BF16), kbf) for kb, kbf in zip(kb_l, kbf_l)]
    a_l = [jnp.where(strict, kk * decay, 0.0) for kk, decay in zip(kk_l, decay_l)]
    t_l = _unit_lower_inverses(a_l)
    egc_l = [jnp.exp(gcc) for gcc in gcc_l]
    rhs_l = [jnp.concatenate([v * beta, kb * egc], axis=1).astype(BF16)
             for v, beta, kb, egc in zip(v_l, beta_l, kb_l, egc_l)]
    sol_l = [_dot(t.astype(BF16), rhs) for t, rhs in zip(t_l, rhs_l)]
    qk_l = [jnp.where(tri, _dot_nt(q.astype(BF16), kbf) * decay, 0.0)
            for q, kbf, decay in zip(q_l, kbf_l, decay_l)]

    for idx, (h, p) in enumerate(blocks):
        rows = slice(p * GDN_PAIR, (p + 1) * GDN_PAIR)
        cols = slice(h * GDN_DK, (h + 1) * GDN_DK)
        gcc = gcc_l[idx]
        gl0 = gcc[GDN_CHUNK - 1:GDN_CHUNK]
        gl1 = gcc[GDN_PAIR - 1:GDN_PAIR]
        g_last = jnp.where(first_half, gl0, gl1)
        u_ref[rows, cols] = sol_l[idx][:, :GDN_DV]
        w_ref[rows, cols] = sol_l[idx][:, GDN_DV:].astype(BF16)
        qt_ref[rows, cols] = (q_l[idx] * egc_l[idx]).astype(BF16)
        kt_ref[rows, cols] = (k_l[idx] * jnp.exp(g_last - gcc)).astype(BF16)
        qk_ref[rows, cols] = qk_l[idx].astype(BF16)
        dec_ref[2 * p, h:h + 1, :] = jnp.broadcast_to(jnp.exp(gl0), (1, LANES))
        dec_ref[2 * p + 1, h:h + 1, :] = jnp.broadcast_to(jnp.exp(gl1), (1, LANES))


def _mix_prep(proj, ab, vnorm_w, w_spatial, b_spatial_t, conv_w, alog_pad, dtb_pad, seq):
    n = proj.shape[0]
    tm = TM_PREP
    tiles_per_batch = seq // tm
    const2 = lambda i: (0, 0)
    shifts = jnp.stack([jnp.eye(tm, k=-(GDN_CONV - 1 - j), dtype=BF16)
                        for j in range(GDN_CONV - 1)])
    return pl.pallas_call(
        functools.partial(_mix_prep_kernel, tiles_per_batch=tiles_per_batch),
        grid=(n // tm,),
        in_specs=[
            pl.BlockSpec((tm, PROJ_MAIN), lambda i: (i, 0)),
            pl.BlockSpec((tm, LANES), lambda i: (i, 0)),
            pl.BlockSpec((GM_GROUPS, GM_DIM), const2),
            pl.BlockSpec((GM_GROUPS, GM_CHUNK, GM_CHUNK), lambda i: (0, 0, 0)),
            pl.BlockSpec((GM_CHUNK, GM_GROUPS), const2),
            pl.BlockSpec((GDN_CONV, 3 * GDN_WIDTH), const2),
            pl.BlockSpec((GDN_CONV - 1, tm, tm), lambda i: (0, 0, 0)),
            pl.BlockSpec((1, LANES), const2),
            pl.BlockSpec((1, LANES), const2),
        ],
        out_specs=[
            pl.BlockSpec((tm, GM_WIDTH), lambda i: (i, 0)),
            pl.BlockSpec((tm, GDN_WIDTH), lambda i: (i, 0)),
            pl.BlockSpec((tm, GDN_WIDTH), lambda i: (i, 0)),
            pl.BlockSpec((tm, GDN_WIDTH), lambda i: (i, 0)),
            pl.BlockSpec((tm, GDN_WIDTH), lambda i: (i, 0)),
            pl.BlockSpec((tm, GDN_WIDTH), lambda i: (i, 0)),
            pl.BlockSpec((tm // GDN_CHUNK, GDN_HEADS, LANES), lambda i: (i, 0, 0)),
        ],
        out_shape=[
            jax.ShapeDtypeStruct((n, GM_WIDTH), BF16),
            jax.ShapeDtypeStruct((n, GDN_WIDTH), BF16),
            jax.ShapeDtypeStruct((n, GDN_WIDTH), BF16),
            jax.ShapeDtypeStruct((n, GDN_WIDTH), BF16),
            jax.ShapeDtypeStruct((n, GDN_WIDTH), F32),
            jax.ShapeDtypeStruct((n, GDN_WIDTH), BF16),
            jax.ShapeDtypeStruct((n // GDN_CHUNK, GDN_HEADS, LANES), F32),
        ],
        scratch_shapes=[pltpu.VMEM((2 * SUBLANES, 3 * GDN_WIDTH), F32)],
        compiler_params=pltpu.CompilerParams(
            dimension_semantics=("arbitrary",), vmem_limit_bytes=VMEM_LIMIT),
        name="mix_prep",
    )(proj, ab, vnorm_w, w_spatial, b_spatial_t, conv_w, shifts, alog_pad, dtb_pad)


def _gdn_scan_kernel(qt_ref, kt_ref, w_ref, u_ref, qk_ref, dec_ref, z_ref, onw_ref,
                     yb_ref, s_ref):
    @pl.when(pl.program_id(1) == 0)
    def _():
        s_ref[...] = jnp.zeros(s_ref.shape, F32)

    chains = [(i, h) for i in range(SCAN_BATCH) for h in range(GDN_HEADS)]
    hcols = [slice(h * GDN_DK, (h + 1) * GDN_DK) for h in range(GDN_HEADS)]
    states = [s_ref[i, h] for i, h in chains]
    v_prev = [None] * len(chains)
    for c in range(TM_SCAN // GDN_CHUNK):
        rows = slice(c * GDN_CHUNK, (c + 1) * GDN_CHUNK)
        states_b = [s.astype(BF16) for s in states]
        ws = [_dot(w_ref[i, rows, hcols[h]], sb) for (i, h), sb in zip(chains, states_b)]
        qs = [_dot(qt_ref[i, rows, hcols[h]], sb) for (i, h), sb in zip(chains, states_b)]
        v_new = [(u_ref[i, rows, hcols[h]] - wsn).astype(BF16) for (i, h), wsn in zip(chains, ws)]
        if c % 2 == 0:
            o = [q + _dot(qk_ref[i, rows, h * GDN_DK:h * GDN_DK + GDN_CHUNK], vn)
                 for (i, h), q, vn in zip(chains, qs, v_new)]
        else:
            o = [q + _dot(qk_ref[i, rows, hcols[h]], jnp.concatenate([vp, vn], axis=0))
                 for (i, h), q, vp, vn in zip(chains, qs, v_prev, v_new)]
        states = [s * dec_ref[i, c, h:h + 1, :] + _dot_tn(kt_ref[i, rows, hcols[h]], vn)
                  for (i, h), s, vn in zip(chains, states, v_new)]
        v_prev = v_new
        for (i, h), on in zip(chains, o):
            zz = z_ref[i, rows, hcols[h]].astype(F32)
            yb_ref[i, rows, hcols[h]] = (_rms(on, onw_ref[...]) * _silu(zz)).astype(BF16)
    for (i, h), s in zip(chains, states):
        s_ref[i, h] = s


def _gdn_scan(qt, kt, w, u, qk, dec, proj, onorm_w, bsz, seq):
    n = qt.shape[0]
    tm = TM_SCAN
    nb = SCAN_BATCH
    assert bsz % nb == 0
    as3d = lambda a: a.reshape(bsz, seq, a.shape[-1])
    tok = lambda b, j: (b, j, 0)
    z_block = (2 * GM_WIDTH + 3 * GDN_WIDTH) // GDN_WIDTH
    wide = pl.BlockSpec((nb, tm, GDN_WIDTH), tok)
    out = pl.pallas_call(
        _gdn_scan_kernel,
        grid=(bsz // nb, seq // tm),
        in_specs=[
            wide, wide, wide, wide, wide,
            pl.BlockSpec((nb, tm // GDN_CHUNK, GDN_HEADS, LANES), lambda b, j: (b, j, 0, 0)),
            pl.BlockSpec((nb, tm, GDN_WIDTH), lambda b, j: (b, j, z_block)),
            pl.BlockSpec((1, GDN_DV), lambda b, j: (0, 0)),
        ],
        out_specs=wide,
        out_shape=jax.ShapeDtypeStruct((bsz, seq, GDN_WIDTH), BF16),
        scratch_shapes=[pltpu.VMEM((nb, GDN_HEADS, GDN_DK, GDN_DV), F32)],
        compiler_params=pltpu.CompilerParams(dimension_semantics=("arbitrary", "arbitrary")),
        name="gdn_scan",
    )(as3d(qt), as3d(kt), as3d(w), as3d(u), as3d(qk),
      dec.reshape(bsz, seq // GDN_CHUNK, GDN_HEADS, LANES), as3d(proj), onorm_w)
    return out.reshape(n, GDN_WIDTH)


def _out_router_kernel(ya_ref, yb_ref, x_ref, mod_ref, wo_ref, n2w_ref, wrh_ref, wrl_ref, br_ref,
                       x1_ref, h2_ref, ridx_ref, rw_ref, cnt_ref, carry_ref):
    tm = TM_ROUTER

    @pl.when(pl.program_id(0) == 0)
    def _():
        carry_ref[...] = jnp.zeros(carry_ref.shape, F32)

    mix = _dot(ya_ref[...], wo_ref[0:GM_WIDTH, :]) + _dot(yb_ref[...], wo_ref[GM_WIDTH:, :])
    x1 = x_ref[...] + mod_ref[0, 2:3, :] * mix
    x1_ref[...] = x1
    h2 = _rms(x1, n2w_ref[...]) * (1.0 + mod_ref[0, 4:5, :]) + mod_ref[0, 3:4, :]
    for j in range(ROW_TILE):
        h2_ref[pl.ds(j, tm, stride=ROW_TILE), :] = h2[:, j * LANES:(j + 1) * LANES]

    h_hi = h2.astype(BF16)
    h_lo = (h2 - h_hi.astype(F32)).astype(BF16)
    logits = (_dot(h_hi, wrh_ref[...]) + _dot(h_hi, wrl_ref[...]) + _dot(h_lo, wrh_ref[...])
              + br_ref[...])
    lane = lax.broadcasted_iota(jnp.int32, (tm, LANES), 1)
    work = jnp.where(lane < N_EXPERTS, logits, -jnp.inf)
    sel_e, sel_v = [], []
    for _ in range(TOP_K):
        m = jnp.max(work, axis=-1, keepdims=True)
        e = jnp.min(jnp.where(work == m, lane, LANES), axis=-1, keepdims=True)
        sel_e.append(e)
        sel_v.append(m)
        work = jnp.where(lane == e, -jnp.inf, work)
    ex = [jnp.exp(v - sel_v[0]) for v in sel_v]
    den = ex[0] + ex[1] + ex[2] + ex[3]

    ridx = jnp.zeros((tm, LANES), jnp.int32)
    rw = jnp.zeros((tm, LANES), F32)
    onehot = jnp.zeros((tm, LANES), F32)
    for k in range(TOP_K):
        onehot = onehot + jnp.where(lane == sel_e[k], 1.0, 0.0)
        ridx = jnp.where(lane == k, sel_e[k], ridx)
        rw = jnp.where(lane == k, ex[k] / den, rw)
    ridx_ref[...] = ridx
    rw_ref[...] = rw
    carry = carry_ref[0:1, :] + jnp.sum(onehot, axis=0, keepdims=True)
    carry_ref[...] = jnp.broadcast_to(carry, carry_ref.shape)
    cnt_ref[...] = jnp.broadcast_to(carry, cnt_ref.shape)


def _out_router(ya, yb, x2d, mod3, w_out, norm2_w, wr_hi, wr_lo, br_pad, seq):
    n = x2d.shape[0]
    tm = TM_ROUTER
    tpb = seq // tm
    tok = lambda i: (i, 0)
    const2 = lambda i: (0, 0)
    return pl.pallas_call(
        _out_router_kernel,
        grid=(n // tm,),
        in_specs=[
            pl.BlockSpec((tm, GM_WIDTH), tok),
            pl.BlockSpec((tm, GDN_WIDTH), tok),
            pl.BlockSpec((tm, D_MODEL), tok),
            pl.BlockSpec((1, N_MOD, D_MODEL), lambda i: (i // tpb, 0, 0)),
            pl.BlockSpec((GM_WIDTH + GDN_WIDTH, D_MODEL), const2),
            pl.BlockSpec((1, D_MODEL), const2),
            pl.BlockSpec((D_MODEL, LANES), const2),
            pl.BlockSpec((D_MODEL, LANES), const2),
            pl.BlockSpec((1, LANES), const2),
        ],
        out_specs=[
            pl.BlockSpec((tm, D_MODEL), tok),
            pl.BlockSpec((tm * ROW_TILE, LANES), tok),
            pl.BlockSpec((tm, LANES), tok),
            pl.BlockSpec((tm, LANES), tok),
            pl.BlockSpec((SUBLANES, LANES), const2),
        ],
        out_shape=[
            jax.ShapeDtypeStruct((n, D_MODEL), F32),
            jax.ShapeDtypeStruct((n * ROW_TILE, LANES), F32),
            jax.ShapeDtypeStruct((n, LANES), jnp.int32),
            jax.ShapeDtypeStruct((n, LANES), F32),
            jax.ShapeDtypeStruct((SUBLANES, LANES), F32),
        ],
        scratch_shapes=[pltpu.VMEM((SUBLANES, LANES), F32)],
        compiler_params=pltpu.CompilerParams(
            dimension_semantics=("arbitrary",), vmem_limit_bytes=VMEM_LIMIT),
        name="out_router",
    )(ya, yb, x2d, mod3, w_out, norm2_w, wr_hi, wr_lo, br_pad)


def _moe_kernel(be_ref, nvb_ref, rows_hbm, h2_hbm, wgu_ref, bgu_ref, wd_ref, bd_ref,
                y4_hbm, idx_smem, xbuf, ynew, ysrc, wgu_b, wd_b, gsem, ssem, isem):
    t = MOE_BLOCK
    b = pl.program_id(0)
    last = nvb_ref[0] - 1

    ring_row = 2 * t
    rt = ROW_TILE

    def idx_copy(block):
        slot = (block + IDX_RING) % IDX_RING
        first = pl.multiple_of((block + 1) * ring_row, ring_row)
        return pltpu.make_async_copy(rows_hbm.at[pl.ds(first, ring_row)],
                                     idx_smem.at[pl.ds(slot * ring_row, ring_row)], isem.at[slot])

    def idx_base(block):
        return ((block + IDX_RING) % IDX_RING) * ring_row

    def tile_of(row):
        first = row * rt
        return pl.ds(first if isinstance(first, int) else pl.multiple_of(first, rt), rt)

    def gather_row(base, r, dst, sem):
        return pltpu.make_async_copy(h2_hbm.at[tile_of(idx_smem[base + r])], dst.at[tile_of(r)], sem)

    def scatter_row(base, r, src, sem):
        return pltpu.make_async_copy(src.at[tile_of(r)], y4_hbm.at[tile_of(idx_smem[base + t + r])],
                                     sem)

    def wait_gather(dst, sem):
        pltpu.make_async_copy(h2_hbm.at[pl.ds(0, t * rt)], dst, sem).wait()

    def wait_scatter(src, sem):
        pltpu.make_async_copy(src, y4_hbm.at[pl.ds(0, t * rt)], sem).wait()

    def load_rows(buf):
        return jnp.concatenate([buf[pl.ds(j, t, stride=rt), :] for j in range(rt)], axis=1)

    def store_rows(buf, val):
        for j in range(rt):
            buf[pl.ds(j, t, stride=rt), :] = val[:, j * LANES:(j + 1) * LANES]

    @pl.when(b == 0)
    def _():
        idx_copy(-1).start()
        idx_copy(0).start()
        idx_copy(1).start()
        idx_copy(-1).wait()
        idx_copy(0).wait()
        ynew[...] = jnp.zeros(ynew.shape, F32)
        dump = pltpu.make_async_copy(ynew, y4_hbm.at[pl.ds(y4_hbm.shape[0] - t * rt, t * rt)], ssem)
        dump.start()
        dump.wait()
        base0 = idx_base(0)

        def body(r, carry):
            gather_row(base0, r, xbuf, gsem).start()
            return carry
        lax.fori_loop(0, t, body, 0)

    @pl.when(b <= last)
    def _():
        @pl.when((b == 0) | (be_ref[b] != be_ref[jnp.maximum(b - 1, 0)]))
        def _():
            wgu_b[...] = wgu_ref[0].astype(BF16)
            wd_b[...] = wd_ref[0].astype(BF16)

        idx_copy(b + 2).start()
        idx_copy(b + 1).wait()
        wait_gather(xbuf, gsem)
        xbf = load_rows(xbuf).astype(BF16)

        gbase = idx_base(b + 1)
        sbase = idx_base(b - 1)

        def issue_gathers(lo, hi):
            for r in range(lo, hi):
                gather_row(gbase, r, xbuf, gsem).start(priority=0)

        def issue_scatters(lo, hi):
            for r in range(lo, hi):
                scatter_row(sbase, r, ysrc, ssem).start(priority=1)

        issue_gathers(0, t)
        gate = _dot(xbf, wgu_b[:, :D_FF]) + bgu_ref[0, :, :D_FF]
        up = _dot(xbf, wgu_b[:, D_FF:]) + bgu_ref[0, :, D_FF:]
        gate = jnp.minimum(gate, SWIGLU_LIMIT)
        up = jnp.clip(up, -SWIGLU_LIMIT, SWIGLU_LIMIT)
        act = (gate * jax.nn.sigmoid(SWIGLU_ALPHA * gate) * (up + 1.0)).astype(BF16)

        @pl.when(b >= 1)
        def _():
            wait_scatter(ysrc, ssem)
        ysrc[...] = ynew[...]

        issue_scatters(0, t)
        store_rows(ynew, _dot(act, wd_b[...]) + bd_ref[0])

        @pl.when(b == last)
        def _():
            wait_scatter(ysrc, ssem)
            base_b = idx_base(b)

            def body(r, carry):
                scatter_row(base_b, r, ynew, ssem).start()
                return carry
            lax.fori_loop(0, t, body, 0)
            wait_scatter(ynew, ssem)
            wait_gather(xbuf, gsem)
            idx_copy(b + 2).wait()


def _moe_ffn(block_e, nvb, block_rows, h2, w_gu, b_gu, w_down, b_down, n_blocks, n_rows_out):
    t = MOE_BLOCK

    def w_map(b, be, nv):
        return (be[b], 0, 0)

    grid_spec = pltpu.PrefetchScalarGridSpec(
        num_scalar_prefetch=2,
        grid=(n_blocks,),
        in_specs=[
            pl.BlockSpec(memory_space=pl.ANY),
            pl.BlockSpec(memory_space=pl.ANY),
            pl.BlockSpec((1, D_MODEL, 2 * D_FF), w_map),
            pl.BlockSpec((1, 1, 2 * D_FF), w_map),
            pl.BlockSpec((1, D_FF, D_MODEL), w_map),
            pl.BlockSpec((1, 1, D_MODEL), w_map),
        ],
        out_specs=pl.BlockSpec(memory_space=pl.ANY),
        scratch_shapes=[
            pltpu.SMEM((IDX_RING * 2 * t,), jnp.int32),
            pltpu.VMEM((t * ROW_TILE, LANES), F32),
            pltpu.VMEM((t * ROW_TILE, LANES), F32),
            pltpu.VMEM((t * ROW_TILE, LANES), F32),
            pltpu.VMEM((D_MODEL, 2 * D_FF), BF16),
            pltpu.VMEM((D_FF, D_MODEL), BF16),
            pltpu.SemaphoreType.DMA(()),
            pltpu.SemaphoreType.DMA(()),
            pltpu.SemaphoreType.DMA((IDX_RING,)),
        ],
    )
    return pl.pallas_call(
        _moe_kernel,
        grid_spec=grid_spec,
        out_shape=jax.ShapeDtypeStruct((n_rows_out * ROW_TILE, LANES), F32),
        compiler_params=pltpu.CompilerParams(
            dimension_semantics=("arbitrary",), vmem_limit_bytes=MOE_VMEM_LIMIT),
        name="moe_ffn",
    )(block_e, nvb, block_rows, h2, w_gu, b_gu, w_down, b_down)


def _combine_kernel(ya_ref, yb_ref, yc_ref, yd_ref, x1_ref, rw_ref, mod_ref, nfw_ref, out_ref):
    tm = TM_COMBINE
    rw = rw_ref[...]
    cols = []
    for j in range(ROW_TILE):
        rows = pl.ds(j, tm, stride=ROW_TILE)
        col = rw[:, 0:1] * ya_ref[rows, :]
        for k, y_ref in enumerate((yb_ref, yc_ref, yd_ref), start=1):
            col = col + rw[:, k:k + 1] * y_ref[rows, :]
        cols.append(col)
    x2 = x1_ref[...] + mod_ref[0, 5:6, :] * jnp.concatenate(cols, axis=1)
    out_ref[...] = _rms(x2, nfw_ref[...])


def _combine(y4, x1, rw, mod3, norm_f_w, seq):
    n = x1.shape[0]
    tm = TM_COMBINE
    tpb = seq // tm
    tok = lambda i: (i, 0)
    y_specs = [pl.BlockSpec((tm * ROW_TILE, LANES),
                            functools.partial(lambda i, k: (k * (n // tm) + i, 0), k=k))
               for k in range(TOP_K)]
    return pl.pallas_call(
        _combine_kernel,
        grid=(n // tm,),
        in_specs=y_specs + [
            pl.BlockSpec((tm, D_MODEL), tok),
            pl.BlockSpec((tm, LANES), tok),
            pl.BlockSpec((1, N_MOD, D_MODEL), lambda i: (i // tpb, 0, 0)),
            pl.BlockSpec((1, D_MODEL), lambda i: (0, 0)),
        ],
        out_specs=pl.BlockSpec((tm, D_MODEL), tok),
        out_shape=jax.ShapeDtypeStruct((n, D_MODEL), F32),
        compiler_params=pltpu.CompilerParams(
            dimension_semantics=("parallel",), vmem_limit_bytes=VMEM_LIMIT),
        name="combine",
    )(y4, y4, y4, y4, x1, rw, mod3, norm_f_w)


def _pad_lanes(v, fill=0.0):
    out = jnp.full((1, LANES), fill, F32)
    return out.at[0, :v.shape[0]].set(v.astype(F32))


def kernel(x, c, w_ada, b_ada, norm1_w, w_in, gm_vnorm_w, gm_w_spatial, gm_b_spatial, gdn_conv_w,
           gdn_a_log, gdn_dt_bias, gdn_onorm_w, w_out, norm2_w, w_router, b_router, w_gu, b_gu,
           w_down, b_down, norm_f_w):
    bsz, seq, d = x.shape
    n = bsz * seq
    assert w_ada.shape[0] == 1, "the closing RMSNorm is fused into the single layer's combine call"
    l = 0
    x2d = x.reshape(n, d)
    c_pad = jnp.zeros((SUBLANES, d), F32).at[:bsz].set(c)

    mod = _adaln(c_pad, w_ada[l], b_ada[l][None, :])[:bsz]
    mod3 = mod.reshape(bsz, N_MOD, d)

    w_main = w_in[l][:, :PROJ_MAIN].astype(BF16)
    w_ab = jnp.zeros((d, LANES), BF16).at[:, :2 * GDN_HEADS].set(
        w_in[l][:, PROJ_MAIN:].astype(BF16))
    proj, ab = _inproj(x2d, mod3, norm1_w[l][None, :], w_main, w_ab, seq)

    ya, qt, kt, wmat, umat, qk, dec = _mix_prep(
        proj, ab, gm_vnorm_w[l], gm_w_spatial[l], gm_b_spatial[l].T, gdn_conv_w[l],
        _pad_lanes(gdn_a_log[l]), _pad_lanes(gdn_dt_bias[l]), seq)
    yb = _gdn_scan(qt, kt, wmat, umat, qk, dec, proj, gdn_onorm_w[l][None, :], bsz, seq)

    wr_pad = jnp.zeros((d, LANES), F32).at[:, :N_EXPERTS].set(w_router[l])
    wr_hi = wr_pad.astype(BF16)
    wr_lo = (wr_pad - wr_hi.astype(F32)).astype(BF16)
    x1, h2, ridx, rw, cnt = _out_router(
        ya, yb, x2d, mod3, w_out[l].astype(BF16), norm2_w[l][None, :], wr_hi, wr_lo,
        _pad_lanes(b_router[l]), seq)

    t = MOE_BLOCK
    n_assign = n * TOP_K
    n_blocks = n_assign // t + N_EXPERTS
    e_flat = ridx[:, :TOP_K].T.reshape(n_assign)
    keys = e_flat * n_assign + jnp.arange(n_assign, dtype=jnp.int32)
    asg = jnp.sort(keys) % n_assign
    counts = cnt[0, :N_EXPERTS].astype(jnp.int32)
    start = jnp.cumsum(counts) - counts
    nblk = (counts + t - 1) // t
    blk_end = jnp.cumsum(nblk)
    blk_start = blk_end - nblk
    nvb = blk_end[-1]
    bid = jnp.arange(-1, n_blocks + 2, dtype=jnp.int32)
    live = (bid >= 0) & (bid < nvb)
    e_of = jnp.minimum(jnp.sum(blk_end[None, :] <= jnp.clip(bid, 0, nvb - 1)[:, None], axis=1),
                       N_EXPERTS - 1).astype(jnp.int32)
    j_of = bid - blk_start[e_of]
    row_start = jnp.where(live, start[e_of] + j_of * t, 0).astype(jnp.int32)
    n_valid = jnp.where(live, jnp.clip(counts[e_of] - j_of * t, 0, t), 0).astype(jnp.int32)
    block_e = e_of[1:n_blocks + 1]
    lane = jnp.arange(t, dtype=jnp.int32)
    a_blk = asg[jnp.minimum(row_start[:, None] + lane[None, :], n_assign - 1)]
    dst_rows = jnp.where(lane[None, :] < n_valid[:, None], a_blk, n_assign + lane[None, :])
    block_rows = jnp.concatenate([a_blk % n, dst_rows], axis=1).reshape(-1)

    y4 = _moe_ffn(block_e, nvb.reshape(1).astype(jnp.int32), block_rows, h2, w_gu[l],
                  b_gu[l][:, None, :], w_down[l], b_down[l][:, None, :], n_blocks, n_assign + t)
    out = _combine(y4, x1, rw, mod3, norm_f_w[None, :], seq)
    return out.reshape(bsz, seq, d)
```
